```python
import math
import jax, jax.numpy as jnp
from jax import lax
import numpy as np

D_MODEL = 4096
BATCH = 16
SEQ = 256
DEPTH = 2
DEC_BATCH = 4
DEC_SEQ = 2048
PAST_LEN = 256

GRID_W = 64
DA_D = 128
DA_W = D_MODEL // 2
DA_HEADS = DA_W // (2 * DA_D)
RW_W = D_MODEL // 4
RW_HEAD = 64
RW_HEADS = RW_W // RW_HEAD
RW_DECAY_R = 64
RW_A_R = 64
RW_GATE_R = 160
RW_IN = 3 * RW_W + 2 * RW_DECAY_R + 2 * RW_A_R + RW_GATE_R
RW_GN_EPS = 64e-5
CM_W = D_MODEL // 4
CM_GROUPS = 4
CHUNK = 128
MIX_W = DA_W + RW_W + CM_W
P_IN = 3 * DA_W + RW_IN + 2 * CM_W
D_FF = 11008
CONV_W = 3
Q_BLOCK = 128
ROPE_THETA = 10000.0
NORM_EPS = 1e-6

kernel_name = 'hybrid_diffattn_rwkv7_chunkmlp_prefix_ctx_step'


def _rmsnorm(x, g):
    xf = x.astype(jnp.float32)
    y = xf * lax.rsqrt(jnp.mean(xf * xf, axis=-1, keepdims=True) + NORM_EPS)
    return y.astype(x.dtype) * g


def _split(x, sizes):
    idx = [int(i) for i in np.cumsum(sizes)[:-1]]
    return jnp.split(x, idx, axis=-1)


def _centred_dwconv(h, w):
    T = h.shape[1]
    pad = CONV_W // 2
    hp = jnp.pad(h, ((0, 0), (pad, pad), (0, 0)))
    return sum(hp[:, j:j + T] * w[j] for j in range(CONV_W))


def _rope_1d(x, pos):
    n = x.shape[-1] // 2
    inv = ROPE_THETA ** (-jnp.arange(n, dtype=jnp.float32) / n)
    ang = pos.astype(jnp.float32)[:, None] * inv[None, :]
    cos = jnp.cos(ang)[None, :, None, None, :]
    sin = jnp.sin(ang)[None, :, None, None, :]
    xf = x.astype(jnp.float32)
    x1, x2 = xf[..., :n], xf[..., n:]
    return jnp.concatenate([x1 * cos - x2 * sin, x2 * cos + x1 * sin], axis=-1)


def _axial_rope(x):
    T = x.shape[1]
    rows = T // GRID_W
    row = jnp.repeat(jnp.arange(rows), GRID_W)
    col = jnp.tile(jnp.arange(GRID_W), rows)
    half = x.shape[-1] // 2
    y = jnp.concatenate([_rope_1d(x[..., :half], row), _rope_1d(x[..., half:], col)], axis=-1)
    return y.astype(x.dtype)


def _diff_attention(q, k, v, lam):
    B, T, H = q.shape[:3]
    nb = T // Q_BLOCK
    qb = q.reshape(B, nb, Q_BLOCK, H, 2, DA_D).swapaxes(0, 1)
    scale = DA_D ** -0.5

    def one_block(qblk):
        s = jnp.einsum('bqhcd,bkhcd->cbhqk', qblk, k, preferred_element_type=jnp.float32) * scale
        p = jax.nn.softmax(s, axis=-1)
        a = p[0] - lam * p[1]
        return jnp.einsum('bhqk,bkhe->bqhe', a.astype(v.dtype), v)

    o = lax.map(one_block, qb)
    return o.swapaxes(0, 1).reshape(B, T, H, 2 * DA_D)


def _wkv_scan(s0, r, w, kk, a, k, v, reverse):
    def step(S, inp):
        r_t, w_t, kk_t, a_t, k_t, v_t = inp
        sa = jnp.einsum('bhvk,bhk->bhv', S, kk_t)
        S = (S * w_t[:, :, None, :]
             - sa[..., None] * (kk_t * a_t)[:, :, None, :]
             + v_t[..., None] * k_t[:, :, None, :])
        y = jnp.einsum('bhvk,bhk->bhv', S, r_t)
        return S, y

    xs = tuple(t.swapaxes(0, 1) for t in (r, w, kk, a, k, v))
    S, ys = lax.scan(step, s0.astype(jnp.float32), xs, reverse=reverse)
    return S, ys.swapaxes(0, 1)


def _rwkv7(z, s0_f, s0_b, lp):
    B, T, _ = z.shape
    odt = z.dtype
    z = z.astype(jnp.float32)
    r, k, v, dec, aa, gl = _split(z, [RW_W, RW_W, RW_W, 2 * RW_DECAY_R, 2 * RW_A_R, RW_GATE_R])
    dec = dec.reshape(B, T, 2, RW_DECAY_R)
    aa = aa.reshape(B, T, 2, RW_A_R)
    wl = lp['rw_w0'] + jnp.einsum('btdr,drc->btdc', jnp.tanh(dec), lp['rw_w2'])
    decay = jnp.exp(-jnp.exp(-jax.nn.softplus(-wl) - 0.5))
    a = jax.nn.sigmoid(lp['rw_a0'] + jnp.einsum('btdr,drc->btdc', aa, lp['rw_a2']))
    gate = jax.nn.sigmoid(gl) @ lp['rw_g2']
    heads = lambda t: t.reshape(*t.shape[:-1], RW_HEADS, RW_HEAD)
    kk = heads(k * lp['rw_k_k'])
    kk = kk * lax.rsqrt(jnp.sum(kk * kk, axis=-1, keepdims=True) + 1e-12)
    kd = heads(k[:, :, None, :] * (1.0 + (a - 1.0) * lp['rw_k_a']))
    wd, ad = heads(decay), heads(a)
    rh, vh = heads(r), heads(v)
    s_f, y_f = _wkv_scan(s0_f, rh, wd[:, :, 0], kk, ad[:, :, 0], kd[:, :, 0], vh, False)
    s_b, y_b = _wkv_scan(s0_b, rh, wd[:, :, 1], kk, ad[:, :, 1], kd[:, :, 1], vh, True)
    y = y_f + y_b
    mu = jnp.mean(y, axis=-1, keepdims=True)
    var = jnp.mean(jnp.square(y - mu), axis=-1, keepdims=True)
    y = ((y - mu) * lax.rsqrt(var + RW_GN_EPS)).reshape(B, T, RW_W) * lp['rw_gn_g'] + lp['rw_gn_b']
    bonus = jnp.sum(rh * (kd[:, :, 0] + kd[:, :, 1]) * lp['rw_r_k'], axis=-1, keepdims=True) * vh
    y = (y + bonus.reshape(B, T, RW_W)) * gate
    return y.astype(odt), s_f.astype(odt), s_b.astype(odt)


def _chunk_mlp(uv, gain, ws, bs):
    u, v = jnp.split(uv, 2, axis=-1)
    B, T, _ = u.shape
    z = _rmsnorm(v, gain).reshape(B, T // CHUNK, CHUNK, CM_GROUPS, CM_W // CM_GROUPS)
    z = jnp.einsum('gpq,bnqgc->bnpgc', ws, z) + bs.T[:, :, None]
    return u * z.reshape(B, T, CM_W)


def _layer(x, cond, ctx, layer_idx, lp):
    B, T, _ = x.shape
    mod = jax.nn.silu(cond) @ lp['mod_w'] + lp['mod_b']
    sh1, sc1, g1, sh2, sc2, g2 = jnp.split(mod[:, None, :], 6, axis=-1)
    h = _rmsnorm(x, lp['norm1_g']) * (1.0 + sc1) + sh1
    proj = h @ lp['w_in']
    q, k, v, z_rw, uv = _split(proj, [DA_W, DA_W, DA_W, RW_IN, 2 * CM_W])
    q = q.reshape(B, T, DA_HEADS, 2, DA_D)
    k = k.reshape(B, T, DA_HEADS, 2, DA_D)
    v = v.reshape(B, T, DA_HEADS, 2 * DA_D)
    if ctx is None:
        k_all, v_all = k, v
        s0_f = jnp.zeros((B, RW_HEADS, RW_HEAD, RW_HEAD), jnp.float32)
        s0_b = s0_f
    else:
        k_ctx, v_ctx, s0_f, s0_b = ctx
        q = _axial_rope(q)
        k_all = jnp.concatenate([_axial_rope(k), k_ctx.astype(k.dtype)], axis=1)
        v_all = jnp.concatenate([v, v_ctx.astype(v.dtype)], axis=1)
    lam_init = 0.8 - 0.6 * math.exp(-0.3 * layer_idx)
    lq1, lk1, lq2, lk2 = lp['da_lambda'].astype(jnp.float32)
    lam = jnp.exp(jnp.sum(lq1 * lk1)) - jnp.exp(jnp.sum(lq2 * lk2)) + lam_init
    y_da = _diff_attention(q, k_all, v_all, lam)
    y_da = (_rmsnorm(y_da, lp['da_subln_g']) * (1.0 - lam_init)).reshape(B, T, DA_W)
    y_rw, s_f, s_b = _rwkv7(_centred_dwconv(z_rw, lp['rw_conv_w']), s0_f, s0_b, lp)
    y_cm = _chunk_mlp(uv, lp['cm_norm_g'], lp['cm_ws'], lp['cm_bs'])
    mix = jnp.concatenate([y_da.astype(x.dtype), y_rw.astype(x.dtype), y_cm.astype(x.dtype)], axis=-1)
    x = x + g1 * (mix @ lp['w_out'])
    h = _rmsnorm(x, lp['norm2_g']) * (1.0 + sc2) + sh2
    up = _centred_dwconv(h @ lp['ffn_up'], lp['ffn_conv_w']) + lp['ffn_conv_b']
    ga, gb = jnp.split(up, 2, axis=-1)
    x = x + g2 * ((jax.nn.silu(ga) * gb) @ lp['ffn_down'])
    return x, (k, v, s_f, s_b)


def setup_inputs(seed: int = 0) -> dict:
    key = jax.random.key(seed)
    ks = iter(jax.random.split(key, 48))
    nrm = lambda shape, s: jax.random.normal(next(ks), shape, jnp.float32) * s
    gain = lambda shape: 1.0 + nrm(shape, 0.02)
    L = DEPTH
    d = {}
    d['x_prompt'] = nrm((BATCH, SEQ, D_MODEL), 1.0)
    d['x_sample'] = nrm((DEC_BATCH, DEC_SEQ, D_MODEL), 1.0)
    d['cache_da_k'] = nrm((DEC_BATCH, L, PAST_LEN, DA_HEADS, 2, DA_D), 1.0)
    d['cache_da_v'] = nrm((DEC_BATCH, L, PAST_LEN, DA_HEADS, 2 * DA_D), 1.0)
    d['state_rwkv'] = nrm((DEC_BATCH, L, 2, RW_HEADS, RW_HEAD, RW_HEAD), 0.3)
    d['c'] = nrm((DEC_BATCH, D_MODEL), 1.0)
    d['c_ctx'] = nrm((D_MODEL,), 1.0)
    d['mod_w'] = nrm((L, D_MODEL, 6 * D_MODEL), 0.5 * D_MODEL ** -0.5)
    d['mod_b'] = nrm((L, 6 * D_MODEL), 0.02)
    d['norm1_g'] = gain((L, D_MODEL))
    d['norm2_g'] = gain((L, D_MODEL))
    d['w_in'] = nrm((L, D_MODEL, P_IN), D_MODEL ** -0.5)
    d['da_lambda'] = nrm((L, 4, DA_D), 0.1)
    d['da_subln_g'] = gain((L, 2 * DA_D))
    d['rw_conv_w'] = nrm((L, CONV_W, RW_IN), CONV_W ** -0.5)
    d['rw_w0'] = nrm((L, 2, RW_W), 0.5) - 1.0
    d['rw_w2'] = nrm((L, 2, RW_DECAY_R, RW_W), 0.5 * RW_DECAY_R ** -0.5)
    d['rw_a0'] = nrm((L, 2, RW_W), 0.1)
    d['rw_a2'] = nrm((L, 2, RW_A_R, RW_W), 0.5 * RW_A_R ** -0.5)
    d['rw_g2'] = nrm((L, RW_GATE_R, RW_W), RW_GATE_R ** -0.5)
    d['rw_k_k'] = 0.85 + nrm((L, RW_W), 0.02)
    d['rw_k_a'] = gain((L, RW_W))
    d['rw_r_k'] = nrm((L, RW_HEADS, RW_HEAD), 0.1)
    d['rw_gn_g'] = gain((L, RW_W))
    d['rw_gn_b'] = nrm((L, RW_W), 0.02)
    d['cm_norm_g'] = gain((L, CM_W))
    d['cm_ws'] = nrm((L, CM_GROUPS, CHUNK, CHUNK), CHUNK ** -0.5)
    d['cm_bs'] = nrm((L, CM_GROUPS, CHUNK), 0.02)
    d['w_out'] = nrm((L, MIX_W, D_MODEL), MIX_W ** -0.5)
    d['ffn_up'] = nrm((L, D_MODEL, 2 * D_FF), D_MODEL ** -0.5)
    d['ffn_conv_w'] = nrm((L, CONV_W, 2 * D_FF), CONV_W ** -0.5)
    d['ffn_conv_b'] = nrm((L, 2 * D_FF), 0.02)
    d['ffn_down'] = nrm((L, D_FF, D_MODEL), D_FF ** -0.5)
    d['final_norm_g'] = gain((D_MODEL,))
    return d


def reference(x_prompt, x_sample, cache_da_k, cache_da_v, state_rwkv, c, c_ctx,
              mod_w, mod_b, norm1_g, norm2_g, w_in, da_lambda, da_subln_g,
              rw_conv_w, rw_w0, rw_w2, rw_a0, rw_a2, rw_g2, rw_k_k, rw_k_a, rw_r_k,
              rw_gn_g, rw_gn_b, cm_norm_g, cm_ws, cm_bs, w_out,
              ffn_up, ffn_conv_w, ffn_conv_b, ffn_down, final_norm_g):
    xp, xs = x_prompt, x_sample
    new_k, new_v, new_s = [], [], []
    for l in range(DEPTH):
        lp = dict(mod_w=mod_w[l], mod_b=mod_b[l], norm1_g=norm1_g[l], norm2_g=norm2_g[l],
                  w_in=w_in[l], da_lambda=da_lambda[l], da_subln_g=da_subln_g[l],
                  rw_conv_w=rw_conv_w[l], rw_w0=rw_w0[l], rw_w2=rw_w2[l], rw_a0=rw_a0[l],
                  rw_a2=rw_a2[l], rw_g2=rw_g2[l], rw_k_k=rw_k_k[l], rw_k_a=rw_k_a[l],
                  rw_r_k=rw_r_k[l], rw_gn_g=rw_gn_g[l], rw_gn_b=rw_gn_b[l],
                  cm_norm_g=cm_norm_g[l], cm_ws=cm_ws[l], cm_bs=cm_bs[l], w_out=w_out[l],
                  ffn_up=ffn_up[l], ffn_conv_w=ffn_conv_w[l], ffn_conv_b=ffn_conv_b[l],
                  ffn_down=ffn_down[l])
        xp, (k_l, v_l, sf_l, sb_l) = _layer(xp, c_ctx[None, :], None, l, lp)
        new_k.append(k_l)
        new_v.append(v_l)
        new_s.append(jnp.stack([sf_l, sb_l], axis=1))
        ctx = (cache_da_k[:, l], cache_da_v[:, l], state_rwkv[:, l, 0], state_rwkv[:, l, 1])
        xs, _ = _layer(xs, c, ctx, l, lp)
    y_prompt = _rmsnorm(xp, final_norm_g)
    y_sample = _rmsnorm(xs, final_norm_g)
    new_cache_da_k = jnp.stack(new_k, axis=1)
    new_cache_da_v = jnp.stack(new_v, axis=1)
    new_state_rwkv = jnp.stack(new_s, axis=1)
    return (y_prompt, y_sample, new_cache_da_k, new_cache_da_v, new_state_rwkv)
```

```python
import functools
import math

import jax
import jax.numpy as jnp
from jax import lax
from jax.experimental import pallas as pl
from jax.experimental.pallas import tpu as pltpu

F32 = jnp.float32
BF16 = jnp.bfloat16
HIGHEST = lax.Precision.HIGHEST

LANES = 128
SUBLANES = 8
VMEM_BYTES_V7X = 64 * 1024 * 1024
VMEM_BUDGET = VMEM_BYTES_V7X * 3 // 4

GRID_W = 64
DA_D = 128
RW_HEAD = 64
RW_LORA_R = 64
RW_GATE_R = 160
RW_GN_EPS = 64e-5
CM_GROUPS = 4
CM_CHUNK = 128
CONV_W = 3
ROPE_THETA = 10000.0
NORM_EPS = 1e-6
SCAN_CHUNK = 64
SCAN_HEADS = 4


def _params(*sem):
    return pltpu.CompilerParams(dimension_semantics=sem, vmem_limit_bytes=VMEM_BUDGET)


def _cond_row(tok0, p_tok, ts):
    return jnp.where(tok0 < p_tok, 0, 1 + jnp.maximum(tok0 - p_tok, 0) // ts)


def _mod_kernel(c_ref, w_ref, b_ref, o_ref):
    c = c_ref[...]
    s = (c * jax.nn.sigmoid(c)).astype(BF16)
    o_ref[...] = jnp.dot(s, w_ref[...].astype(BF16), preferred_element_type=F32) + b_ref[...]


def _modulation(cond8, mod_w, mod_b, tn=512):
    L, D, N = mod_w.shape
    return pl.pallas_call(
        _mod_kernel,
        grid=(L, N // tn),
        in_specs=[pl.BlockSpec((SUBLANES, D), lambda l, j: (0, 0)),
                  pl.BlockSpec((None, D, tn), lambda l, j: (l, 0, j)),
                  pl.BlockSpec((None, 1, tn), lambda l, j: (l, 0, j))],
        out_specs=pl.BlockSpec((None, SUBLANES, tn), lambda l, j: (l, 0, j)),
        out_shape=jax.ShapeDtypeStruct((L, SUBLANES, N), F32),
        compiler_params=_params("parallel", "parallel"),
        name="modulation",
    )(cond8, mod_w, mod_b.reshape(L, 1, N))


def _norm_mod_kernel(x_ref, g_ref, sh_ref, sc_ref, o_ref, *, tm, p_tok, ts):
    row = _cond_row(pl.program_id(0) * tm, p_tok, ts)
    x = x_ref[...]
    y = x * lax.rsqrt(jnp.mean(x * x, axis=-1, keepdims=True) + NORM_EPS)
    sc = sc_ref[pl.ds(row, 1), :]
    sh = sh_ref[pl.ds(row, 1), :]
    o_ref[...] = ((y * g_ref[...]) * (1.0 + sc) + sh).astype(o_ref.dtype)


def _norm_mod(x, g, mod, layer, k_sh, k_sc, p_tok, ts, tm=256):
    n, D = x.shape
    return pl.pallas_call(
        functools.partial(_norm_mod_kernel, tm=tm, p_tok=p_tok, ts=ts),
        grid=(n // tm,),
        in_specs=[pl.BlockSpec((tm, D), lambda i: (i, 0)),
                  pl.BlockSpec((1, D), lambda i: (0, 0)),
                  pl.BlockSpec((None, SUBLANES, D), lambda i: (layer, 0, k_sh)),
                  pl.BlockSpec((None, SUBLANES, D), lambda i: (layer, 0, k_sc))],
        out_specs=pl.BlockSpec((tm, D), lambda i: (i, 0)),
        out_shape=jax.ShapeDtypeStruct((n, D), BF16),
        compiler_params=_params("parallel"),
        name="norm_mod",
    )(x, g.reshape(1, D), mod, mod)


def _final_norm_kernel(x_ref, g_ref, o_ref):
    x = x_ref[...]
    o_ref[...] = x * lax.rsqrt(jnp.mean(x * x, axis=-1, keepdims=True) + NORM_EPS) * g_ref[...]


def _final_norm(x, g, tm=256):
    n, D = x.shape
    return pl.pallas_call(
        _final_norm_kernel,
        grid=(n // tm,),
        in_specs=[pl.BlockSpec((tm, D), lambda i: (i, 0)), pl.BlockSpec((1, D), lambda i: (0, 0))],
        out_specs=pl.BlockSpec((tm, D), lambda i: (i, 0)),
        out_shape=jax.ShapeDtypeStruct((n, D), F32),
        compiler_params=_params("parallel"),
        name="final_norm",
    )(x, g.reshape(1, D))


def _mm_kernel(*refs, nk, resid, tm, p_tok, ts):
    if resid:
        a_ref, b_ref, x_ref, g_ref, o_ref = refs[:5]
    else:
        a_ref, b_ref, o_ref = refs[:3]
    acc_ref = refs[-1] if nk > 1 else None

    def finish(acc):
        if resid:
            row = _cond_row(pl.program_id(0) * tm, p_tok, ts)
            o_ref[...] = x_ref[...] + g_ref[pl.ds(row, 1), :] * acc
        else:
            o_ref[...] = acc

    part = jnp.dot(a_ref[...], b_ref[...], preferred_element_type=F32)
    if nk == 1:
        finish(part)
    else:
        k = pl.program_id(2)

        @pl.when(k == 0)
        def _():
            acc_ref[...] = part

        @pl.when((k > 0) & (k < nk - 1))
        def _():
            acc_ref[...] += part

        @pl.when(k == nk - 1)
        def _():
            finish(acc_ref[...] + part)


def _matmul(a, b, *, tm, tn, tk=None, resid=None, name):
    M, K = a.shape
    N = b.shape[1]
    tk = K if tk is None else tk
    nk = K // tk
    in_specs = [pl.BlockSpec((tm, tk), lambda i, j, k: (i, k)),
                pl.BlockSpec((tk, tn), lambda i, j, k: (k, j))]
    args = [a, b]
    p_tok = ts = 0
    if resid is not None:
        x, mod, layer, gate_blk, p_tok, ts = resid
        in_specs += [pl.BlockSpec((tm, tn), lambda i, j, k: (i, j)),
                     pl.BlockSpec((None, SUBLANES, tn), lambda i, j, k: (layer, 0, gate_blk + j))]
        args += [x, mod]
    return pl.pallas_call(
        functools.partial(_mm_kernel, nk=nk, resid=resid is not None, tm=tm, p_tok=p_tok, ts=ts),
        grid=(M // tm, N // tn, nk),
        in_specs=in_specs,
        out_specs=pl.BlockSpec((tm, tn), lambda i, j, k: (i, j)),
        out_shape=jax.ShapeDtypeStruct((M, N), F32),
        scratch_shapes=[pltpu.VMEM((tm, tn), F32)] if nk > 1 else [],
        compiler_params=_params("parallel", "parallel", "arbitrary"),
        name=name,
    )(*args)


def _rope(x, cos, sin_signed):
    lane = lax.broadcasted_iota(jnp.int32, x.shape, 1)
    width = x.shape[1]
    rot = jnp.where((lane % 64) < 32, pltpu.roll(x, width - 32, 1), pltpu.roll(x, 32, 1))
    return x * cos + rot * sin_signed


def _attn_kernel(*refs, rope, lam_init):
    if rope:
        (lam_ref, q_ref, k_ref, v_ref, kc_ref, vc_ref, cq_ref, sq_ref, ck_ref, sk_ref, g_ref,
         o_ref, kr_ref) = refs
    else:
        lam_ref, q_ref, k_ref, v_ref, g_ref, o_ref = refs
    lm = lam_ref[...]
    s1 = jnp.sum(lm[0:1] * lm[1:2], axis=-1, keepdims=True)
    s2 = jnp.sum(lm[2:3] * lm[3:4], axis=-1, keepdims=True)
    lam = jnp.exp(s1) - jnp.exp(s2) + lam_init
    scale = DA_D ** -0.5
    nt = (((1,), (1,)), ((), ()))

    if rope:
        @pl.when(pl.program_id(2) == 0)
        def _():
            kr_ref[...] = _rope(k_ref[...], ck_ref[...], sk_ref[...]).astype(BF16)

        q = _rope(q_ref[...], cq_ref[...], sq_ref[...]).astype(BF16)
        keys = [kr_ref[...], kc_ref[...].astype(BF16)]
        vals = [v_ref[...].astype(BF16), vc_ref[...].astype(BF16)]
    else:
        q = q_ref[...].astype(BF16)
        keys = [k_ref[...].astype(BF16)]
        vals = [v_ref[...].astype(BF16)]

    es, rs = [], []
    for c in range(2):
        qc = q[:, c * DA_D:(c + 1) * DA_D]
        ss = [lax.dot_general(qc, kk[:, c * DA_D:(c + 1) * DA_D], nt, preferred_element_type=F32) * scale
              for kk in keys]
        m = functools.reduce(jnp.maximum, [jnp.max(s, axis=-1, keepdims=True) for s in ss])
        e = [jnp.exp(s - m) for s in ss]
        d = functools.reduce(jnp.add, [jnp.sum(x, axis=-1, keepdims=True) for x in e])
        es.append(e)
        rs.append(1.0 / d)
    r0 = rs[0]
    r1 = rs[1] * lam
    o = None
    for j in range(len(keys)):
        a = (es[0][j] * r0 - es[1][j] * r1).astype(BF16)
        t = jnp.dot(a, vals[j], preferred_element_type=F32)
        o = t if o is None else o + t
    y = o * lax.rsqrt(jnp.mean(o * o, axis=-1, keepdims=True) + NORM_EPS)
    o_ref[...] = (y * g_ref[...] * (1.0 - lam_init)).astype(o_ref.dtype)


def _attention(proj, lam_p, g, lam_init, *, row0, B, T, H, ctx=None, tq=256):
    W = 2 * DA_D
    nq = T // tq
    qb0 = row0 // tq
    kb0 = row0 // T
    rope = ctx is not None
    in_specs = [pl.BlockSpec((4, DA_D), lambda b, h, i: (0, 0)),
                pl.BlockSpec((tq, W), lambda b, h, i: (qb0 + b * nq + i, h)),
                pl.BlockSpec((T, W), lambda b, h, i: (kb0 + b, H + h)),
                pl.BlockSpec((T, W), lambda b, h, i: (kb0 + b, 2 * H + h))]
    args = [lam_p, proj, proj, proj]
    scratch = []
    if rope:
        ck, cv, layer, cos, sin = ctx
        past = ck.shape[2]
        in_specs += [pl.BlockSpec((None, None, past, W), lambda b, h, i: (b, layer, 0, h)),
                     pl.BlockSpec((None, None, past, W), lambda b, h, i: (b, layer, 0, h)),
                     pl.BlockSpec((tq, W), lambda b, h, i: (i, 0)),
                     pl.BlockSpec((tq, W), lambda b, h, i: (i, 0)),
                     pl.BlockSpec((T, W), lambda b, h, i: (0, 0)),
                     pl.BlockSpec((T, W), lambda b, h, i: (0, 0))]
        args += [ck, cv, cos, sin, cos, sin]
        scratch = [pltpu.VMEM((T, W), BF16)]
    in_specs.append(pl.BlockSpec((1, W), lambda b, h, i: (0, 0)))
    args.append(g.reshape(1, W))
    return pl.pallas_call(
        functools.partial(_attn_kernel, rope=rope, lam_init=lam_init),
        grid=(B, H, nq),
        in_specs=in_specs,
        out_specs=pl.BlockSpec((tq, W), lambda b, h, i: (b * nq + i, h)),
        out_shape=jax.ShapeDtypeStruct((B * T, H * W), BF16),
        scratch_shapes=scratch,
        compiler_params=_params("parallel", "parallel", "arbitrary"),
        name="diff_attn_ctx" if rope else "diff_attn",
    )(*args)


def _seq_edges(tok0, tm, p_tok, tp, ts):
    t = tok0 + lax.broadcasted_iota(jnp.int32, (tm, 1), 0)
    in_p = t < p_tok
    pos = jnp.where(in_p, t % tp, jnp.maximum(t - p_tok, 0) % ts)
    length = jnp.where(in_p, tp, ts)
    return pos == 0, pos == length - 1


def _conv3(x, prev8, next8, w, first, last):
    tm = x.shape[0]
    ridx = lax.broadcasted_iota(jnp.int32, (tm, 1), 0)
    xp = jnp.where(ridx == 0, prev8[SUBLANES - 1:SUBLANES], pltpu.roll(x, 1, 0))
    xp = jnp.where(first, 0.0, xp)
    xn = jnp.where(ridx == tm - 1, next8[0:1], pltpu.roll(x, tm - 1, 0))
    xn = jnp.where(last, 0.0, xn)
    return xp * w[0:1] + x * w[1:2] + xn * w[2:3]


def _halo_specs(tm, width, col_blk, n_tok, nidx=1):
    r = tm // SUBLANES
    last = n_tok // SUBLANES - 1
    if nidx == 1:
        return [pl.BlockSpec((tm, width), lambda i: (i, col_blk)),
                pl.BlockSpec((SUBLANES, width), lambda i: (jnp.maximum(i * r - 1, 0), col_blk)),
                pl.BlockSpec((SUBLANES, width), lambda i: (jnp.minimum((i + 1) * r, last), col_blk))]
    return [pl.BlockSpec((tm, width), lambda i, j: (i, col_blk + j)),
            pl.BlockSpec((SUBLANES, width), lambda i, j: (jnp.maximum(i * r - 1, 0), col_blk + j)),
            pl.BlockSpec((SUBLANES, width), lambda i, j: (jnp.minimum((i + 1) * r, last), col_blk + j))]


def _cmlp_kernel(u_ref, v_ref, gain_ref, ws_ref, bs_ref, o_ref, *, tm):
    v = v_ref[...]
    z = (v * lax.rsqrt(jnp.mean(v * v, axis=-1, keepdims=True) + NORM_EPS) * gain_ref[...]).astype(BF16)
    gw = z.shape[1] // CM_GROUPS
    for n in range(tm // CM_CHUNK):
        rows = slice(n * CM_CHUNK, (n + 1) * CM_CHUNK)
        for g in range(CM_GROUPS):
            cols = slice(g * gw, (g + 1) * gw)
            t = jnp.dot(ws_ref[g].astype(BF16), z[rows, cols], preferred_element_type=F32) + bs_ref[g]
            o_ref[rows, cols] = (u_ref[rows, cols] * t).astype(o_ref.dtype)


def _chunk_mlp(proj, gain, ws, bs, *, u_blk, tm=256):
    n_tok = proj.shape[0]
    W = gain.shape[0]
    gw = W // CM_GROUPS
    bs_b = jnp.broadcast_to(bs[:, :, None], (CM_GROUPS, CM_CHUNK, gw))
    return pl.pallas_call(
        functools.partial(_cmlp_kernel, tm=tm),
        grid=(n_tok // tm,),
        in_specs=[pl.BlockSpec((tm, W), lambda i: (i, u_blk)),
                  pl.BlockSpec((tm, W), lambda i: (i, u_blk + 1)),
                  pl.BlockSpec((1, W), lambda i: (0, 0)),
                  pl.BlockSpec((CM_GROUPS, CM_CHUNK, CM_CHUNK), lambda i: (0, 0, 0)),
                  pl.BlockSpec((CM_GROUPS, CM_CHUNK, gw), lambda i: (0, 0, 0))],
        out_specs=pl.BlockSpec((tm, W), lambda i: (i, 0)),
        out_shape=jax.ShapeDtypeStruct((n_tok, W), BF16),
        compiler_params=_params("parallel"),
        name="chunk_mlp",
    )(proj, proj, gain.reshape(1, W), ws, bs_b)


def _ffn_act_kernel(a_ref, ap_ref, an_ref, b_ref, bp_ref, bn_ref, wa_ref, wb_ref, ba_ref, bb_ref, o_ref,
                    *, tm, p_tok, tp, ts):
    first, last = _seq_edges(pl.program_id(0) * tm, tm, p_tok, tp, ts)
    ga = _conv3(a_ref[...], ap_ref[...], an_ref[...], wa_ref[...], first, last) + ba_ref[...]
    gb = _conv3(b_ref[...], bp_ref[...], bn_ref[...], wb_ref[...], first, last) + bb_ref[...]
    o_ref[...] = (ga * jax.nn.sigmoid(ga) * gb).astype(o_ref.dtype)


def _ffn_act(up, conv_w, conv_b, *, p_tok, tp, ts, tm=512, tc=256):
    n_tok, two_f = up.shape
    F = two_f // 2
    nb = F // tc
    cb = conv_b.reshape(1, two_f)
    wspec = lambda off: pl.BlockSpec((CONV_W, tc), lambda i, j: (0, off + j))
    bspec = lambda off: pl.BlockSpec((1, tc), lambda i, j: (0, off + j))
    return pl.pallas_call(
        functools.partial(_ffn_act_kernel, tm=tm, p_tok=p_tok, tp=tp, ts=ts),
        grid=(n_tok // tm, nb),
        in_specs=(_halo_specs(tm, tc, 0, n_tok, 2) + _halo_specs(tm, tc, nb, n_tok, 2)
                  + [wspec(0), wspec(nb), bspec(0), bspec(nb)]),
        out_specs=pl.BlockSpec((tm, tc), lambda i, j: (i, j)),
        out_shape=jax.ShapeDtypeStruct((n_tok, F), BF16),
        compiler_params=_params("parallel", "parallel"),
        name="ffn_act",
    )(up, up, up, up, up, up, conv_w, conv_w, cb, cb)


def _head_sum(x):
    r = lax.broadcasted_iota(jnp.int32, (LANES, LANES), 0) // RW_HEAD
    c = lax.broadcasted_iota(jnp.int32, (LANES, LANES), 1) // RW_HEAD
    e = (r == c).astype(F32)
    cols = [jnp.dot(x[:, j * LANES:(j + 1) * LANES], e, precision=HIGHEST, preferred_element_type=F32)
            for j in range(x.shape[1] // LANES)]
    return jnp.concatenate(cols, axis=1)


def _rw_prep_kernel(r_ref, rp_ref, rn_ref, k_ref, kp_ref, kn_ref, v_ref, vp_ref, vn_ref,
                    z_ref, zp_ref, zn_ref, cw_ref, cz_ref, w0_ref, w2_ref, a0_ref, a2_ref, g2_ref,
                    kk_ref, ka_ref, rk_ref,
                    ro_ref, kko_ref, vo_ref, lw_ref, b_ref, kd_ref, gate_ref, bonus_ref,
                    *, tm, p_tok, tp, ts, C):
    first, last = _seq_edges(pl.program_id(0) * tm, tm, p_tok, tp, ts)
    cw = cw_ref[...]
    r = _conv3(r_ref[...], rp_ref[...], rn_ref[...], cw[:, 0:C], first, last)
    k = _conv3(k_ref[...], kp_ref[...], kn_ref[...], cw[:, C:2 * C], first, last)
    v = _conv3(v_ref[...], vp_ref[...], vn_ref[...], cw[:, 2 * C:3 * C], first, last)
    z = _conv3(z_ref[...], zp_ref[...], zn_ref[...], cz_ref[...], first, last)
    dec = jnp.tanh(z[:, 0:LANES]).astype(BF16)
    aa = z[:, LANES:2 * LANES].astype(BF16)
    gl = jax.nn.sigmoid(z[:, 2 * LANES:4 * LANES]).astype(BF16)
    gate_ref[...] = jnp.dot(gl, g2_ref[...].astype(BF16), preferred_element_type=F32)
    kk = k * kk_ref[...]
    kk = kk * lax.rsqrt(_head_sum(kk * kk) + 1e-12)
    ro_ref[...] = r
    kko_ref[...] = kk
    vo_ref[...] = v
    kd_sum = None
    for d in range(2):
        wl = w0_ref[d:d + 1, :] + jnp.dot(dec, w2_ref[d].astype(BF16), preferred_element_type=F32)
        lw_ref[d] = -jnp.exp(-jax.nn.softplus(-wl) - 0.5)
        a = jax.nn.sigmoid(a0_ref[d:d + 1, :] + jnp.dot(aa, a2_ref[d].astype(BF16), preferred_element_type=F32))
        b_ref[d] = kk * a
        kd = k * (1.0 + (a - 1.0) * ka_ref[...])
        kd_ref[d] = kd
        kd_sum = kd if kd_sum is None else kd_sum + kd
    bonus_ref[...] = _head_sum(r * kd_sum * rk_ref[...]) * v


def _rw_prep(proj, lp, *, r_blk, z_blk, p_tok, tp, ts, tm=256):
    n_tok = proj.shape[0]
    C = lp["rw_k_k"].shape[0]
    ZW = 4 * LANES
    full = lambda shape: pl.BlockSpec(shape, lambda i: (0,) * len(shape))
    tok = pl.BlockSpec((tm, C), lambda i: (i, 0))
    tok2 = pl.BlockSpec((2, tm, C), lambda i: (0, i, 0))
    one = jax.ShapeDtypeStruct((n_tok, C), F32)
    two = jax.ShapeDtypeStruct((2, n_tok, C), F32)
    return pl.pallas_call(
        functools.partial(_rw_prep_kernel, tm=tm, p_tok=p_tok, tp=tp, ts=ts, C=C),
        grid=(n_tok // tm,),
        in_specs=(_halo_specs(tm, C, r_blk, n_tok) + _halo_specs(tm, C, r_blk + 1, n_tok)
                  + _halo_specs(tm, C, r_blk + 2, n_tok) + _halo_specs(tm, ZW, z_blk, n_tok)
                  + [full((CONV_W, 3 * C)), full((CONV_W, ZW)), full((2, C)), full((2, LANES, C)),
                     full((2, C)), full((2, LANES, C)), full((2 * LANES, C)),
                     full((1, C)), full((1, C)), full((1, C))]),
        out_specs=[tok, tok, tok, tok2, tok2, tok2, tok, tok],
        out_shape=[one, one, one, two, two, two, one, one],
        compiler_params=_params("parallel"),
        name="rwkv_prep",
    )(*([proj] * 12), lp["cw_rkv"], lp["cw_z"], lp["rw_w0"], lp["w2_pad"], lp["rw_a0"], lp["a2_pad"],
      lp["g2_pad"], lp["rw_k_k"].reshape(1, C), lp["rw_k_a"].reshape(1, C), lp["rw_r_k"].reshape(1, C))


def _mmh(a, b, dims=(((1,), (0,)), ((), ()))):
    return lax.dot_general(a, b, dims, precision=HIGHEST, preferred_element_type=F32)


_NT = (((1,), (1,)), ((), ()))
_TN = (((0,), (0,)), ((), ()))


def _rw_scan_kernel(*refs, has_s0, nc):
    if has_s0:
        s0_ref, r_ref, kk_ref, v_ref, lw_ref, b_ref, kd_ref, y_ref, so_ref, st_ref = refs
    else:
        r_ref, kk_ref, v_ref, lw_ref, b_ref, kd_ref, y_ref, so_ref, st_ref = refs
    C, G, N = SCAN_CHUNK, SCAN_HEADS, RW_HEAD
    W = G * N
    bwd = pl.program_id(2) == 1
    c = pl.program_id(3)

    row = lax.broadcasted_iota(jnp.int32, (W, W), 0)
    col = lax.broadcasted_iota(jnp.int32, (W, W), 1)
    same_head = (row // N) == (col // N)
    sgn = jnp.where(bwd, -1, 1)
    order = (row % C - col % C) * sgn
    strict = order > 0
    incl = order >= 0

    def tile_lanes(x):
        sel = (lax.broadcasted_iota(jnp.int32, (N, W), 0) == lax.broadcasted_iota(jnp.int32, (N, W), 1) % N)
        return _mmh(x, sel.astype(F32))

    @pl.when(c == 0)
    def _():
        if has_s0:
            st_ref[...] = jnp.where(same_head, tile_lanes(s0_ref[...].reshape(W, N)), 0.0)
        else:
            st_ref[...] = jnp.zeros((W, W), F32)

    lw = lw_ref[...]
    tr = lax.broadcasted_iota(jnp.int32, (C, C), 0)
    tc = lax.broadcasted_iota(jnp.int32, (C, C), 1)
    cum = ((tr - tc) * sgn >= 0).astype(F32)
    g_in = _mmh(cum, lw)
    g_ex = g_in - lw
    g_tot = jnp.sum(lw, axis=0, keepdims=True)
    e_neg = jnp.exp(-g_in)

    def expand(x):
        return jnp.where(same_head, jnp.concatenate([x] * G, axis=0), 0.0)

    ak = expand(kk_ref[...] * jnp.exp(g_ex))
    ar = expand(r_ref[...] * jnp.exp(g_in))
    bb = expand(b_ref[...] * e_neg)
    kd = expand(kd_ref[...] * e_neg)
    vb = expand(v_ref[...])
    e_rem = jnp.exp(g_tot - g_in)
    bb_end = expand(b_ref[...] * e_rem)
    kd_end = expand(kd_ref[...] * e_rem)

    S = st_ref[...]
    lm = jnp.where(strict, _mmh(ak, bb, _NT), 0.0)
    mm = jnp.where(strict, _mmh(ak, kd, _NT), 0.0)
    nb = jnp.where(incl, _mmh(ar, bb, _NT), 0.0)
    nk = jnp.where(incl, _mmh(ar, kd, _NT), 0.0)
    u = _mmh(ak, S, _NT) + _mmh(mm, vb)
    q = -lm
    u = u + _mmh(q, u)
    for _ in range(int(math.log2(C)) - 1):
        q = _mmh(q, q)
        u = u + _mmh(q, u)
    y = _mmh(ar, S, _NT) - _mmh(nb, u) + _mmh(nk, vb)
    y_ref[...] = functools.reduce(jnp.add, [y[h * C:(h + 1) * C] for h in range(G)])
    s_new = S * jnp.exp(g_tot) - _mmh(u, bb_end, _TN) + _mmh(vb, kd_end, _TN)
    st_ref[...] = s_new

    @pl.when(c == nc - 1)
    def _():
        fold = (lax.broadcasted_iota(jnp.int32, (W, N), 0) % N == lax.broadcasted_iota(jnp.int32, (W, N), 1))
        so_ref[...] = _mmh(s_new, fold.astype(F32)).reshape(G, N, N)


def _rw_scan(r, kk, v, lw, b, kd, *, row0, B, T, s0=None, layer=0):
    C = r.shape[1]
    H = C // RW_HEAD
    G, CH = SCAN_HEADS, SCAN_CHUNK
    W = G * RW_HEAD
    nc = T // CH
    blk0 = row0 // CH

    def tok_idx(bi, hg, d, c):
        return blk0 + bi * nc + jnp.where(d == 1, nc - 1 - c, c)

    one = pl.BlockSpec((CH, W), lambda bi, hg, d, c: (tok_idx(bi, hg, d, c), hg))
    two = pl.BlockSpec((None, CH, W), lambda bi, hg, d, c: (d, tok_idx(bi, hg, d, c), hg))
    in_specs = [one, one, one, two, two, two]
    args = [r, kk, v, lw, b, kd]
    if s0 is not None:
        in_specs = [pl.BlockSpec((None, None, None, G, RW_HEAD, RW_HEAD),
                                 lambda bi, hg, d, c: (bi, layer, d, hg, 0, 0))] + in_specs
        args = [s0] + args
    return pl.pallas_call(
        functools.partial(_rw_scan_kernel, has_s0=s0 is not None, nc=nc),
        grid=(B, H // G, 2, nc),
        in_specs=in_specs,
        out_specs=[pl.BlockSpec((None, CH, W),
                                lambda bi, hg, d, c: (d, bi * nc + jnp.where(d == 1, nc - 1 - c, c), hg)),
                   pl.BlockSpec((None, None, G, RW_HEAD, RW_HEAD), lambda bi, hg, d, c: (bi, d, hg, 0, 0))],
        out_shape=[jax.ShapeDtypeStruct((2, B * T, C), F32),
                   jax.ShapeDtypeStruct((B, 2, H, RW_HEAD, RW_HEAD), F32)],
        scratch_shapes=[pltpu.VMEM((W, W), F32)],
        compiler_params=_params("parallel", "parallel", "parallel", "arbitrary"),
        name="rwkv_scan_ctx" if s0 is not None else "rwkv_scan",
    )(*args)


def _rw_post_kernel(y_ref, bonus_ref, gate_ref, g_ref, b_ref, o_ref):
    y = y_ref[0] + y_ref[1]
    inv_n = 1.0 / RW_HEAD
    mu = _head_sum(y) * inv_n
    yc = y - mu
    var = _head_sum(yc * yc) * inv_n
    yn = yc * lax.rsqrt(var + RW_GN_EPS) * g_ref[...] + b_ref[...]
    o_ref[...] = ((yn + bonus_ref[...]) * gate_ref[...]).astype(o_ref.dtype)


def _rw_post(y, bonus, gate, gn_g, gn_b, tm=256):
    n_tok, C = bonus.shape
    tok = pl.BlockSpec((tm, C), lambda i: (i, 0))
    vec = pl.BlockSpec((1, C), lambda i: (0, 0))
    return pl.pallas_call(
        _rw_post_kernel,
        grid=(n_tok // tm,),
        in_specs=[pl.BlockSpec((2, tm, C), lambda i: (0, i, 0)), tok, tok, vec, vec],
        out_specs=tok,
        out_shape=jax.ShapeDtypeStruct((n_tok, C), BF16),
        compiler_params=_params("parallel"),
        name="rwkv_post",
    )(y, bonus, gate, gn_g.reshape(1, C), gn_b.reshape(1, C))


def _pad_rows(w, rows, at):
    return jnp.zeros((rows, w.shape[1]), w.dtype).at[at:at + w.shape[0]].set(w)


def _layer_weights(l, D, w_in, rw_conv_w, rw_w2, rw_a2, rw_g2, w_out, ffn_up, ffn_down):
    DA = D // 2
    C = D // 4
    o_rw = 3 * DA
    o_z = o_rw + 3 * C
    n_z = 4 * RW_LORA_R + RW_GATE_R
    o_uv = o_z + n_z
    zpad = 4 * LANES - n_z
    perm = lambda w: jnp.concatenate(
        [w[:, :o_z], w[:, o_uv:], w[:, o_z:o_uv], jnp.zeros((w.shape[0], zpad), w.dtype)], axis=1)
    cw = rw_conv_w[l]
    return dict(
        w_in=perm(w_in[l]).astype(BF16),
        cw_rkv=cw[:, :3 * C],
        cw_z=jnp.concatenate([cw[:, 3 * C:], jnp.zeros((CONV_W, zpad), F32)], axis=1),
        w2_pad=jnp.stack([_pad_rows(rw_w2[l, d], LANES, d * RW_LORA_R) for d in range(2)]),
        a2_pad=jnp.stack([_pad_rows(rw_a2[l, d], LANES, d * RW_LORA_R) for d in range(2)]),
        g2_pad=_pad_rows(rw_g2[l], 2 * LANES, 0),
        w_out=w_out[l].astype(BF16),
        ffn_up=ffn_up[l].astype(BF16),
        ffn_down=ffn_down[l].astype(BF16),
    )


def _rope_tables(T):
    n = DA_D // 4
    inv = ROPE_THETA ** (-jnp.arange(n, dtype=F32) / n)
    rows = T // GRID_W
    row = jnp.repeat(jnp.arange(rows), GRID_W).astype(F32)
    col = jnp.tile(jnp.arange(GRID_W), rows).astype(F32)
    sign = jnp.concatenate([-jnp.ones((n,), F32), jnp.ones((n,), F32)])
    cs, sn = [], []
    for pos in (row, col):
        ang = pos[:, None] * inv[None, :]
        cs.append(jnp.concatenate([jnp.cos(ang), jnp.cos(ang)], axis=1))
        sn.append(jnp.concatenate([jnp.sin(ang), jnp.sin(ang)], axis=1) * sign[None, :])
    cos = jnp.concatenate(cs, axis=1)
    sin = jnp.concatenate(sn, axis=1)
    return jnp.tile(cos, (1, 2)), jnp.tile(sin, (1, 2))


def kernel(x_prompt, x_sample, cache_da_k, cache_da_v, state_rwkv, c, c_ctx, mod_w, mod_b, norm1_g, norm2_g, w_in, da_lambda, da_subln_g, rw_conv_w, rw_w0, rw_w2, rw_a0, rw_a2, rw_g2, rw_k_k, rw_k_a, rw_r_k, rw_gn_g, rw_gn_b, cm_norm_g, cm_ws, cm_bs, w_out, ffn_up, ffn_conv_w, ffn_conv_b, ffn_down, final_norm_g):
    Bp, Tp, D = x_prompt.shape
    Bs, Ts, _ = x_sample.shape
    L = mod_w.shape[0]
    past = cache_da_k.shape[2]
    DA = D // 2
    H = DA // (2 * DA_D)
    C = D // 4
    F = ffn_down.shape[1]
    p_tok, s_tok = Bp * Tp, Bs * Ts
    assert Bs + 1 <= SUBLANES and p_tok % Ts == 0

    x = jnp.concatenate([x_prompt.reshape(p_tok, D), x_sample.reshape(s_tok, D)], axis=0)
    cond8 = jnp.concatenate([c_ctx[None, :], c, jnp.zeros((SUBLANES - 1 - Bs, D), F32)], axis=0)
    mod = _modulation(cond8, mod_w, mod_b)
    ck4 = cache_da_k.reshape(Bs, L, past, DA)
    cv4 = cache_da_v.reshape(Bs, L, past, DA)
    cos, sin = _rope_tables(Ts)
    blk = dict(r=3 * DA // C, u=(3 * DA + 3 * C) // C, z=(3 * DA + 5 * C) // (4 * LANES))

    new_k, new_v, new_s = [], [], []
    for l in range(L):
        lw_ = _layer_weights(l, D, w_in, rw_conv_w, rw_w2, rw_a2, rw_g2, w_out, ffn_up, ffn_down)
        lp = dict(lw_, rw_w0=rw_w0[l], rw_a0=rw_a0[l], rw_k_k=rw_k_k[l], rw_k_a=rw_k_a[l], rw_r_k=rw_r_k[l])
        lam_init = 0.8 - 0.6 * math.exp(-0.3 * l)

        h = _norm_mod(x, norm1_g[l], mod, l, 0, 1, p_tok, Ts)
        proj = _matmul(h, lp["w_in"], tm=1024, tn=512, name="proj_in")

        da_p = _attention(proj, da_lambda[l], da_subln_g[l], lam_init, row0=0, B=Bp, T=Tp, H=H)
        da_s = _attention(proj, da_lambda[l], da_subln_g[l], lam_init, row0=p_tok, B=Bs, T=Ts, H=H,
                          ctx=(ck4, cv4, l, cos, sin))
        r_, kk_, v_, lg_, b_, kd_, gate_, bonus_ = _rw_prep(proj, lp, r_blk=blk["r"], z_blk=blk["z"],
                                                            p_tok=p_tok, tp=Tp, ts=Ts)
        y_p, s_p = _rw_scan(r_, kk_, v_, lg_, b_, kd_, row0=0, B=Bp, T=Tp)
        y_s, _ = _rw_scan(r_, kk_, v_, lg_, b_, kd_, row0=p_tok, B=Bs, T=Ts, s0=state_rwkv, layer=l)
        y_rw = _rw_post(jnp.concatenate([y_p, y_s], axis=1), bonus_, gate_, rw_gn_g[l], rw_gn_b[l])
        y_cm = _chunk_mlp(proj, cm_norm_g[l], cm_ws[l], cm_bs[l], u_blk=blk["u"])

        mix = jnp.concatenate([jnp.concatenate([da_p, da_s], axis=0), y_rw, y_cm], axis=1)
        x = _matmul(mix, lp["w_out"], tm=1024, tn=512, resid=(x, mod, l, 2 * D // 512, p_tok, Ts),
                    name="proj_out")

        h = _norm_mod(x, norm2_g[l], mod, l, 3, 4, p_tok, Ts)
        up = _matmul(h, lp["ffn_up"], tm=1024, tn=512, name="ffn_up")
        act = _ffn_act(up, ffn_conv_w[l], ffn_conv_b[l], p_tok=p_tok, tp=Tp, ts=Ts)
        x = _matmul(act, lp["ffn_down"], tm=1024, tn=512, tk=F // 2,
                    resid=(x, mod, l, 5 * D // 512, p_tok, Ts), name="ffn_down")

        new_k.append(proj[:p_tok, DA:2 * DA].reshape(Bp, Tp, H, 2, DA_D))
        new_v.append(proj[:p_tok, 2 * DA:3 * DA].reshape(Bp, Tp, H, 2 * DA_D))
        new_s.append(s_p)

    y = _final_norm(x, final_norm_g)
    return (y[:p_tok].reshape(Bp, Tp, D), y[p_tok:].reshape(Bs, Ts, D),
            jnp.stack(new_k, axis=1), jnp.stack(new_v, axis=1), jnp.stack(new_s, axis=1))
```

```python
import functools
import math

import jax
import jax.numpy as jnp
from jax import lax
from jax.experimental import pallas as pl
from jax.experimental.pallas import tpu as pltpu

F32 = jnp.float32
BF16 = jnp.bfloat16
HIGHEST = lax.Precision.HIGHEST

LANES = 128
SUBLANES = 8
VMEM_BYTES_V7X = 64 * 1024 * 1024
VMEM_BUDGET = VMEM_BYTES_V7X * 3 // 4

GRID_W = 64
DA_D = 128
RW_HEAD = 64
RW_LORA_R = 64
RW_GATE_R = 160
RW_GN_EPS = 64e-5
CM_GROUPS = 4
CM_CHUNK = 128
CONV_W = 3
ROPE_THETA = 10000.0
NORM_EPS = 1e-6
SCAN_CHUNK = 64


def _params(*sem):
    return pltpu.CompilerParams(dimension_semantics=sem, vmem_limit_bytes=VMEM_BUDGET)


def _cond_row(tok0, p_tok, ts):
    return jnp.where(tok0 < p_tok, 0, 1 + jnp.maximum(tok0 - p_tok, 0) // ts)


def _mod_kernel(c_ref, w_ref, b_ref, o_ref):
    c = c_ref[...]
    s = (c * jax.nn.sigmoid(c)).astype(BF16)
    o_ref[...] = jnp.dot(s, w_ref[...].astype(BF16), preferred_element_type=F32) + b_ref[...]


def _modulation(cond8, mod_w, mod_b, tn=512):
    L, D, N = mod_w.shape
    return pl.pallas_call(
        _mod_kernel,
        grid=(L, N // tn),
        in_specs=[pl.BlockSpec((SUBLANES, D), lambda l, j: (0, 0)),
                  pl.BlockSpec((None, D, tn), lambda l, j: (l, 0, j)),
                  pl.BlockSpec((None, 1, tn), lambda l, j: (l, 0, j))],
        out_specs=pl.BlockSpec((None, SUBLANES, tn), lambda l, j: (l, 0, j)),
        out_shape=jax.ShapeDtypeStruct((L, SUBLANES, N), F32),
        compiler_params=_params("parallel", "parallel"),
        name="modulation",
    )(cond8, mod_w, mod_b.reshape(L, 1, N))


def _norm_mod_kernel(x_ref, g_ref, sh_ref, sc_ref, o_ref, *, tm, p_tok, ts):
    row = _cond_row(pl.program_id(0) * tm, p_tok, ts)
    x = x_ref[...]
    y = x * lax.rsqrt(jnp.mean(x * x, axis=-1, keepdims=True) + NORM_EPS)
    sc = sc_ref[pl.ds(row, 1), :]
    sh = sh_ref[pl.ds(row, 1), :]
    o_ref[...] = ((y * g_ref[...]) * (1.0 + sc) + sh).astype(o_ref.dtype)


def _norm_mod(x, g, mod, layer, k_sh, k_sc, p_tok, ts, tm=256):
    n, D = x.shape
    return pl.pallas_call(
        functools.partial(_norm_mod_kernel, tm=tm, p_tok=p_tok, ts=ts),
        grid=(n // tm,),
        in_specs=[pl.BlockSpec((tm, D), lambda i: (i, 0)),
                  pl.BlockSpec((1, D), lambda i: (0, 0)),
                  pl.BlockSpec((None, SUBLANES, D), lambda i: (layer, 0, k_sh)),
                  pl.BlockSpec((None, SUBLANES, D), lambda i: (layer, 0, k_sc))],
        out_specs=pl.BlockSpec((tm, D), lambda i: (i, 0)),
        out_shape=jax.ShapeDtypeStruct((n, D), BF16),
        compiler_params=_params("parallel"),
        name="norm_mod",
    )(x, g.reshape(1, D), mod, mod)


def _final_norm_kernel(x_ref, g_ref, o_ref):
    x = x_ref[...]
    o_ref[...] = x * lax.rsqrt(jnp.mean(x * x, axis=-1, keepdims=True) + NORM_EPS) * g_ref[...]


def _final_norm(x, g, tm=256):
    n, D = x.shape
    return pl.pallas_call(
        _final_norm_kernel,
        grid=(n // tm,),
        in_specs=[pl.BlockSpec((tm, D), lambda i: (i, 0)), pl.BlockSpec((1, D), lambda i: (0, 0))],
        out_specs=pl.BlockSpec((tm, D), lambda i: (i, 0)),
        out_shape=jax.ShapeDtypeStruct((n, D), F32),
        compiler_params=_params("parallel"),
        name="final_norm",
    )(x, g.reshape(1, D))


def _mm_kernel(*refs, nk, resid, tm, p_tok, ts):
    if resid:
        a_ref, b_ref, x_ref, g_ref, o_ref = refs[:5]
    else:
        a_ref, b_ref, o_ref = refs[:3]
    acc_ref = refs[-1] if nk > 1 else None

    def finish(acc):
        if resid:
            row = _cond_row(pl.program_id(0) * tm, p_tok, ts)
            o_ref[...] = x_ref[...] + g_ref[pl.ds(row, 1), :] * acc
        else:
            o_ref[...] = acc

    part = jnp.dot(a_ref[...], b_ref[...], preferred_element_type=F32)
    if nk == 1:
        finish(part)
    else:
        k = pl.program_id(2)

        @pl.when(k == 0)
        def _():
            acc_ref[...] = part

        @pl.when((k > 0) & (k < nk - 1))
        def _():
            acc_ref[...] += part

        @pl.when(k == nk - 1)
        def _():
            finish(acc_ref[...] + part)


def _matmul(a, b, *, tm, tn, tk=None, resid=None, name):
    M, K = a.shape
    N = b.shape[1]
    tk = K if tk is None else tk
    nk = K // tk
    in_specs = [pl.BlockSpec((tm, tk), lambda i, j, k: (i, k)),
                pl.BlockSpec((tk, tn), lambda i, j, k: (k, j))]
    args = [a, b]
    p_tok = ts = 0
    if resid is not None:
        x, mod, layer, gate_blk, p_tok, ts = resid
        in_specs += [pl.BlockSpec((tm, tn), lambda i, j, k: (i, j)),
                     pl.BlockSpec((None, SUBLANES, tn), lambda i, j, k: (layer, 0, gate_blk + j))]
        args += [x, mod]
    return pl.pallas_call(
        functools.partial(_mm_kernel, nk=nk, resid=resid is not None, tm=tm, p_tok=p_tok, ts=ts),
        grid=(M // tm, N // tn, nk),
        in_specs=in_specs,
        out_specs=pl.BlockSpec((tm, tn), lambda i, j, k: (i, j)),
        out_shape=jax.ShapeDtypeStruct((M, N), F32),
        scratch_shapes=[pltpu.VMEM((tm, tn), F32)] if nk > 1 else [],
        compiler_params=_params("parallel", "parallel", "arbitrary"),
        name=name,
    )(*args)


def _rope(x, cos, sin_signed):
    lane = lax.broadcasted_iota(jnp.int32, x.shape, 1)
    width = x.shape[1]
    rot = jnp.where((lane % 64) < 32, pltpu.roll(x, width - 32, 1), pltpu.roll(x, 32, 1))
    return x * cos + rot * sin_signed


def _attn_kernel(*refs, rope, lam_init):
    if rope:
        (lam_ref, q_ref, k_ref, v_ref, kc_ref, vc_ref, cq_ref, sq_ref, ck_ref, sk_ref, g_ref,
         o_ref, kr_ref) = refs
    else:
        lam_ref, q_ref, k_ref, v_ref, g_ref, o_ref = refs
    lm = lam_ref[...]
    s1 = jnp.sum(lm[0:1] * lm[1:2], axis=-1, keepdims=True)
    s2 = jnp.sum(lm[2:3] * lm[3:4], axis=-1, keepdims=True)
    lam = jnp.exp(s1) - jnp.exp(s2) + lam_init
    scale = DA_D ** -0.5
    nt = (((1,), (1,)), ((), ()))

    if rope:
        @pl.when(pl.program_id(2) == 0)
        def _():
            kr_ref[...] = _rope(k_ref[...], ck_ref[...], sk_ref[...]).astype(BF16)

        q = _rope(q_ref[...], cq_ref[...], sq_ref[...]).astype(BF16)
        keys = [kr_ref[...], kc_ref[...].astype(BF16)]
        vals = [v_ref[...].astype(BF16), vc_ref[...].astype(BF16)]
    else:
        q = q_ref[...].astype(BF16)
        keys = [k_ref[...].astype(BF16)]
        vals = [v_ref[...].astype(BF16)]

    es, rs = [], []
    for c in range(2):
        qc = q[:, c * DA_D:(c + 1) * DA_D]
        ss = [lax.dot_general(qc, kk[:, c * DA_D:(c + 1) * DA_D], nt, preferred_element_type=F32) * scale
              for kk in keys]
        m = functools.reduce(jnp.maximum, [jnp.max(s, axis=-1, keepdims=True) for s in ss])
        e = [jnp.exp(s - m) for s in ss]
        d = functools.reduce(jnp.add, [jnp.sum(x, axis=-1, keepdims=True) for x in e])
        es.append(e)
        rs.append(1.0 / d)
    r0 = rs[0]
    r1 = rs[1] * lam
    o = None
    for j in range(len(keys)):
        a = (es[0][j] * r0 - es[1][j] * r1).astype(BF16)
        t = jnp.dot(a, vals[j], preferred_element_type=F32)
        o = t if o is None else o + t
    y = o * lax.rsqrt(jnp.mean(o * o, axis=-1, keepdims=True) + NORM_EPS)
    o_ref[...] = (y * g_ref[...] * (1.0 - lam_init)).astype(o_ref.dtype)


def _attention(proj, lam_p, g, lam_init, *, row0, B, T, H, ctx=None, tq=256):
    W = 2 * DA_D
    nq = T // tq
    qb0 = row0 // tq
    kb0 = row0 // T
    rope = ctx is not None
    in_specs = [pl.BlockSpec((4, DA_D), lambda b, h, i: (0, 0)),
                pl.BlockSpec((tq, W), lambda b, h, i: (qb0 + b * nq + i, h)),
                pl.BlockSpec((T, W), lambda b, h, i: (kb0 + b, H + h)),
                pl.BlockSpec((T, W), lambda b, h, i: (kb0 + b, 2 * H + h))]
    args = [lam_p, proj, proj, proj]
    scratch = []
    if rope:
        ck, cv, layer, cos, sin = ctx
        past = ck.shape[2]
        in_specs += [pl.BlockSpec((None, None, past, W), lambda b, h, i: (b, layer, 0, h)),
                     pl.BlockSpec((None, None, past, W), lambda b, h, i: (b, layer, 0, h)),
                     pl.BlockSpec((tq, W), lambda b, h, i: (i, 0)),
                     pl.BlockSpec((tq, W), lambda b, h, i: (i, 0)),
                     pl.BlockSpec((T, W), lambda b, h, i: (0, 0)),
                     pl.BlockSpec((T, W), lambda b, h, i: (0, 0))]
        args += [ck, cv, cos, sin, cos, sin]
        scratch = [pltpu.VMEM((T, W), BF16)]
    in_specs.append(pl.BlockSpec((1, W), lambda b, h, i: (0, 0)))
    args.append(g.reshape(1, W))
    return pl.pallas_call(
        functools.partial(_attn_kernel, rope=rope, lam_init=lam_init),
        grid=(B, H, nq),
        in_specs=in_specs,
        out_specs=pl.BlockSpec((tq, W), lambda b, h, i: (b * nq + i, h)),
        out_shape=jax.ShapeDtypeStruct((B * T, H * W), BF16),
        scratch_shapes=scratch,
        compiler_params=_params("parallel", "parallel", "arbitrary"),
        name="diff_attn_ctx" if rope else "diff_attn",
    )(*args)


def _seq_edges(tok0, tm, p_tok, tp, ts):
    t = tok0 + lax.broadcasted_iota(jnp.int32, (tm, 1), 0)
    in_p = t < p_tok
    pos = jnp.where(in_p, t % tp, jnp.maximum(t - p_tok, 0) % ts)
    length = jnp.where(in_p, tp, ts)
    return pos == 0, pos == length - 1


def _conv3(x, prev8, next8, w, first, last):
    tm = x.shape[0]
    ridx = lax.broadcasted_iota(jnp.int32, (tm, 1), 0)
    xp = jnp.where(ridx == 0, prev8[SUBLANES - 1:SUBLANES], pltpu.roll(x, 1, 0))
    xp = jnp.where(first, 0.0, xp)
    xn = jnp.where(ridx == tm - 1, next8[0:1], pltpu.roll(x, tm - 1, 0))
    xn = jnp.where(last, 0.0, xn)
    return xp * w[0:1] + x * w[1:2] + xn * w[2:3]


def _halo_specs(tm, width, col_blk, n_tok, nidx=1):
    r = tm // SUBLANES
    last = n_tok // SUBLANES - 1
    if nidx == 1:
        return [pl.BlockSpec((tm, width), lambda i: (i, col_blk)),
                pl.BlockSpec((SUBLANES, width), lambda i: (jnp.maximum(i * r - 1, 0), col_blk)),
                pl.BlockSpec((SUBLANES, width), lambda i: (jnp.minimum((i + 1) * r, last), col_blk))]
    return [pl.BlockSpec((tm, width), lambda i, j: (i, col_blk + j)),
            pl.BlockSpec((SUBLANES, width), lambda i, j: (jnp.maximum(i * r - 1, 0), col_blk + j)),
            pl.BlockSpec((SUBLANES, width), lambda i, j: (jnp.minimum((i + 1) * r, last), col_blk + j))]


def _cmlp_kernel(u_ref, v_ref, gain_ref, ws_ref, bs_ref, o_ref, *, tm):
    v = v_ref[...]
    z = (v * lax.rsqrt(jnp.mean(v * v, axis=-1, keepdims=True) + NORM_EPS) * gain_ref[...]).astype(BF16)
    gw = z.shape[1] // CM_GROUPS
    for n in range(tm // CM_CHUNK):
        rows = slice(n * CM_CHUNK, (n + 1) * CM_CHUNK)
        for g in range(CM_GROUPS):
            cols = slice(g * gw, (g + 1) * gw)
            t = jnp.dot(ws_ref[g].astype(BF16), z[rows, cols], preferred_element_type=F32) + bs_ref[g]
            o_ref[rows, cols] = (u_ref[rows, cols] * t).astype(o_ref.dtype)


def _chunk_mlp(proj, gain, ws, bs, *, u_blk, tm=256):
    n_tok = proj.shape[0]
    W = gain.shape[0]
    gw = W // CM_GROUPS
    bs_b = jnp.broadcast_to(bs[:, :, None], (CM_GROUPS, CM_CHUNK, gw))
    return pl.pallas_call(
        functools.partial(_cmlp_kernel, tm=tm),
        grid=(n_tok // tm,),
        in_specs=[pl.BlockSpec((tm, W), lambda i: (i, u_blk)),
                  pl.BlockSpec((tm, W), lambda i: (i, u_blk + 1)),
                  pl.BlockSpec((1, W), lambda i: (0, 0)),
                  pl.BlockSpec((CM_GROUPS, CM_CHUNK, CM_CHUNK), lambda i: (0, 0, 0)),
                  pl.BlockSpec((CM_GROUPS, CM_CHUNK, gw), lambda i: (0, 0, 0))],
        out_specs=pl.BlockSpec((tm, W), lambda i: (i, 0)),
        out_shape=jax.ShapeDtypeStruct((n_tok, W), BF16),
        compiler_params=_params("parallel"),
        name="chunk_mlp",
    )(proj, proj, gain.reshape(1, W), ws, bs_b)


def _ffn_act_kernel(a_ref, ap_ref, an_ref, b_ref, bp_ref, bn_ref, wa_ref, wb_ref, ba_ref, bb_ref, o_ref,
                    sa_ref, sb_ref, *, tm, tc, p_tok, tp, ts):
    tok0 = pl.program_id(0) * tm
    in_p = tok0 < p_tok
    pos = jnp.where(in_p, tok0 % tp, jnp.maximum(tok0 - p_tok, 0) % ts)
    length = jnp.where(in_p, tp, ts)
    keep_prev = (pos != 0).astype(F32)
    keep_next = (pos + tm != length).astype(F32)

    def conv(x_ref, p_ref, n_ref, s_ref, w_ref, bias_ref, cols):
        s_ref[SUBLANES:SUBLANES + tm, :] = x_ref[:, cols]
        s_ref[SUBLANES - 1:SUBLANES, :] = p_ref[SUBLANES - 1:SUBLANES, cols] * keep_prev
        s_ref[SUBLANES + tm:SUBLANES + tm + 1, :] = n_ref[0:1, cols] * keep_next
        w = w_ref[:, cols]
        return (s_ref[SUBLANES - 1:SUBLANES - 1 + tm, :] * w[0:1] + x_ref[:, cols] * w[1:2]
                + s_ref[SUBLANES + 1:SUBLANES + 1 + tm, :] * w[2:3] + bias_ref[:, cols])

    def body(j, carry):
        cols = pl.ds(pl.multiple_of(j * LANES, LANES), LANES)
        ga = conv(a_ref, ap_ref, an_ref, sa_ref, wa_ref, ba_ref, cols)
        gb = conv(b_ref, bp_ref, bn_ref, sb_ref, wb_ref, bb_ref, cols)
        o_ref[:, cols] = (ga * jax.nn.sigmoid(ga) * gb).astype(o_ref.dtype)
        return carry

    lax.fori_loop(0, tc // LANES, body, 0)


def _ffn_act(up, conv_w, conv_b, *, p_tok, tp, ts, tm=256, col_blocks=2):
    n_tok, two_f = up.shape
    F = two_f // 2
    tc = F // col_blocks
    assert tp % tm == 0 and ts % tm == 0 and tc % LANES == 0
    cb = conv_b.reshape(1, two_f)
    wspec = lambda off: pl.BlockSpec((CONV_W, tc), lambda i, j: (0, off + j))
    bspec = lambda off: pl.BlockSpec((1, tc), lambda i, j: (0, off + j))
    halo = pltpu.VMEM((tm + 2 * SUBLANES, LANES), F32)
    return pl.pallas_call(
        functools.partial(_ffn_act_kernel, tm=tm, tc=tc, p_tok=p_tok, tp=tp, ts=ts),
        grid=(n_tok // tm, col_blocks),
        in_specs=(_halo_specs(tm, tc, 0, n_tok, 2) + _halo_specs(tm, tc, col_blocks, n_tok, 2)
                  + [wspec(0), wspec(col_blocks), bspec(0), bspec(col_blocks)]),
        out_specs=pl.BlockSpec((tm, tc), lambda i, j: (i, j)),
        out_shape=jax.ShapeDtypeStruct((n_tok, F), BF16),
        scratch_shapes=[halo, halo],
        compiler_params=_params("parallel", "parallel"),
        name="ffn_act",
    )(up, up, up, up, up, up, conv_w, conv_w, cb, cb)


def _head_sum(x):
    r = lax.broadcasted_iota(jnp.int32, (LANES, LANES), 0) // RW_HEAD
    c = lax.broadcasted_iota(jnp.int32, (LANES, LANES), 1) // RW_HEAD
    e = (r == c).astype(F32)
    cols = [jnp.dot(x[:, j * LANES:(j + 1) * LANES], e, precision=HIGHEST, preferred_element_type=F32)
            for j in range(x.shape[1] // LANES)]
    return jnp.concatenate(cols, axis=1)


def _rw_prep_kernel(r_ref, rp_ref, rn_ref, k_ref, kp_ref, kn_ref, v_ref, vp_ref, vn_ref,
                    z_ref, zp_ref, zn_ref, cw_ref, cz_ref, w0_ref, w2_ref, a0_ref, a2_ref, g2_ref,
                    kk_ref, ka_ref, rk_ref,
                    ro_ref, kko_ref, vo_ref, lw_ref, b_ref, kd_ref, gate_ref, bonus_ref,
                    *, tm, p_tok, tp, ts, C):
    first, last = _seq_edges(pl.program_id(0) * tm, tm, p_tok, tp, ts)
    cw = cw_ref[...]
    r = _conv3(r_ref[...], rp_ref[...], rn_ref[...], cw[:, 0:C], first, last)
    k = _conv3(k_ref[...], kp_ref[...], kn_ref[...], cw[:, C:2 * C], first, last)
    v = _conv3(v_ref[...], vp_ref[...], vn_ref[...], cw[:, 2 * C:3 * C], first, last)
    z = _conv3(z_ref[...], zp_ref[...], zn_ref[...], cz_ref[...], first, last)
    dec = jnp.tanh(z[:, 0:LANES]).astype(BF16)
    aa = z[:, LANES:2 * LANES].astype(BF16)
    gl = jax.nn.sigmoid(z[:, 2 * LANES:4 * LANES]).astype(BF16)
    gate_ref[...] = jnp.dot(gl, g2_ref[...].astype(BF16), preferred_element_type=F32)
    kk = k * kk_ref[...]
    kk = kk * lax.rsqrt(_head_sum(kk * kk) + 1e-12)
    ro_ref[...] = r
    kko_ref[...] = kk
    vo_ref[...] = v
    kd_sum = None
    for d in range(2):
        wl = w0_ref[d:d + 1, :] + jnp.dot(dec, w2_ref[d].astype(BF16), preferred_element_type=F32)
        lw_ref[d] = -jnp.exp(-jax.nn.softplus(-wl) - 0.5)
        a = jax.nn.sigmoid(a0_ref[d:d + 1, :] + jnp.dot(aa, a2_ref[d].astype(BF16), preferred_element_type=F32))
        b_ref[d] = kk * a
        kd = k * (1.0 + (a - 1.0) * ka_ref[...])
        kd_ref[d] = kd
        kd_sum = kd if kd_sum is None else kd_sum + kd
    bonus_ref[...] = _head_sum(r * kd_sum * rk_ref[...]) * v


def _rw_prep(proj, lp, *, r_blk, z_blk, p_tok, tp, ts, tm=256):
    n_tok = proj.shape[0]
    C = lp["rw_k_k"].shape[0]
    ZW = 4 * LANES
    full = lambda shape: pl.BlockSpec(shape, lambda i: (0,) * len(shape))
    tok = pl.BlockSpec((tm, C), lambda i: (i, 0))
    tok2 = pl.BlockSpec((2, tm, C), lambda i: (0, i, 0))
    one = jax.ShapeDtypeStruct((n_tok, C), F32)
    two = jax.ShapeDtypeStruct((2, n_tok, C), F32)
    return pl.pallas_call(
        functools.partial(_rw_prep_kernel, tm=tm, p_tok=p_tok, tp=tp, ts=ts, C=C),
        grid=(n_tok // tm,),
        in_specs=(_halo_specs(tm, C, r_blk, n_tok) + _halo_specs(tm, C, r_blk + 1, n_tok)
                  + _halo_specs(tm, C, r_blk + 2, n_tok) + _halo_specs(tm, ZW, z_blk, n_tok)
                  + [full((CONV_W, 3 * C)), full((CONV_W, ZW)), full((2, C)), full((2, LANES, C)),
                     full((2, C)), full((2, LANES, C)), full((2 * LANES, C)),
                     full((1, C)), full((1, C)), full((1, C))]),
        out_specs=[tok, tok, tok, tok2, tok2, tok2, tok, tok],
        out_shape=[one, one, one, two, two, two, one, one],
        compiler_params=_params("parallel"),
        name="rwkv_prep",
    )(*([proj] * 12), lp["cw_rkv"], lp["cw_z"], lp["rw_w0"], lp["w2_pad"], lp["rw_a0"], lp["a2_pad"],
      lp["g2_pad"], lp["rw_k_k"].reshape(1, C), lp["rw_k_a"].reshape(1, C), lp["rw_r_k"].reshape(1, C))


def _mmh(a, b, dims=(((1,), (0,)), ((), ()))):
    return lax.dot_general(a, b, dims, precision=HIGHEST, preferred_element_type=F32)


_DIMS = {"nn": ((1,), (0,)), "nt": ((1,), (1,)), "tn": ((0,), (0,))}


def _mm1(a, b, kind="nn"):
    return lax.dot_general(a, b, (_DIMS[kind], ((), ())), preferred_element_type=F32)


def _cumsum_rows(tri, x):
    hi = x.astype(BF16)
    r1 = x - hi.astype(F32)
    mid = r1.astype(BF16)
    lo = (r1 - mid.astype(F32)).astype(BF16)
    t = tri.astype(BF16)
    return _mm1(jnp.concatenate([t, t, t], axis=1), jnp.concatenate([hi, mid, lo], axis=0))


def _rw_scan_kernel(*refs, has_s0, nc):
    if has_s0:
        s0_ref, r_ref, kk_ref, v_ref, lw_ref, b_ref, kd_ref, y_ref, so_ref, st_ref = refs
    else:
        r_ref, kk_ref, v_ref, lw_ref, b_ref, kd_ref, y_ref, so_ref, st_ref = refs
    C, N = SCAN_CHUNK, RW_HEAD
    W = 2 * N
    npair = r_ref.shape[1] // W
    bwd = (pl.program_id(1) == 1).astype(jnp.int32)
    c = pl.program_id(2)

    def order(shape, dim):
        t = lax.broadcasted_iota(jnp.int32, shape, dim) % C
        return t + bwd * (C - 1 - 2 * t)

    row = lax.broadcasted_iota(jnp.int32, (W, W), 0)
    col = lax.broadcasted_iota(jnp.int32, (W, W), 1)
    same_head = (row // N) == (col // N)
    eye = (row == col).astype(F32)
    rt, ct = order((W, W), 0), order((W, W), 1)
    strict = rt > ct
    incl = rt >= ct
    level = {s: ((rt // (2 * s)) == (ct // (2 * s))) & ((rt // s) % 2 == 1) & ((ct // s) % 2 == 0)
             for s in (1, 2, 4, 8, 16, 32)}

    @pl.when(c == 0)
    def _():
        if has_s0:
            sel = (lax.broadcasted_iota(jnp.int32, (N, W), 0) == lax.broadcasted_iota(jnp.int32, (N, W), 1) % N)
            for p in range(npair):
                tiled = _mmh(s0_ref[2 * p:2 * p + 2].reshape(W, N), sel.astype(F32))
                st_ref[p] = jnp.where(same_head, tiled, 0.0)
        else:
            st_ref[...] = jnp.zeros(st_ref.shape, F32)

    lw = lw_ref[...]
    g_in = _cumsum_rows(order((C, C), 0) >= order((C, C), 1), lw)
    g_tot = jnp.sum(lw, axis=0, keepdims=True)
    e_neg = jnp.exp(-g_in)
    e_rem = jnp.exp(g_tot - g_in)
    e_tot = jnp.exp(g_tot)
    kk_t = (kk_ref[...] * jnp.exp(g_in - lw)).astype(BF16)
    r_t = (r_ref[...] * jnp.exp(g_in)).astype(BF16)
    b_t = (b_ref[...] * e_neg).astype(BF16)
    kd_t = (kd_ref[...] * e_neg).astype(BF16)
    b_end = (b_ref[...] * e_rem).astype(BF16)
    kd_end = (kd_ref[...] * e_rem).astype(BF16)
    v_t = v_ref[...].astype(BF16)
    zero = jnp.zeros((), BF16)

    def expand(x, p):
        return jnp.where(same_head, jnp.concatenate([x[:, p * W:(p + 1) * W]] * 2, axis=0), zero)

    P = range(npair)
    ak = [expand(kk_t, p) for p in P]
    bk = [jnp.concatenate([expand(b_t, p), expand(kd_t, p)], axis=0) for p in P]
    vb = [expand(v_t, p) for p in P]
    S = [st_ref[p] for p in P]
    Sb = [S[p].astype(BF16) for p in P]
    lm = [_mm1(ak[p], bk[p], "nt") for p in P]
    L = [jnp.where(strict, lm[p][:, :W], 0.0) for p in P]
    Lb = [L[p].astype(BF16) for p in P]
    M = [jnp.where(strict, lm[p][:, W:], 0.0).astype(BF16) for p in P]
    rhs = [(_mm1(ak[p], Sb[p], "nt") + _mm1(M[p], vb[p])).astype(BF16) for p in P]
    X = [eye - jnp.where(level[1], L[p], 0.0) for p in P]
    for s in (2, 4, 8, 16, 32):
        Xb = [X[p].astype(BF16) for p in P]
        t = [_mm1(jnp.where(level[s], Lb[p], zero), Xb[p]).astype(BF16) for p in P]
        X = [X[p] - _mm1(Xb[p], t[p]) for p in P]
    Ub = [_mm1(X[p].astype(BF16), rhs[p]).astype(BF16) for p in P]
    for p in P:
        st_ref[p] = (S[p] * e_tot[:, p * W:(p + 1) * W] - _mm1(Ub[p], expand(b_end, p), "tn")
                     + _mm1(vb[p], expand(kd_end, p), "tn"))
    ar = [expand(r_t, p) for p in P]
    nn = [_mm1(ar[p], bk[p], "nt") for p in P]
    nb = [jnp.where(incl, nn[p][:, :W], 0.0).astype(BF16) for p in P]
    nk = [jnp.where(incl, nn[p][:, W:], 0.0).astype(BF16) for p in P]
    for p in P:
        y = _mm1(ar[p], Sb[p], "nt") - _mm1(nb[p], Ub[p]) + _mm1(nk[p], vb[p])
        y_ref[:, p * W:(p + 1) * W] = y[:C] + y[C:]

    @pl.when(c == nc - 1)
    def _():
        fold = (lax.broadcasted_iota(jnp.int32, (W, N), 0) % N == lax.broadcasted_iota(jnp.int32, (W, N), 1))
        for p in range(npair):
            so_ref[2 * p:2 * p + 2] = _mmh(st_ref[p], fold.astype(F32)).reshape(2, N, N)


def _rw_scan(r, kk, v, lw, b, kd, *, row0, B, T, s0=None, layer=0):
    C = r.shape[1]
    H = C // RW_HEAD
    CH = SCAN_CHUNK
    nc = T // CH
    blk0 = row0 // CH

    def chunk(d, c):
        return jnp.where(d == 1, nc - 1 - c, c)

    one = pl.BlockSpec((CH, C), lambda bi, d, c: (blk0 + bi * nc + chunk(d, c), 0))
    two = pl.BlockSpec((None, CH, C), lambda bi, d, c: (d, blk0 + bi * nc + chunk(d, c), 0))
    in_specs = [one, one, one, two, two, two]
    args = [r, kk, v, lw, b, kd]
    if s0 is not None:
        in_specs = [pl.BlockSpec((None, None, None, H, RW_HEAD, RW_HEAD),
                                 lambda bi, d, c: (bi, layer, d, 0, 0, 0))] + in_specs
        args = [s0] + args
    return pl.pallas_call(
        functools.partial(_rw_scan_kernel, has_s0=s0 is not None, nc=nc),
        grid=(B, 2, nc),
        in_specs=in_specs,
        out_specs=[pl.BlockSpec((None, CH, C), lambda bi, d, c: (d, bi * nc + chunk(d, c), 0)),
                   pl.BlockSpec((None, None, H, RW_HEAD, RW_HEAD), lambda bi, d, c: (bi, d, 0, 0, 0))],
        out_shape=[jax.ShapeDtypeStruct((2, B * T, C), F32),
                   jax.ShapeDtypeStruct((B, 2, H, RW_HEAD, RW_HEAD), F32)],
        scratch_shapes=[pltpu.VMEM((H // 2, 2 * RW_HEAD, 2 * RW_HEAD), F32)],
        compiler_params=_params("parallel", "parallel", "arbitrary"),
        name="rwkv_scan_ctx" if s0 is not None else "rwkv_scan",
    )(*args)


def _rw_post_kernel(y_ref, bonus_ref, gate_ref, g_ref, b_ref, o_ref):
    y = y_ref[0] + y_ref[1]
    inv_n = 1.0 / RW_HEAD
    mu = _head_sum(y) * inv_n
    yc = y - mu
    var = _head_sum(yc * yc) * inv_n
    yn = yc * lax.rsqrt(var + RW_GN_EPS) * g_ref[...] + b_ref[...]
    o_ref[...] = ((yn + bonus_ref[...]) * gate_ref[...]).astype(o_ref.dtype)


def _rw_post(y, bonus, gate, gn_g, gn_b, tm=256):
    n_tok, C = bonus.shape
    tok = pl.BlockSpec((tm, C), lambda i: (i, 0))
    vec = pl.BlockSpec((1, C), lambda i: (0, 0))
    return pl.pallas_call(
        _rw_post_kernel,
        grid=(n_tok // tm,),
        in_specs=[pl.BlockSpec((2, tm, C), lambda i: (0, i, 0)), tok, tok, vec, vec],
        out_specs=tok,
        out_shape=jax.ShapeDtypeStruct((n_tok, C), BF16),
        compiler_params=_params("parallel"),
        name="rwkv_post",
    )(y, bonus, gate, gn_g.reshape(1, C), gn_b.reshape(1, C))


def _pad_rows(w, rows, at):
    return jnp.zeros((rows, w.shape[1]), w.dtype).at[at:at + w.shape[0]].set(w)


def _layer_weights(l, D, w_in, rw_conv_w, rw_w2, rw_a2, rw_g2, w_out, ffn_up, ffn_down):
    DA = D // 2
    C = D // 4
    o_rw = 3 * DA
    o_z = o_rw + 3 * C
    n_z = 4 * RW_LORA_R + RW_GATE_R
    o_uv = o_z + n_z
    zpad = 4 * LANES - n_z
    perm = lambda w: jnp.concatenate(
        [w[:, :o_z], w[:, o_uv:], w[:, o_z:o_uv], jnp.zeros((w.shape[0], zpad), w.dtype)], axis=1)
    cw = rw_conv_w[l]
    return dict(
        w_in=perm(w_in[l]).astype(BF16),
        cw_rkv=cw[:, :3 * C],
        cw_z=jnp.concatenate([cw[:, 3 * C:], jnp.zeros((CONV_W, zpad), F32)], axis=1),
        w2_pad=jnp.stack([_pad_rows(rw_w2[l, d], LANES, d * RW_LORA_R) for d in range(2)]),
        a2_pad=jnp.stack([_pad_rows(rw_a2[l, d], LANES, d * RW_LORA_R) for d in range(2)]),
        g2_pad=_pad_rows(rw_g2[l], 2 * LANES, 0),
        w_out=w_out[l].astype(BF16),
        ffn_up=ffn_up[l].astype(BF16),
        ffn_down=ffn_down[l].astype(BF16),
    )


def _rope_tables(T):
    n = DA_D // 4
    inv = ROPE_THETA ** (-jnp.arange(n, dtype=F32) / n)
    rows = T // GRID_W
    row = jnp.repeat(jnp.arange(rows), GRID_W).astype(F32)
    col = jnp.tile(jnp.arange(GRID_W), rows).astype(F32)
    sign = jnp.concatenate([-jnp.ones((n,), F32), jnp.ones((n,), F32)])
    cs, sn = [], []
    for pos in (row, col):
        ang = pos[:, None] * inv[None, :]
        cs.append(jnp.concatenate([jnp.cos(ang), jnp.cos(ang)], axis=1))
        sn.append(jnp.concatenate([jnp.sin(ang), jnp.sin(ang)], axis=1) * sign[None, :])
    cos = jnp.concatenate(cs, axis=1)
    sin = jnp.concatenate(sn, axis=1)
    return jnp.tile(cos, (1, 2)), jnp.tile(sin, (1, 2))


def kernel(x_prompt, x_sample, cache_da_k, cache_da_v, state_rwkv, c, c_ctx, mod_w, mod_b, norm1_g, norm2_g, w_in, da_lambda, da_subln_g, rw_conv_w, rw_w0, rw_w2, rw_a0, rw_a2, rw_g2, rw_k_k, rw_k_a, rw_r_k, rw_gn_g, rw_gn_b, cm_norm_g, cm_ws, cm_bs, w_out, ffn_up, ffn_conv_w, ffn_conv_b, ffn_down, final_norm_g):
    Bp, Tp, D = x_prompt.shape
    Bs, Ts, _ = x_sample.shape
    L = mod_w.shape[0]
    past = cache_da_k.shape[2]
    DA = D // 2
    H = DA // (2 * DA_D)
    C = D // 4
    F = ffn_down.shape[1]
    p_tok, s_tok = Bp * Tp, Bs * Ts
    assert Bs + 1 <= SUBLANES and p_tok % Ts == 0

    x = jnp.concatenate([x_prompt.reshape(p_tok, D), x_sample.reshape(s_tok, D)], axis=0)
    cond8 = jnp.concatenate([c_ctx[None, :], c, jnp.zeros((SUBLANES - 1 - Bs, D), F32)], axis=0)
    mod = _modulation(cond8, mod_w, mod_b)
    ck4 = cache_da_k.reshape(Bs, L, past, DA)
    cv4 = cache_da_v.reshape(Bs, L, past, DA)
    cos, sin = _rope_tables(Ts)
    blk = dict(r=3 * DA // C, u=(3 * DA + 3 * C) // C, z=(3 * DA + 5 * C) // (4 * LANES))

    new_k, new_v, new_s = [], [], []
    for l in range(L):
        lw_ = _layer_weights(l, D, w_in, rw_conv_w, rw_w2, rw_a2, rw_g2, w_out, ffn_up, ffn_down)
        lp = dict(lw_, rw_w0=rw_w0[l], rw_a0=rw_a0[l], rw_k_k=rw_k_k[l], rw_k_a=rw_k_a[l], rw_r_k=rw_r_k[l])
        lam_init = 0.8 - 0.6 * math.exp(-0.3 * l)

        h = _norm_mod(x, norm1_g[l], mod, l, 0, 1, p_tok, Ts)
        proj = _matmul(h, lp["w_in"], tm=1024, tn=512, name="proj_in")

        da_p = _attention(proj, da_lambda[l], da_subln_g[l], lam_init, row0=0, B=Bp, T=Tp, H=H)
        da_s = _attention(proj, da_lambda[l], da_subln_g[l], lam_init, row0=p_tok, B=Bs, T=Ts, H=H,
                          ctx=(ck4, cv4, l, cos, sin))
        r_, kk_, v_, lg_, b_, kd_, gate_, bonus_ = _rw_prep(proj, lp, r_blk=blk["r"], z_blk=blk["z"],
                                                            p_tok=p_tok, tp=Tp, ts=Ts)
        y_p, s_p = _rw_scan(r_, kk_, v_, lg_, b_, kd_, row0=0, B=Bp, T=Tp)
        y_s, _ = _rw_scan(r_, kk_, v_, lg_, b_, kd_, row0=p_tok, B=Bs, T=Ts, s0=state_rwkv, layer=l)
        y_rw = _rw_post(jnp.concatenate([y_p, y_s], axis=1), bonus_, gate_, rw_gn_g[l], rw_gn_b[l])
        y_cm = _chunk_mlp(proj, cm_norm_g[l], cm_ws[l], cm_bs[l], u_blk=blk["u"])

        mix = jnp.concatenate([jnp.concatenate([da_p, da_s], axis=0), y_rw, y_cm], axis=1)
        x = _matmul(mix, lp["w_out"], tm=1024, tn=512, resid=(x, mod, l, 2 * D // 512, p_tok, Ts),
                    name="proj_out")

        h = _norm_mod(x, norm2_g[l], mod, l, 3, 4, p_tok, Ts)
        up = _matmul(h, lp["ffn_up"], tm=1024, tn=512, name="ffn_up")
        act = _ffn_act(up, ffn_conv_w[l], ffn_conv_b[l], p_tok=p_tok, tp=Tp, ts=Ts)
        x = _matmul(act, lp["ffn_down"], tm=1024, tn=512, tk=F // 2,
                    resid=(x, mod, l, 5 * D // 512, p_tok, Ts), name="ffn_down")

        new_k.append(proj[:p_tok, DA:2 * DA].reshape(Bp, Tp, H, 2, DA_D))
        new_v.append(proj[:p_tok, 2 * DA:3 * DA].reshape(Bp, Tp, H, 2 * DA_D))
        new_s.append(s_p)

    y = _final_norm(x, final_norm_g)
    return (y[:p_tok].reshape(Bp, Tp, D), y[p_tok:].reshape(Bs, Ts, D),
            jnp.stack(new_k, axis=1), jnp.stack(new_v, axis=1), jnp.stack(new_s, axis=1))
```

```python
import functools
import math

import jax
import jax.numpy as jnp
from jax import lax
from jax.experimental import pallas as pl
from jax.experimental.pallas import tpu as pltpu

F32 = jnp.float32
BF16 = jnp.bfloat16
HIGHEST = lax.Precision.HIGHEST

LANES = 128
SUBLANES = 8
VMEM_BYTES_V7X = 64 * 1024 * 1024
VMEM_BUDGET = VMEM_BYTES_V7X * 3 // 4

GRID_W = 64
DA_D = 128
RW_HEAD = 64
RW_LORA_R = 64
RW_GATE_R = 160
RW_GN_EPS = 64e-5
CM_GROUPS = 4
CM_CHUNK = 128
CONV_W = 3
ROPE_THETA = 10000.0
NORM_EPS = 1e-6
SCAN_CHUNK = 64


def _params(*sem):
    return pltpu.CompilerParams(dimension_semantics=sem, vmem_limit_bytes=VMEM_BUDGET)


def _cond_row(tok0, p_tok, ts):
    return jnp.where(tok0 < p_tok, 0, 1 + jnp.maximum(tok0 - p_tok, 0) // ts)


def _mod_kernel(c_ref, w_ref, b_ref, o_ref):
    c = c_ref[...]
    s = (c * jax.nn.sigmoid(c)).astype(BF16)
    o_ref[...] = jnp.dot(s, w_ref[...].astype(BF16), preferred_element_type=F32) + b_ref[...]


def _modulation(cond8, mod_w, mod_b, tn=512):
    L, D, N = mod_w.shape
    return pl.pallas_call(
        _mod_kernel,
        grid=(L, N // tn),
        in_specs=[pl.BlockSpec((SUBLANES, D), lambda l, j: (0, 0)),
                  pl.BlockSpec((None, D, tn), lambda l, j: (l, 0, j)),
                  pl.BlockSpec((None, 1, tn), lambda l, j: (l, 0, j))],
        out_specs=pl.BlockSpec((None, SUBLANES, tn), lambda l, j: (l, 0, j)),
        out_shape=jax.ShapeDtypeStruct((L, SUBLANES, N), F32),
        compiler_params=_params("parallel", "parallel"),
        name="modulation",
    )(cond8, mod_w, mod_b.reshape(L, 1, N))


def _norm_mod_kernel(x_ref, g_ref, sh_ref, sc_ref, o_ref, *, tm, p_tok, ts):
    row = _cond_row(pl.program_id(0) * tm, p_tok, ts)
    x = x_ref[...]
    y = x * lax.rsqrt(jnp.mean(x * x, axis=-1, keepdims=True) + NORM_EPS)
    sc = sc_ref[pl.ds(row, 1), :]
    sh = sh_ref[pl.ds(row, 1), :]
    o_ref[...] = ((y * g_ref[...]) * (1.0 + sc) + sh).astype(o_ref.dtype)


def _norm_mod(x, g, mod, layer, k_sh, k_sc, p_tok, ts, tm=256):
    n, D = x.shape
    return pl.pallas_call(
        functools.partial(_norm_mod_kernel, tm=tm, p_tok=p_tok, ts=ts),
        grid=(n // tm,),
        in_specs=[pl.BlockSpec((tm, D), lambda i: (i, 0)),
                  pl.BlockSpec((1, D), lambda i: (0, 0)),
                  pl.BlockSpec((None, SUBLANES, D), lambda i: (layer, 0, k_sh)),
                  pl.BlockSpec((None, SUBLANES, D), lambda i: (layer, 0, k_sc))],
        out_specs=pl.BlockSpec((tm, D), lambda i: (i, 0)),
        out_shape=jax.ShapeDtypeStruct((n, D), BF16),
        compiler_params=_params("parallel"),
        name="norm_mod",
    )(x, g.reshape(1, D), mod, mod)


def _final_norm_kernel(x_ref, g_ref, o_ref):
    x = x_ref[...]
    o_ref[...] = x * lax.rsqrt(jnp.mean(x * x, axis=-1, keepdims=True) + NORM_EPS) * g_ref[...]


def _final_norm(x, g, tm=256):
    n, D = x.shape
    return pl.pallas_call(
        _final_norm_kernel,
        grid=(n // tm,),
        in_specs=[pl.BlockSpec((tm, D), lambda i: (i, 0)), pl.BlockSpec((1, D), lambda i: (0, 0))],
        out_specs=pl.BlockSpec((tm, D), lambda i: (i, 0)),
        out_shape=jax.ShapeDtypeStruct((n, D), F32),
        compiler_params=_params("parallel"),
        name="final_norm",
    )(x, g.reshape(1, D))


def _mm_kernel(*refs, nk, resid, tm, p_tok, ts):
    if resid:
        a_ref, b_ref, x_ref, g_ref, o_ref = refs[:5]
    else:
        a_ref, b_ref, o_ref = refs[:3]
    acc_ref = refs[-1] if nk > 1 else None

    def finish(acc):
        if resid:
            row = _cond_row(pl.program_id(0) * tm, p_tok, ts)
            o_ref[...] = x_ref[...] + g_ref[pl.ds(row, 1), :] * acc
        else:
            o_ref[...] = acc

    part = jnp.dot(a_ref[...], b_ref[...], preferred_element_type=F32)
    if nk == 1:
        finish(part)
    else:
        k = pl.program_id(2)

        @pl.when(k == 0)
        def _():
            acc_ref[...] = part

        @pl.when((k > 0) & (k < nk - 1))
        def _():
            acc_ref[...] += part

        @pl.when(k == nk - 1)
        def _():
            finish(acc_ref[...] + part)


def _matmul(a, b, *, tm, tn, tk=None, resid=None, name):
    M, K = a.shape
    N = b.shape[1]
    tk = K if tk is None else tk
    nk = K // tk
    in_specs = [pl.BlockSpec((tm, tk), lambda i, j, k: (i, k)),
                pl.BlockSpec((tk, tn), lambda i, j, k: (k, j))]
    args = [a, b]
    p_tok = ts = 0
    if resid is not None:
        x, mod, layer, gate_blk, p_tok, ts = resid
        in_specs += [pl.BlockSpec((tm, tn), lambda i, j, k: (i, j)),
                     pl.BlockSpec((None, SUBLANES, tn), lambda i, j, k: (layer, 0, gate_blk + j))]
        args += [x, mod]
    return pl.pallas_call(
        functools.partial(_mm_kernel, nk=nk, resid=resid is not None, tm=tm, p_tok=p_tok, ts=ts),
        grid=(M // tm, N // tn, nk),
        in_specs=in_specs,
        out_specs=pl.BlockSpec((tm, tn), lambda i, j, k: (i, j)),
        out_shape=jax.ShapeDtypeStruct((M, N), F32),
        scratch_shapes=[pltpu.VMEM((tm, tn), F32)] if nk > 1 else [],
        compiler_params=_params("parallel", "parallel", "arbitrary"),
        name=name,
    )(*args)


def _rope(x, cos, sin_signed):
    lane = lax.broadcasted_iota(jnp.int32, x.shape, 1)
    width = x.shape[1]
    rot = jnp.where((lane % 64) < 32, pltpu.roll(x, width - 32, 1), pltpu.roll(x, 32, 1))
    return x * cos + rot * sin_signed


def _attn_kernel(*refs, rope, lam_init, key_block):
    if rope:
        (lam_ref, q_ref, k_ref, v_ref, kc_ref, vc_ref, cq_ref, sq_ref, ck_ref, sk_ref, g_ref, _mix_ref,
         o_ref, kr_ref, vr_ref) = refs
    else:
        lam_ref, q_ref, k_ref, v_ref, g_ref, _mix_ref, o_ref = refs
    lm = lam_ref[...]
    s1 = jnp.sum(lm[0:1] * lm[1:2], axis=-1, keepdims=True)
    s2 = jnp.sum(lm[2:3] * lm[3:4], axis=-1, keepdims=True)
    lam = jnp.exp(s1) - jnp.exp(s2) + lam_init
    qscale = DA_D ** -0.5 * math.log2(math.e)

    if rope:
        @pl.when(pl.program_id(2) == 0)
        def _():
            kr_ref[...] = _rope(k_ref[...], ck_ref[...], sk_ref[...]).astype(BF16)
            vr_ref[...] = v_ref[...].astype(BF16)

        q = (_rope(q_ref[...], cq_ref[...], sq_ref[...]) * qscale).astype(BF16)
        T = kr_ref.shape[0]
        blocks = [(kr_ref[j * key_block:(j + 1) * key_block], vr_ref[j * key_block:(j + 1) * key_block])
                  for j in range(T // key_block)]
        blocks.append((kc_ref[...].astype(BF16), vc_ref[...].astype(BF16)))
    else:
        q = (q_ref[...] * qscale).astype(BF16)
        blocks = [(k_ref[...].astype(BF16), v_ref[...].astype(BF16))]

    maps = []
    for c in range(2):
        cols = slice(c * DA_D, (c + 1) * DA_D)
        ss = [_mm1(q[:, cols], kb[:, cols], "nt") for kb, _ in blocks]
        m = functools.reduce(jnp.maximum, [jnp.max(s, axis=-1, keepdims=True) for s in ss])
        o = d = None
        for s, (_, vb) in zip(ss, blocks):
            e = jnp.exp2(s - m)
            t = jnp.sum(e, axis=-1, keepdims=True)
            u = _mm1(e.astype(BF16), vb)
            d = t if d is None else d + t
            o = u if o is None else o + u
        maps.append((o, d))
    o = maps[0][0] * (1.0 / maps[0][1]) - maps[1][0] * (lam / maps[1][1])
    y = o * lax.rsqrt(jnp.mean(o * o, axis=-1, keepdims=True) + NORM_EPS)
    o_ref[...] = (y * g_ref[...] * (1.0 - lam_init)).astype(o_ref.dtype)


def _any_spec():
    return pl.BlockSpec(memory_space=pl.ANY)


def _attention(proj, mix, lam_p, g, lam_init, *, row0, B, T, H, ctx=None, tq=256, key_block=512):
    W = 2 * DA_D
    nq = T // tq
    qb0 = row0 // tq
    kb0 = row0 // T
    rope = ctx is not None
    in_specs = [pl.BlockSpec((4, DA_D), lambda b, h, i: (0, 0)),
                pl.BlockSpec((tq, W), lambda b, h, i: (qb0 + b * nq + i, h)),
                pl.BlockSpec((T, W), lambda b, h, i: (kb0 + b, H + h)),
                pl.BlockSpec((T, W), lambda b, h, i: (kb0 + b, 2 * H + h))]
    args = [lam_p, proj, proj, proj]
    scratch = []
    if rope:
        ck, cv, layer, cos, sin = ctx
        past = ck.shape[2]
        in_specs += [pl.BlockSpec((None, None, past, W), lambda b, h, i: (b, layer, 0, h)),
                     pl.BlockSpec((None, None, past, W), lambda b, h, i: (b, layer, 0, h)),
                     pl.BlockSpec((tq, W), lambda b, h, i: (i, 0)),
                     pl.BlockSpec((tq, W), lambda b, h, i: (i, 0)),
                     pl.BlockSpec((T, W), lambda b, h, i: (0, 0)),
                     pl.BlockSpec((T, W), lambda b, h, i: (0, 0))]
        args += [ck, cv, cos, sin, cos, sin]
        scratch = [pltpu.VMEM((T, W), BF16), pltpu.VMEM((T, W), BF16)]
    in_specs += [pl.BlockSpec((1, W), lambda b, h, i: (0, 0)), _any_spec()]
    args += [g.reshape(1, W), mix]
    return pl.pallas_call(
        functools.partial(_attn_kernel, rope=rope, lam_init=lam_init, key_block=key_block),
        grid=(B, H, nq),
        in_specs=in_specs,
        out_specs=pl.BlockSpec((tq, W), lambda b, h, i: (qb0 + b * nq + i, h)),
        out_shape=jax.ShapeDtypeStruct(mix.shape, mix.dtype),
        input_output_aliases={len(args) - 1: 0},
        scratch_shapes=scratch,
        compiler_params=_params("parallel", "parallel", "arbitrary"),
        name="diff_attn_ctx" if rope else "diff_attn",
    )(*args)


def _seq_edges(tok0, tm, p_tok, tp, ts):
    t = tok0 + lax.broadcasted_iota(jnp.int32, (tm, 1), 0)
    in_p = t < p_tok
    pos = jnp.where(in_p, t % tp, jnp.maximum(t - p_tok, 0) % ts)
    length = jnp.where(in_p, tp, ts)
    return pos == 0, pos == length - 1


def _conv3(x, prev8, next8, w, first, last):
    tm = x.shape[0]
    ridx = lax.broadcasted_iota(jnp.int32, (tm, 1), 0)
    xp = jnp.where(ridx == 0, prev8[SUBLANES - 1:SUBLANES], pltpu.roll(x, 1, 0))
    xp = jnp.where(first, 0.0, xp)
    xn = jnp.where(ridx == tm - 1, next8[0:1], pltpu.roll(x, tm - 1, 0))
    xn = jnp.where(last, 0.0, xn)
    return xp * w[0:1] + x * w[1:2] + xn * w[2:3]


def _halo_specs(tm, width, col_blk, n_tok, nidx=1):
    r = tm // SUBLANES
    last = n_tok // SUBLANES - 1
    if nidx == 1:
        return [pl.BlockSpec((tm, width), lambda i: (i, col_blk)),
                pl.BlockSpec((SUBLANES, width), lambda i: (jnp.maximum(i * r - 1, 0), col_blk)),
                pl.BlockSpec((SUBLANES, width), lambda i: (jnp.minimum((i + 1) * r, last), col_blk))]
    return [pl.BlockSpec((tm, width), lambda i, j: (i, col_blk + j)),
            pl.BlockSpec((SUBLANES, width), lambda i, j: (jnp.maximum(i * r - 1, 0), col_blk + j)),
            pl.BlockSpec((SUBLANES, width), lambda i, j: (jnp.minimum((i + 1) * r, last), col_blk + j))]


def _cmlp_kernel(u_ref, v_ref, gain_ref, ws_ref, bs_ref, _mix_ref, o_ref, *, tm):
    v = v_ref[...]
    z = (v * lax.rsqrt(jnp.mean(v * v, axis=-1, keepdims=True) + NORM_EPS) * gain_ref[...]).astype(BF16)
    gw = z.shape[1] // CM_GROUPS
    for n in range(tm // CM_CHUNK):
        rows = slice(n * CM_CHUNK, (n + 1) * CM_CHUNK)
        for g in range(CM_GROUPS):
            cols = slice(g * gw, (g + 1) * gw)
            t = jnp.dot(ws_ref[g].astype(BF16), z[rows, cols], preferred_element_type=F32) + bs_ref[g]
            o_ref[rows, cols] = (u_ref[rows, cols] * t).astype(o_ref.dtype)


def _chunk_mlp(proj, mix, gain, ws, bs, *, u_blk, out_blk, tm=256):
    n_tok = proj.shape[0]
    W = gain.shape[0]
    gw = W // CM_GROUPS
    bs_b = jnp.broadcast_to(bs[:, :, None], (CM_GROUPS, CM_CHUNK, gw))
    return pl.pallas_call(
        functools.partial(_cmlp_kernel, tm=tm),
        grid=(n_tok // tm,),
        in_specs=[pl.BlockSpec((tm, W), lambda i: (i, u_blk)),
                  pl.BlockSpec((tm, W), lambda i: (i, u_blk + 1)),
                  pl.BlockSpec((1, W), lambda i: (0, 0)),
                  pl.BlockSpec((CM_GROUPS, CM_CHUNK, CM_CHUNK), lambda i: (0, 0, 0)),
                  pl.BlockSpec((CM_GROUPS, CM_CHUNK, gw), lambda i: (0, 0, 0)),
                  _any_spec()],
        out_specs=pl.BlockSpec((tm, W), lambda i: (i, out_blk)),
        out_shape=jax.ShapeDtypeStruct(mix.shape, mix.dtype),
        input_output_aliases={5: 0},
        compiler_params=_params("parallel"),
        name="chunk_mlp",
    )(proj, proj, gain.reshape(1, W), ws, bs_b, mix)


def _ffn_act_kernel(a_ref, ap_ref, an_ref, b_ref, bp_ref, bn_ref, wa_ref, wb_ref, ba_ref, bb_ref, o_ref,
                    sa_ref, sb_ref, *, tm, tc, p_tok, tp, ts):
    tok0 = pl.program_id(0) * tm
    in_p = tok0 < p_tok
    pos = jnp.where(in_p, tok0 % tp, jnp.maximum(tok0 - p_tok, 0) % ts)
    length = jnp.where(in_p, tp, ts)
    keep_prev = (pos != 0).astype(F32)
    keep_next = (pos + tm != length).astype(F32)

    def conv(x_ref, p_ref, n_ref, s_ref, w_ref, bias_ref, cols):
        s_ref[SUBLANES:SUBLANES + tm, :] = x_ref[:, cols]
        s_ref[SUBLANES - 1:SUBLANES, :] = p_ref[SUBLANES - 1:SUBLANES, cols] * keep_prev
        s_ref[SUBLANES + tm:SUBLANES + tm + 1, :] = n_ref[0:1, cols] * keep_next
        w = w_ref[:, cols]
        return (s_ref[SUBLANES - 1:SUBLANES - 1 + tm, :] * w[0:1] + x_ref[:, cols] * w[1:2]
                + s_ref[SUBLANES + 1:SUBLANES + 1 + tm, :] * w[2:3] + bias_ref[:, cols])

    def body(j, carry):
        cols = pl.ds(pl.multiple_of(j * LANES, LANES), LANES)
        ga = conv(a_ref, ap_ref, an_ref, sa_ref, wa_ref, ba_ref, cols)
        gb = conv(b_ref, bp_ref, bn_ref, sb_ref, wb_ref, bb_ref, cols)
        o_ref[:, cols] = (ga * jax.nn.sigmoid(ga) * gb).astype(o_ref.dtype)
        return carry

    lax.fori_loop(0, tc // LANES, body, 0)


def _ffn_act(up, conv_w, conv_b, *, p_tok, tp, ts, tm=256, col_blocks=2):
    n_tok, two_f = up.shape
    F = two_f // 2
    tc = F // col_blocks
    assert tp % tm == 0 and ts % tm == 0 and tc % LANES == 0
    cb = conv_b.reshape(1, two_f)
    wspec = lambda off: pl.BlockSpec((CONV_W, tc), lambda i, j: (0, off + j))
    bspec = lambda off: pl.BlockSpec((1, tc), lambda i, j: (0, off + j))
    halo = pltpu.VMEM((tm + 2 * SUBLANES, LANES), F32)
    return pl.pallas_call(
        functools.partial(_ffn_act_kernel, tm=tm, tc=tc, p_tok=p_tok, tp=tp, ts=ts),
        grid=(n_tok // tm, col_blocks),
        in_specs=(_halo_specs(tm, tc, 0, n_tok, 2) + _halo_specs(tm, tc, col_blocks, n_tok, 2)
                  + [wspec(0), wspec(col_blocks), bspec(0), bspec(col_blocks)]),
        out_specs=pl.BlockSpec((tm, tc), lambda i, j: (i, j)),
        out_shape=jax.ShapeDtypeStruct((n_tok, F), BF16),
        scratch_shapes=[halo, halo],
        compiler_params=_params("parallel", "parallel"),
        name="ffn_act",
    )(up, up, up, up, up, up, conv_w, conv_w, cb, cb)


def _split3(x):
    hi = x.astype(BF16)
    r1 = x - hi.astype(F32)
    mid = r1.astype(BF16)
    return hi, mid, (r1 - mid.astype(F32)).astype(BF16)


def _head_sum(x):
    r = lax.broadcasted_iota(jnp.int32, (3 * LANES, LANES), 0) % LANES // RW_HEAD
    c = lax.broadcasted_iota(jnp.int32, (3 * LANES, LANES), 1) // RW_HEAD
    e3 = jnp.where(r == c, 1.0, 0.0).astype(BF16)
    parts = _split3(x)
    cols = [_mm1(jnp.concatenate([p[:, j * LANES:(j + 1) * LANES] for p in parts], axis=1), e3)
            for j in range(x.shape[1] // LANES)]
    return jnp.concatenate(cols, axis=1)


def _rw_prep_kernel(r_ref, rp_ref, rn_ref, k_ref, kp_ref, kn_ref, v_ref, vp_ref, vn_ref,
                    z_ref, zp_ref, zn_ref, cw_ref, cz_ref, w0_ref, w2_ref, a0_ref, a2_ref, g2_ref,
                    kk_ref, ka_ref, rk_ref,
                    ro_ref, kko_ref, vo_ref, lw_ref, b_ref, kd_ref, gate_ref, bonus_ref,
                    *, tm, p_tok, tp, ts, C):
    first, last = _seq_edges(pl.program_id(0) * tm, tm, p_tok, tp, ts)
    cw = cw_ref[...]
    r = _conv3(r_ref[...], rp_ref[...], rn_ref[...], cw[:, 0:C], first, last)
    k = _conv3(k_ref[...], kp_ref[...], kn_ref[...], cw[:, C:2 * C], first, last)
    v = _conv3(v_ref[...], vp_ref[...], vn_ref[...], cw[:, 2 * C:3 * C], first, last)
    z = _conv3(z_ref[...], zp_ref[...], zn_ref[...], cz_ref[...], first, last)
    dec = jnp.tanh(z[:, 0:LANES]).astype(BF16)
    aa = z[:, LANES:2 * LANES].astype(BF16)
    gl = jax.nn.sigmoid(z[:, 2 * LANES:4 * LANES]).astype(BF16)
    gate_ref[...] = jnp.dot(gl, g2_ref[...].astype(BF16), preferred_element_type=F32)
    kk = k * kk_ref[...]
    kk = kk * lax.rsqrt(_head_sum(kk * kk) + 1e-12)
    ro_ref[...] = r
    kko_ref[...] = kk
    vo_ref[...] = v
    kd_sum = None
    for d in range(2):
        wl = w0_ref[d:d + 1, :] + jnp.dot(dec, w2_ref[d].astype(BF16), preferred_element_type=F32)
        lw_ref[d] = -jnp.exp(-jax.nn.softplus(-wl) - 0.5)
        a = jax.nn.sigmoid(a0_ref[d:d + 1, :] + jnp.dot(aa, a2_ref[d].astype(BF16), preferred_element_type=F32))
        b_ref[d] = kk * a
        kd = k * (1.0 + (a - 1.0) * ka_ref[...])
        kd_ref[d] = kd
        kd_sum = kd if kd_sum is None else kd_sum + kd
    bonus_ref[...] = _head_sum(r * kd_sum * rk_ref[...]) * v


def _rw_prep(proj, lp, *, r_blk, z_blk, p_tok, tp, ts, tm=256):
    n_tok = proj.shape[0]
    C = lp["rw_k_k"].shape[0]
    ZW = 4 * LANES
    full = lambda shape: pl.BlockSpec(shape, lambda i: (0,) * len(shape))
    tok = pl.BlockSpec((tm, C), lambda i: (i, 0))
    tok2 = pl.BlockSpec((2, tm, C), lambda i: (0, i, 0))
    one = jax.ShapeDtypeStruct((n_tok, C), F32)
    two = jax.ShapeDtypeStruct((2, n_tok, C), F32)
    return pl.pallas_call(
        functools.partial(_rw_prep_kernel, tm=tm, p_tok=p_tok, tp=tp, ts=ts, C=C),
        grid=(n_tok // tm,),
        in_specs=(_halo_specs(tm, C, r_blk, n_tok) + _halo_specs(tm, C, r_blk + 1, n_tok)
                  + _halo_specs(tm, C, r_blk + 2, n_tok) + _halo_specs(tm, ZW, z_blk, n_tok)
                  + [full((CONV_W, 3 * C)), full((CONV_W, ZW)), full((2, C)), full((2, LANES, C)),
                     full((2, C)), full((2, LANES, C)), full((2 * LANES, C)),
                     full((1, C)), full((1, C)), full((1, C))]),
        out_specs=[tok, tok, tok, tok2, tok2, tok2, tok, tok],
        out_shape=[one, one, one, two, two, two, one, one],
        compiler_params=_params("parallel"),
        name="rwkv_prep",
    )(*([proj] * 12), lp["cw_rkv"], lp["cw_z"], lp["rw_w0"], lp["w2_pad"], lp["rw_a0"], lp["a2_pad"],
      lp["g2_pad"], lp["rw_k_k"].reshape(1, C), lp["rw_k_a"].reshape(1, C), lp["rw_r_k"].reshape(1, C))


def _mmh(a, b, dims=(((1,), (0,)), ((), ()))):
    return lax.dot_general(a, b, dims, precision=HIGHEST, preferred_element_type=F32)


_DIMS = {"nn": ((1,), (0,)), "nt": ((1,), (1,)), "tn": ((0,), (0,))}


def _mm1(a, b, kind="nn"):
    return lax.dot_general(a, b, (_DIMS[kind], ((), ())), preferred_element_type=F32)


def _cumsum_rows(tri, x):
    t = tri.astype(BF16)
    return _mm1(jnp.concatenate([t, t, t], axis=1), jnp.concatenate(_split3(x), axis=0))


def _rw_scan_kernel(*refs, has_s0, nc):
    if has_s0:
        s0_ref, r_ref, kk_ref, v_ref, lw_ref, b_ref, kd_ref, y_ref, so_ref, st_ref = refs
    else:
        r_ref, kk_ref, v_ref, lw_ref, b_ref, kd_ref, y_ref, so_ref, st_ref = refs
    C, N = SCAN_CHUNK, RW_HEAD
    W = 2 * N
    npair = r_ref.shape[1] // W
    bwd = (pl.program_id(1) == 1).astype(jnp.int32)
    c = pl.program_id(2)

    def order(shape, dim):
        t = lax.broadcasted_iota(jnp.int32, shape, dim) % C
        return t + bwd * (C - 1 - 2 * t)

    row = lax.broadcasted_iota(jnp.int32, (W, W), 0)
    col = lax.broadcasted_iota(jnp.int32, (W, W), 1)
    same_head = (row // N) == (col // N)
    eye = (row == col).astype(F32)
    rt, ct = order((W, W), 0), order((W, W), 1)
    strict = rt > ct
    incl = rt >= ct
    level = {s: ((rt // (2 * s)) == (ct // (2 * s))) & ((rt // s) % 2 == 1) & ((ct // s) % 2 == 0)
             for s in (1, 2, 4, 8, 16, 32)}

    @pl.when(c == 0)
    def _():
        if has_s0:
            sel = (lax.broadcasted_iota(jnp.int32, (N, W), 0) == lax.broadcasted_iota(jnp.int32, (N, W), 1) % N)
            for p in range(npair):
                tiled = _mmh(s0_ref[2 * p:2 * p + 2].reshape(W, N), sel.astype(F32))
                st_ref[p] = jnp.where(same_head, tiled, 0.0)
        else:
            st_ref[...] = jnp.zeros(st_ref.shape, F32)

    lw = lw_ref[...]
    g_in = _cumsum_rows(order((C, C), 0) >= order((C, C), 1), lw)
    g_tot = jnp.sum(lw, axis=0, keepdims=True)
    e_neg = jnp.exp(-g_in)
    e_rem = jnp.exp(g_tot - g_in)
    e_tot = jnp.exp(g_tot)
    kk_t = (kk_ref[...] * jnp.exp(g_in - lw)).astype(BF16)
    r_t = (r_ref[...] * jnp.exp(g_in)).astype(BF16)
    b_t = (b_ref[...] * e_neg).astype(BF16)
    kd_t = (kd_ref[...] * e_neg).astype(BF16)
    b_end = (b_ref[...] * e_rem).astype(BF16)
    kd_end = (kd_ref[...] * e_rem).astype(BF16)
    v_t = v_ref[...].astype(BF16)
    zero = jnp.zeros((), BF16)

    def expand(x, p):
        return jnp.where(same_head, jnp.concatenate([x[:, p * W:(p + 1) * W]] * 2, axis=0), zero)

    P = range(npair)
    ak = [expand(kk_t, p) for p in P]
    bk = [jnp.concatenate([expand(b_t, p), expand(kd_t, p)], axis=0) for p in P]
    vb = [expand(v_t, p) for p in P]
    S = [st_ref[p] for p in P]
    Sb = [S[p].astype(BF16) for p in P]
    lm = [_mm1(ak[p], bk[p], "nt") for p in P]
    L = [jnp.where(strict, lm[p][:, :W], 0.0) for p in P]
    Lb = [L[p].astype(BF16) for p in P]
    M = [jnp.where(strict, lm[p][:, W:], 0.0).astype(BF16) for p in P]
    rhs = [(_mm1(ak[p], Sb[p], "nt") + _mm1(M[p], vb[p])).astype(BF16) for p in P]
    X = [eye - jnp.where(level[1], L[p], 0.0) for p in P]
    for s in (2, 4, 8, 16, 32):
        Xb = [X[p].astype(BF16) for p in P]
        t = [_mm1(jnp.where(level[s], Lb[p], zero), Xb[p]).astype(BF16) for p in P]
        X = [X[p] - _mm1(Xb[p], t[p]) for p in P]
    Ub = [_mm1(X[p].astype(BF16), rhs[p]).astype(BF16) for p in P]
    for p in P:
        st_ref[p] = (S[p] * e_tot[:, p * W:(p + 1) * W] - _mm1(Ub[p], expand(b_end, p), "tn")
                     + _mm1(vb[p], expand(kd_end, p), "tn"))
    ar = [expand(r_t, p) for p in P]
    nn = [_mm1(ar[p], bk[p], "nt") for p in P]
    nb = [jnp.where(incl, nn[p][:, :W], 0.0).astype(BF16) for p in P]
    nk = [jnp.where(incl, nn[p][:, W:], 0.0).astype(BF16) for p in P]
    for p in P:
        y = _mm1(ar[p], Sb[p], "nt") - _mm1(nb[p], Ub[p]) + _mm1(nk[p], vb[p])
        y_ref[:, p * W:(p + 1) * W] = y[:C] + y[C:]

    @pl.when(c == nc - 1)
    def _():
        fold = (lax.broadcasted_iota(jnp.int32, (W, N), 0) % N == lax.broadcasted_iota(jnp.int32, (W, N), 1))
        for p in range(npair):
            so_ref[2 * p:2 * p + 2] = _mmh(st_ref[p], fold.astype(F32)).reshape(2, N, N)


def _rw_scan(r, kk, v, lw, b, kd, *, row0, B, T, s0=None, layer=0):
    C = r.shape[1]
    H = C // RW_HEAD
    CH = SCAN_CHUNK
    nc = T // CH
    blk0 = row0 // CH

    def chunk(d, c):
        return jnp.where(d == 1, nc - 1 - c, c)

    one = pl.BlockSpec((CH, C), lambda bi, d, c: (blk0 + bi * nc + chunk(d, c), 0))
    two = pl.BlockSpec((None, CH, C), lambda bi, d, c: (d, blk0 + bi * nc + chunk(d, c), 0))
    in_specs = [one, one, one, two, two, two]
    args = [r, kk, v, lw, b, kd]
    if s0 is not None:
        in_specs = [pl.BlockSpec((None, None, None, H, RW_HEAD, RW_HEAD),
                                 lambda bi, d, c: (bi, layer, d, 0, 0, 0))] + in_specs
        args = [s0] + args
    return pl.pallas_call(
        functools.partial(_rw_scan_kernel, has_s0=s0 is not None, nc=nc),
        grid=(B, 2, nc),
        in_specs=in_specs,
        out_specs=[pl.BlockSpec((None, CH, C), lambda bi, d, c: (d, bi * nc + chunk(d, c), 0)),
                   pl.BlockSpec((None, None, H, RW_HEAD, RW_HEAD), lambda bi, d, c: (bi, d, 0, 0, 0))],
        out_shape=[jax.ShapeDtypeStruct((2, B * T, C), F32),
                   jax.ShapeDtypeStruct((B, 2, H, RW_HEAD, RW_HEAD), F32)],
        scratch_shapes=[pltpu.VMEM((H // 2, 2 * RW_HEAD, 2 * RW_HEAD), F32)],
        compiler_params=_params("parallel", "parallel", "arbitrary"),
        name="rwkv_scan_ctx" if s0 is not None else "rwkv_scan",
    )(*args)


def _rw_post_kernel(y_ref, bonus_ref, gate_ref, g_ref, b_ref, _mix_ref, o_ref):
    y = y_ref[0] + y_ref[1]
    inv_n = 1.0 / RW_HEAD
    mu = _head_sum(y) * inv_n
    yc = y - mu
    var = _head_sum(yc * yc) * inv_n
    yn = yc * lax.rsqrt(var + RW_GN_EPS) * g_ref[...] + b_ref[...]
    o_ref[...] = ((yn + bonus_ref[...]) * gate_ref[...]).astype(o_ref.dtype)


def _rw_post(y, bonus, gate, mix, gn_g, gn_b, *, row0, out_blk, tm=256):
    n_rows = y.shape[1]
    C = bonus.shape[1]
    r0 = row0 // tm
    tok = pl.BlockSpec((tm, C), lambda i: (r0 + i, 0))
    vec = pl.BlockSpec((1, C), lambda i: (0, 0))
    return pl.pallas_call(
        _rw_post_kernel,
        grid=(n_rows // tm,),
        in_specs=[pl.BlockSpec((2, tm, C), lambda i: (0, i, 0)), tok, tok, vec, vec, _any_spec()],
        out_specs=pl.BlockSpec((tm, C), lambda i: (r0 + i, out_blk)),
        out_shape=jax.ShapeDtypeStruct(mix.shape, mix.dtype),
        input_output_aliases={5: 0},
        compiler_params=_params("parallel"),
        name="rwkv_post",
    )(y, bonus, gate, gn_g.reshape(1, C), gn_b.reshape(1, C), mix)


def _pad_rows(w, rows, at):
    return jnp.zeros((rows, w.shape[1]), w.dtype).at[at:at + w.shape[0]].set(w)


def _layer_weights(l, D, w_in, rw_conv_w, rw_w2, rw_a2, rw_g2, w_out, ffn_up, ffn_down):
    DA = D // 2
    C = D // 4
    o_rw = 3 * DA
    o_z = o_rw + 3 * C
    n_z = 4 * RW_LORA_R + RW_GATE_R
    o_uv = o_z + n_z
    zpad = 4 * LANES - n_z
    perm = lambda w: jnp.concatenate(
        [w[:, :o_z], w[:, o_uv:], w[:, o_z:o_uv], jnp.zeros((w.shape[0], zpad), w.dtype)], axis=1)
    cw = rw_conv_w[l]
    return dict(
        w_in=perm(w_in[l]).astype(BF16),
        cw_rkv=cw[:, :3 * C],
        cw_z=jnp.concatenate([cw[:, 3 * C:], jnp.zeros((CONV_W, zpad), F32)], axis=1),
        w2_pad=jnp.stack([_pad_rows(rw_w2[l, d], LANES, d * RW_LORA_R) for d in range(2)]),
        a2_pad=jnp.stack([_pad_rows(rw_a2[l, d], LANES, d * RW_LORA_R) for d in range(2)]),
        g2_pad=_pad_rows(rw_g2[l], 2 * LANES, 0),
        w_out=w_out[l].astype(BF16),
        ffn_up=ffn_up[l].astype(BF16),
        ffn_down=ffn_down[l].astype(BF16),
    )


def _rope_tables(T):
    n = DA_D // 4
    inv = ROPE_THETA ** (-jnp.arange(n, dtype=F32) / n)
    rows = T // GRID_W
    row = jnp.repeat(jnp.arange(rows), GRID_W).astype(F32)
    col = jnp.tile(jnp.arange(GRID_W), rows).astype(F32)
    sign = jnp.concatenate([-jnp.ones((n,), F32), jnp.ones((n,), F32)])
    cs, sn = [], []
    for pos in (row, col):
        ang = pos[:, None] * inv[None, :]
        cs.append(jnp.concatenate([jnp.cos(ang), jnp.cos(ang)], axis=1))
        sn.append(jnp.concatenate([jnp.sin(ang), jnp.sin(ang)], axis=1) * sign[None, :])
    cos = jnp.concatenate(cs, axis=1)
    sin = jnp.concatenate(sn, axis=1)
    return jnp.tile(cos, (1, 2)), jnp.tile(sin, (1, 2))


def kernel(x_prompt, x_sample, cache_da_k, cache_da_v, state_rwkv, c, c_ctx, mod_w, mod_b, norm1_g, norm2_g, w_in, da_lambda, da_subln_g, rw_conv_w, rw_w0, rw_w2, rw_a0, rw_a2, rw_g2, rw_k_k, rw_k_a, rw_r_k, rw_gn_g, rw_gn_b, cm_norm_g, cm_ws, cm_bs, w_out, ffn_up, ffn_conv_w, ffn_conv_b, ffn_down, final_norm_g):
    Bp, Tp, D = x_prompt.shape
    Bs, Ts, _ = x_sample.shape
    L = mod_w.shape[0]
    past = cache_da_k.shape[2]
    DA = D // 2
    H = DA // (2 * DA_D)
    C = D // 4
    F = ffn_down.shape[1]
    p_tok, s_tok = Bp * Tp, Bs * Ts
    assert Bs + 1 <= SUBLANES and p_tok % Ts == 0

    x = jnp.concatenate([x_prompt.reshape(p_tok, D), x_sample.reshape(s_tok, D)], axis=0)
    cond8 = jnp.concatenate([c_ctx[None, :], c, jnp.zeros((SUBLANES - 1 - Bs, D), F32)], axis=0)
    mod = _modulation(cond8, mod_w, mod_b)
    ck4 = cache_da_k.reshape(Bs, L, past, DA)
    cv4 = cache_da_v.reshape(Bs, L, past, DA)
    cos, sin = _rope_tables(Ts)
    blk = dict(r=3 * DA // C, u=(3 * DA + 3 * C) // C, z=(3 * DA + 5 * C) // (4 * LANES))

    new_k, new_v, new_s = [], [], []
    for l in range(L):
        lw_ = _layer_weights(l, D, w_in, rw_conv_w, rw_w2, rw_a2, rw_g2, w_out, ffn_up, ffn_down)
        lp = dict(lw_, rw_w0=rw_w0[l], rw_a0=rw_a0[l], rw_k_k=rw_k_k[l], rw_k_a=rw_k_a[l], rw_r_k=rw_r_k[l])
        lam_init = 0.8 - 0.6 * math.exp(-0.3 * l)

        h = _norm_mod(x, norm1_g[l], mod, l, 0, 1, p_tok, Ts)
        proj = _matmul(h, lp["w_in"], tm=1024, tn=512, name="proj_in")

        mix = jnp.zeros((p_tok + s_tok, D), BF16)
        mix = _attention(proj, mix, da_lambda[l], da_subln_g[l], lam_init, row0=0, B=Bp, T=Tp, H=H)
        mix = _attention(proj, mix, da_lambda[l], da_subln_g[l], lam_init, row0=p_tok, B=Bs, T=Ts, H=H,
                         ctx=(ck4, cv4, l, cos, sin))
        r_, kk_, v_, lg_, b_, kd_, gate_, bonus_ = _rw_prep(proj, lp, r_blk=blk["r"], z_blk=blk["z"],
                                                            p_tok=p_tok, tp=Tp, ts=Ts)
        y_p, s_p = _rw_scan(r_, kk_, v_, lg_, b_, kd_, row0=0, B=Bp, T=Tp)
        y_s, _ = _rw_scan(r_, kk_, v_, lg_, b_, kd_, row0=p_tok, B=Bs, T=Ts, s0=state_rwkv, layer=l)
        mix = _rw_post(y_p, bonus_, gate_, mix, rw_gn_g[l], rw_gn_b[l], row0=0, out_blk=DA // C)
        mix = _rw_post(y_s, bonus_, gate_, mix, rw_gn_g[l], rw_gn_b[l], row0=p_tok, out_blk=DA // C)
        mix = _chunk_mlp(proj, mix, cm_norm_g[l], cm_ws[l], cm_bs[l], u_blk=blk["u"], out_blk=DA // C + 1)

        x = _matmul(mix, lp["w_out"], tm=1024, tn=512, resid=(x, mod, l, 2 * D // 512, p_tok, Ts),
                    name="proj_out")

        h = _norm_mod(x, norm2_g[l], mod, l, 3, 4, p_tok, Ts)
        up = _matmul(h, lp["ffn_up"], tm=1024, tn=512, name="ffn_up")
        act = _ffn_act(up, ffn_conv_w[l], ffn_conv_b[l], p_tok=p_tok, tp=Tp, ts=Ts)
        x = _matmul(act, lp["ffn_down"], tm=1024, tn=512, tk=F // 2,
                    resid=(x, mod, l, 5 * D // 512, p_tok, Ts), name="ffn_down")

        new_k.append(proj[:p_tok, DA:2 * DA].reshape(Bp, Tp, H, 2, DA_D))
        new_v.append(proj[:p_tok, 2 * DA:3 * DA].reshape(Bp, Tp, H, 2 * DA_D))
        new_s.append(s_p)

    y = _final_norm(x, final_norm_g)
    return (y[:p_tok].reshape(Bp, Tp, D), y[p_tok:].reshape(Bs, Ts, D),
            jnp.stack(new_k, axis=1), jnp.stack(new_v, axis=1), jnp.stack(new_s, axis=1))
```

```python
import functools
import math

import jax
import jax.numpy as jnp
from jax import lax
from jax.experimental import pallas as pl
from jax.experimental.pallas import tpu as pltpu

F32 = jnp.float32
BF16 = jnp.bfloat16
HIGHEST = lax.Precision.HIGHEST

LANES = 128
SUBLANES = 8
VMEM_BYTES_V7X = 64 * 1024 * 1024
VMEM_BUDGET = VMEM_BYTES_V7X * 3 // 4

GRID_W = 64
DA_D = 128
RW_HEAD = 64
RW_LORA_R = 64
RW_GATE_R = 160
RW_GN_EPS = 64e-5
CM_GROUPS = 4
CM_CHUNK = 128
CONV_W = 3
ROPE_THETA = 10000.0
NORM_EPS = 1e-6
SCAN_CHUNK = 64


def _params(*sem):
    return pltpu.CompilerParams(dimension_semantics=sem, vmem_limit_bytes=VMEM_BUDGET)


def _any_spec():
    return pl.BlockSpec(memory_space=pl.ANY)


def _cond_row(tok0, p_tok, ts):
    return jnp.where(tok0 < p_tok, 0, 1 + jnp.maximum(tok0 - p_tok, 0) // ts)


def _mod_kernel(c_ref, w_ref, b_ref, o_ref):
    c = c_ref[...]
    s = (c * jax.nn.sigmoid(c)).astype(BF16)
    o_ref[...] = jnp.dot(s, w_ref[...].astype(BF16), preferred_element_type=F32) + b_ref[...]


def _modulation(cond8, mod_w, mod_b, tn=512):
    L, D, N = mod_w.shape
    return pl.pallas_call(
        _mod_kernel,
        grid=(L, N // tn),
        in_specs=[pl.BlockSpec((SUBLANES, D), lambda l, j: (0, 0)),
                  pl.BlockSpec((None, D, tn), lambda l, j: (l, 0, j)),
                  pl.BlockSpec((None, 1, tn), lambda l, j: (l, 0, j))],
        out_specs=pl.BlockSpec((None, SUBLANES, tn), lambda l, j: (l, 0, j)),
        out_shape=jax.ShapeDtypeStruct((L, SUBLANES, N), F32),
        compiler_params=_params("parallel", "parallel"),
        name="modulation",
    )(cond8, mod_w, mod_b.reshape(L, 1, N))


def _norm_mod_kernel(x_ref, g_ref, sh_ref, sc_ref, o_ref, *, tm, p_tok, ts):
    row = _cond_row(pl.program_id(0) * tm, p_tok, ts)
    x = x_ref[...]
    y = x * lax.rsqrt(jnp.mean(x * x, axis=-1, keepdims=True) + NORM_EPS)
    sc = sc_ref[pl.ds(row, 1), :]
    sh = sh_ref[pl.ds(row, 1), :]
    o_ref[...] = ((y * g_ref[...]) * (1.0 + sc) + sh).astype(o_ref.dtype)


def _norm_mod(x, g, mod, layer, k_sh, k_sc, p_tok, ts, tm=256):
    n, D = x.shape
    return pl.pallas_call(
        functools.partial(_norm_mod_kernel, tm=tm, p_tok=p_tok, ts=ts),
        grid=(n // tm,),
        in_specs=[pl.BlockSpec((tm, D), lambda i: (i, 0)),
                  pl.BlockSpec((1, D), lambda i: (0, 0)),
                  pl.BlockSpec((None, SUBLANES, D), lambda i: (layer, 0, k_sh)),
                  pl.BlockSpec((None, SUBLANES, D), lambda i: (layer, 0, k_sc))],
        out_specs=pl.BlockSpec((tm, D), lambda i: (i, 0)),
        out_shape=jax.ShapeDtypeStruct((n, D), BF16),
        compiler_params=_params("parallel"),
        name="norm_mod",
    )(x, g.reshape(1, D), mod, mod)


def _final_norm_kernel(x_ref, g_ref, o_ref):
    x = x_ref[...]
    o_ref[...] = x * lax.rsqrt(jnp.mean(x * x, axis=-1, keepdims=True) + NORM_EPS) * g_ref[...]


def _final_norm(x, g, *, row0, n_rows, tm=256):
    D = x.shape[1]
    r0 = row0 // tm
    return pl.pallas_call(
        _final_norm_kernel,
        grid=(n_rows // tm,),
        in_specs=[pl.BlockSpec((tm, D), lambda i: (r0 + i, 0)), pl.BlockSpec((1, D), lambda i: (0, 0))],
        out_specs=pl.BlockSpec((tm, D), lambda i: (i, 0)),
        out_shape=jax.ShapeDtypeStruct((n_rows, D), F32),
        compiler_params=_params("parallel"),
        name="final_norm",
    )(x, g.reshape(1, D))


def _mm_kernel(*refs, nk, resid, tm, p_tok, ts):
    if resid:
        a_ref, b_ref, x_ref, g_ref, o_ref = refs[:5]
    else:
        a_ref, b_ref, o_ref = refs[:3]
    acc_ref = refs[-1] if nk > 1 else None

    def finish(acc):
        if resid:
            row = _cond_row(pl.program_id(0) * tm, p_tok, ts)
            o_ref[...] = x_ref[...] + g_ref[pl.ds(row, 1), :] * acc
        else:
            o_ref[...] = acc

    part = jnp.dot(a_ref[...], b_ref[...], preferred_element_type=F32)
    if nk == 1:
        finish(part)
    else:
        k = pl.program_id(2)

        @pl.when(k == 0)
        def _():
            acc_ref[...] = part

        @pl.when((k > 0) & (k < nk - 1))
        def _():
            acc_ref[...] += part

        @pl.when(k == nk - 1)
        def _():
            finish(acc_ref[...] + part)


def _matmul(a, b, *, tm, tn, tk=None, resid=None, name):
    M, K = a.shape
    N = b.shape[1]
    tk = K if tk is None else tk
    nk = K // tk
    in_specs = [pl.BlockSpec((tm, tk), lambda i, j, k: (i, k)),
                pl.BlockSpec((tk, tn), lambda i, j, k: (k, j))]
    args = [a, b]
    p_tok = ts = 0
    if resid is not None:
        x, mod, layer, gate_blk, p_tok, ts = resid
        in_specs += [pl.BlockSpec((tm, tn), lambda i, j, k: (i, j)),
                     pl.BlockSpec((None, SUBLANES, tn), lambda i, j, k: (layer, 0, gate_blk + j))]
        args += [x, mod]
    return pl.pallas_call(
        functools.partial(_mm_kernel, nk=nk, resid=resid is not None, tm=tm, p_tok=p_tok, ts=ts),
        grid=(M // tm, N // tn, nk),
        in_specs=in_specs,
        out_specs=pl.BlockSpec((tm, tn), lambda i, j, k: (i, j)),
        out_shape=jax.ShapeDtypeStruct((M, N), F32),
        scratch_shapes=[pltpu.VMEM((tm, tn), F32)] if nk > 1 else [],
        compiler_params=_params("parallel", "parallel", "arbitrary"),
        name=name,
    )(*args)


def _store_kv_kernel(k_ref, v_ref, _ko_in, _vo_in, ko_ref, vo_ref):
    ko_ref[...] = k_ref[...]
    vo_ref[...] = v_ref[...]


def _store_kv(proj, ko, vo, layer, *, Bp, Tp, DA):
    out = pl.BlockSpec((None, None, Tp, DA), lambda b: (b, layer, 0, 0))
    return pl.pallas_call(
        _store_kv_kernel,
        grid=(Bp,),
        in_specs=[pl.BlockSpec((Tp, DA), lambda b: (b, 1)), pl.BlockSpec((Tp, DA), lambda b: (b, 2)),
                  _any_spec(), _any_spec()],
        out_specs=[out, out],
        out_shape=[jax.ShapeDtypeStruct(ko.shape, ko.dtype), jax.ShapeDtypeStruct(vo.shape, vo.dtype)],
        input_output_aliases={2: 0, 3: 1},
        compiler_params=_params("parallel"),
        name="store_kv",
    )(proj, proj, ko, vo)


def _rope(x, cos, sin_signed):
    lane = lax.broadcasted_iota(jnp.int32, x.shape, 1)
    width = x.shape[1]
    rot = jnp.where((lane % 64) < 32, pltpu.roll(x, width - 32, 1), pltpu.roll(x, 32, 1))
    return x * cos + rot * sin_signed


def _attn_kernel(*refs, rope, lam_init, key_block):
    if rope:
        (lam_ref, q_ref, k_ref, v_ref, kc_ref, vc_ref, cq_ref, sq_ref, ck_ref, sk_ref, g_ref, _mix_ref,
         o_ref, kr_ref, vr_ref) = refs
    else:
        lam_ref, q_ref, k_ref, v_ref, g_ref, _mix_ref, o_ref = refs
    lm = lam_ref[...]
    s1 = jnp.sum(lm[0:1] * lm[1:2], axis=-1, keepdims=True)
    s2 = jnp.sum(lm[2:3] * lm[3:4], axis=-1, keepdims=True)
    lam = jnp.exp(s1) - jnp.exp(s2) + lam_init
    qscale = DA_D ** -0.5 * math.log2(math.e)

    if rope:
        @pl.when(pl.program_id(2) == 0)
        def _():
            kr_ref[...] = _rope(k_ref[...], ck_ref[...], sk_ref[...]).astype(BF16)
            vr_ref[...] = v_ref[...].astype(BF16)

        q = (_rope(q_ref[...], cq_ref[...], sq_ref[...]) * qscale).astype(BF16)
        T = kr_ref.shape[0]
        blocks = [(kr_ref[j * key_block:(j + 1) * key_block], vr_ref[j * key_block:(j + 1) * key_block])
                  for j in range(T // key_block)]
        blocks.append((kc_ref[...].astype(BF16), vc_ref[...].astype(BF16)))
    else:
        q = (q_ref[...] * qscale).astype(BF16)
        blocks = [(k_ref[...].astype(BF16), v_ref[...].astype(BF16))]

    maps = []
    for c in range(2):
        cols = slice(c * DA_D, (c + 1) * DA_D)
        ss = [_mm1(q[:, cols], kb[:, cols], "nt") for kb, _ in blocks]
        m = functools.reduce(jnp.maximum, [jnp.max(s, axis=-1, keepdims=True) for s in ss])
        o = d = None
        for s, (_, vb) in zip(ss, blocks):
            e = jnp.exp2(s - m)
            t = jnp.sum(e, axis=-1, keepdims=True)
            u = _mm1(e.astype(BF16), vb)
            d = t if d is None else d + t
            o = u if o is None else o + u
        maps.append((o, d))
    o = maps[0][0] * (1.0 / maps[0][1]) - maps[1][0] * (lam / maps[1][1])
    y = o * lax.rsqrt(jnp.mean(o * o, axis=-1, keepdims=True) + NORM_EPS)
    o_ref[...] = (y * g_ref[...] * (1.0 - lam_init)).astype(o_ref.dtype)


def _attention(proj, mix, lam_p, g, lam_init, *, row0, B, T, H, ctx=None, tq=256, key_block=512):
    W = 2 * DA_D
    nq = T // tq
    qb0 = row0 // tq
    kb0 = row0 // T
    rope = ctx is not None
    in_specs = [pl.BlockSpec((4, DA_D), lambda b, h, i: (0, 0)),
                pl.BlockSpec((tq, W), lambda b, h, i: (qb0 + b * nq + i, h)),
                pl.BlockSpec((T, W), lambda b, h, i: (kb0 + b, H + h)),
                pl.BlockSpec((T, W), lambda b, h, i: (kb0 + b, 2 * H + h))]
    args = [lam_p, proj, proj, proj]
    scratch = []
    if rope:
        ck, cv, layer, cos, sin = ctx
        past = ck.shape[2]
        in_specs += [pl.BlockSpec((None, None, past, W), lambda b, h, i: (b, layer, 0, h)),
                     pl.BlockSpec((None, None, past, W), lambda b, h, i: (b, layer, 0, h)),
                     pl.BlockSpec((tq, W), lambda b, h, i: (i, 0)),
                     pl.BlockSpec((tq, W), lambda b, h, i: (i, 0)),
                     pl.BlockSpec((T, W), lambda b, h, i: (0, 0)),
                     pl.BlockSpec((T, W), lambda b, h, i: (0, 0))]
        args += [ck, cv, cos, sin, cos, sin]
        scratch = [pltpu.VMEM((T, W), BF16), pltpu.VMEM((T, W), BF16)]
    in_specs += [pl.BlockSpec((1, W), lambda b, h, i: (0, 0)), _any_spec()]
    args += [g.reshape(1, W), mix]
    return pl.pallas_call(
        functools.partial(_attn_kernel, rope=rope, lam_init=lam_init, key_block=key_block),
        grid=(B, H, nq),
        in_specs=in_specs,
        out_specs=pl.BlockSpec((tq, W), lambda b, h, i: (qb0 + b * nq + i, h)),
        out_shape=jax.ShapeDtypeStruct(mix.shape, mix.dtype),
        input_output_aliases={len(args) - 1: 0},
        scratch_shapes=scratch,
        compiler_params=_params("parallel", "parallel", "arbitrary"),
        name="diff_attn_ctx" if rope else "diff_attn",
    )(*args)


def _seq_edges(tok0, tm, p_tok, tp, ts):
    t = tok0 + lax.broadcasted_iota(jnp.int32, (tm, 1), 0)
    in_p = t < p_tok
    pos = jnp.where(in_p, t % tp, jnp.maximum(t - p_tok, 0) % ts)
    length = jnp.where(in_p, tp, ts)
    return pos == 0, pos == length - 1


def _conv3(x, prev8, next8, w, first, last):
    tm = x.shape[0]
    ridx = lax.broadcasted_iota(jnp.int32, (tm, 1), 0)
    xp = jnp.where(ridx == 0, prev8[SUBLANES - 1:SUBLANES], pltpu.roll(x, 1, 0))
    xp = jnp.where(first, 0.0, xp)
    xn = jnp.where(ridx == tm - 1, next8[0:1], pltpu.roll(x, tm - 1, 0))
    xn = jnp.where(last, 0.0, xn)
    return xp * w[0:1] + x * w[1:2] + xn * w[2:3]


def _halo_specs(tm, width, col_blk, n_tok, nidx=1):
    r = tm // SUBLANES
    last = n_tok // SUBLANES - 1
    if nidx == 1:
        return [pl.BlockSpec((tm, width), lambda i: (i, col_blk)),
                pl.BlockSpec((SUBLANES, width), lambda i: (jnp.maximum(i * r - 1, 0), col_blk)),
                pl.BlockSpec((SUBLANES, width), lambda i: (jnp.minimum((i + 1) * r, last), col_blk))]
    return [pl.BlockSpec((tm, width), lambda i, j: (i, col_blk + j)),
            pl.BlockSpec((SUBLANES, width), lambda i, j: (jnp.maximum(i * r - 1, 0), col_blk + j)),
            pl.BlockSpec((SUBLANES, width), lambda i, j: (jnp.minimum((i + 1) * r, last), col_blk + j))]


def _cmlp_kernel(u_ref, v_ref, gain_ref, ws_ref, bs_ref, _mix_ref, o_ref, *, tm):
    v = v_ref[...]
    z = (v * lax.rsqrt(jnp.mean(v * v, axis=-1, keepdims=True) + NORM_EPS) * gain_ref[...]).astype(BF16)
    gw = z.shape[1] // CM_GROUPS
    for n in range(tm // CM_CHUNK):
        rows = slice(n * CM_CHUNK, (n + 1) * CM_CHUNK)
        for g in range(CM_GROUPS):
            cols = slice(g * gw, (g + 1) * gw)
            t = jnp.dot(ws_ref[g].astype(BF16), z[rows, cols], preferred_element_type=F32) + bs_ref[g]
            o_ref[rows, cols] = (u_ref[rows, cols] * t).astype(o_ref.dtype)


def _chunk_mlp(proj, mix, gain, ws, bs, *, u_blk, out_blk, tm=256):
    n_tok = proj.shape[0]
    W = gain.shape[0]
    gw = W // CM_GROUPS
    bs_b = jnp.broadcast_to(bs[:, :, None], (CM_GROUPS, CM_CHUNK, gw))
    return pl.pallas_call(
        functools.partial(_cmlp_kernel, tm=tm),
        grid=(n_tok // tm,),
        in_specs=[pl.BlockSpec((tm, W), lambda i: (i, u_blk)),
                  pl.BlockSpec((tm, W), lambda i: (i, u_blk + 1)),
                  pl.BlockSpec((1, W), lambda i: (0, 0)),
                  pl.BlockSpec((CM_GROUPS, CM_CHUNK, CM_CHUNK), lambda i: (0, 0, 0)),
                  pl.BlockSpec((CM_GROUPS, CM_CHUNK, gw), lambda i: (0, 0, 0)),
                  _any_spec()],
        out_specs=pl.BlockSpec((tm, W), lambda i: (i, out_blk)),
        out_shape=jax.ShapeDtypeStruct(mix.shape, mix.dtype),
        input_output_aliases={5: 0},
        compiler_params=_params("parallel"),
        name="chunk_mlp",
    )(proj, proj, gain.reshape(1, W), ws, bs_b, mix)


def _ffn_act_kernel(a_ref, ap_ref, an_ref, b_ref, bp_ref, bn_ref, wa_ref, wb_ref, ba_ref, bb_ref, o_ref,
                    sa_ref, sb_ref, *, tm, tc, p_tok, tp, ts):
    tok0 = pl.program_id(0) * tm
    in_p = tok0 < p_tok
    pos = jnp.where(in_p, tok0 % tp, jnp.maximum(tok0 - p_tok, 0) % ts)
    length = jnp.where(in_p, tp, ts)
    keep_prev = (pos != 0).astype(F32)
    keep_next = (pos + tm != length).astype(F32)

    def conv(x_ref, p_ref, n_ref, s_ref, w_ref, bias_ref, cols):
        s_ref[SUBLANES:SUBLANES + tm, :] = x_ref[:, cols]
        s_ref[SUBLANES - 1:SUBLANES, :] = p_ref[SUBLANES - 1:SUBLANES, cols] * keep_prev
        s_ref[SUBLANES + tm:SUBLANES + tm + 1, :] = n_ref[0:1, cols] * keep_next
        w = w_ref[:, cols]
        return (s_ref[SUBLANES - 1:SUBLANES - 1 + tm, :] * w[0:1] + x_ref[:, cols] * w[1:2]
                + s_ref[SUBLANES + 1:SUBLANES + 1 + tm, :] * w[2:3] + bias_ref[:, cols])

    def body(j, carry):
        cols = pl.ds(pl.multiple_of(j * LANES, LANES), LANES)
        ga = conv(a_ref, ap_ref, an_ref, sa_ref, wa_ref, ba_ref, cols)
        gb = conv(b_ref, bp_ref, bn_ref, sb_ref, wb_ref, bb_ref, cols)
        o_ref[:, cols] = (ga * jax.nn.sigmoid(ga) * gb).astype(o_ref.dtype)
        return carry

    lax.fori_loop(0, tc // LANES, body, 0)


def _ffn_act(up, conv_w, conv_b, *, p_tok, tp, ts, tm=256, col_blocks=2):
    n_tok, two_f = up.shape
    F = two_f // 2
    tc = F // col_blocks
    assert tp % tm == 0 and ts % tm == 0 and tc % LANES == 0
    cb = conv_b.reshape(1, two_f)
    wspec = lambda off: pl.BlockSpec((CONV_W, tc), lambda i, j: (0, off + j))
    bspec = lambda off: pl.BlockSpec((1, tc), lambda i, j: (0, off + j))
    halo = pltpu.VMEM((tm + 2 * SUBLANES, LANES), F32)
    return pl.pallas_call(
        functools.partial(_ffn_act_kernel, tm=tm, tc=tc, p_tok=p_tok, tp=tp, ts=ts),
        grid=(n_tok // tm, col_blocks),
        in_specs=(_halo_specs(tm, tc, 0, n_tok, 2) + _halo_specs(tm, tc, col_blocks, n_tok, 2)
                  + [wspec(0), wspec(col_blocks), bspec(0), bspec(col_blocks)]),
        out_specs=pl.BlockSpec((tm, tc), lambda i, j: (i, j)),
        out_shape=jax.ShapeDtypeStruct((n_tok, F), BF16),
        scratch_shapes=[halo, halo],
        compiler_params=_params("parallel", "parallel"),
        name="ffn_act",
    )(up, up, up, up, up, up, conv_w, conv_w, cb, cb)


_DIMS = {"nn": ((1,), (0,)), "nt": ((1,), (1,)), "tn": ((0,), (0,))}


def _mm1(a, b, kind="nn"):
    return lax.dot_general(a, b, (_DIMS[kind], ((), ())), preferred_element_type=F32)


def _mmh(a, b, dims=(((1,), (0,)), ((), ()))):
    return lax.dot_general(a, b, dims, precision=HIGHEST, preferred_element_type=F32)


def _split3(x):
    hi = x.astype(BF16)
    r1 = x - hi.astype(F32)
    mid = r1.astype(BF16)
    return hi, mid, (r1 - mid.astype(F32)).astype(BF16)


def _head_sum(x):
    r = lax.broadcasted_iota(jnp.int32, (3 * LANES, LANES), 0) % LANES // RW_HEAD
    c = lax.broadcasted_iota(jnp.int32, (3 * LANES, LANES), 1) // RW_HEAD
    e3 = jnp.where(r == c, 1.0, 0.0).astype(BF16)
    parts = _split3(x)
    cols = [_mm1(jnp.concatenate([p[:, j * LANES:(j + 1) * LANES] for p in parts], axis=1), e3)
            for j in range(x.shape[1] // LANES)]
    return jnp.concatenate(cols, axis=1)


def _cumsum_rows(tri, x):
    t = tri.astype(BF16)
    return _mm1(jnp.concatenate([t, t, t], axis=1), jnp.concatenate(_split3(x), axis=0))


def _rw_prep_kernel(r_ref, rp_ref, rn_ref, k_ref, kp_ref, kn_ref, v_ref, vp_ref, vn_ref,
                    z_ref, zp_ref, zn_ref, cw_ref, cz_ref, w0_ref, w2_ref, a0_ref, a2_ref, g2_ref,
                    kk_ref, ka_ref, rk_ref,
                    ro_ref, kko_ref, vo_ref, lw_ref, b_ref, kd_ref, gate_ref, bonus_ref,
                    *, tm, p_tok, tp, ts, C):
    first, last = _seq_edges(pl.program_id(0) * tm, tm, p_tok, tp, ts)
    cw = cw_ref[...]
    r = _conv3(r_ref[...], rp_ref[...], rn_ref[...], cw[:, 0:C], first, last)
    k = _conv3(k_ref[...], kp_ref[...], kn_ref[...], cw[:, C:2 * C], first, last)
    v = _conv3(v_ref[...], vp_ref[...], vn_ref[...], cw[:, 2 * C:3 * C], first, last)
    z = _conv3(z_ref[...], zp_ref[...], zn_ref[...], cz_ref[...], first, last)
    dec = jnp.tanh(z[:, 0:LANES]).astype(BF16)
    aa = z[:, LANES:2 * LANES].astype(BF16)
    gl = jax.nn.sigmoid(z[:, 2 * LANES:4 * LANES]).astype(BF16)
    gate_ref[...] = jnp.dot(gl, g2_ref[...].astype(BF16), preferred_element_type=F32)
    kk = k * kk_ref[...]
    kk = kk * lax.rsqrt(_head_sum(kk * kk) + 1e-12)
    ro_ref[...] = r
    kko_ref[...] = kk
    vo_ref[...] = v
    kd_sum = None
    for d in range(2):
        wl = w0_ref[d:d + 1, :] + jnp.dot(dec, w2_ref[d].astype(BF16), preferred_element_type=F32)
        lw_ref[d] = -jnp.exp(-jax.nn.softplus(-wl) - 0.5)
        a = jax.nn.sigmoid(a0_ref[d:d + 1, :] + jnp.dot(aa, a2_ref[d].astype(BF16), preferred_element_type=F32))
        b_ref[d] = kk * a
        kd = k * (1.0 + (a - 1.0) * ka_ref[...])
        kd_ref[d] = kd
        kd_sum = kd if kd_sum is None else kd_sum + kd
    bonus_ref[...] = _head_sum(r * kd_sum * rk_ref[...]) * v


def _rw_prep(proj, proj_z, lp, *, r_blk, p_tok, tp, ts, tm=256):
    n_tok = proj.shape[0]
    C = lp["rw_k_k"].shape[0]
    ZW = 4 * LANES
    full = lambda shape: pl.BlockSpec(shape, lambda i: (0,) * len(shape))
    tok = pl.BlockSpec((tm, C), lambda i: (i, 0))
    tok2 = pl.BlockSpec((2, tm, C), lambda i: (0, i, 0))
    one = jax.ShapeDtypeStruct((n_tok, C), F32)
    two = jax.ShapeDtypeStruct((2, n_tok, C), F32)
    return pl.pallas_call(
        functools.partial(_rw_prep_kernel, tm=tm, p_tok=p_tok, tp=tp, ts=ts, C=C),
        grid=(n_tok // tm,),
        in_specs=(_halo_specs(tm, C, r_blk, n_tok) + _halo_specs(tm, C, r_blk + 1, n_tok)
                  + _halo_specs(tm, C, r_blk + 2, n_tok) + _halo_specs(tm, ZW, 0, n_tok)
                  + [full((CONV_W, 3 * C)), full((CONV_W, ZW)), full((2, C)), full((2, LANES, C)),
                     full((2, C)), full((2, LANES, C)), full((2 * LANES, C)),
                     full((1, C)), full((1, C)), full((1, C))]),
        out_specs=[tok, tok, tok, tok2, tok2, tok2, tok, tok],
        out_shape=[one, one, one, two, two, two, one, one],
        compiler_params=_params("parallel"),
        name="rwkv_prep",
    )(*([proj] * 9 + [proj_z] * 3), lp["cw_rkv"], lp["cw_z"], lp["rw_w0"], lp["w2_pad"], lp["rw_a0"], lp["a2_pad"],
      lp["g2_pad"], lp["rw_k_k"].reshape(1, C), lp["rw_k_a"].reshape(1, C), lp["rw_r_k"].reshape(1, C))


def _rw_scan_kernel(*refs, has_s0, nc):
    if has_s0:
        s0_ref, refs = refs[0], refs[1:]
    (rf_ref, kkf_ref, vf_ref, rb_ref, kkb_ref, vb_ref, lwf_ref, bf_ref, kdf_ref, lwb_ref, bb_ref, kdb_ref,
     yf_ref, yb_ref, so_ref, st_ref) = refs
    C, N = SCAN_CHUNK, RW_HEAD
    W = 2 * N
    npair = rf_ref.shape[1] // W
    c = pl.program_id(1)
    zero = jnp.zeros((), BF16)

    row = lax.broadcasted_iota(jnp.int32, (W, W), 0)
    col = lax.broadcasted_iota(jnp.int32, (W, W), 1)
    same_head = (row // N) == (col // N)
    eye = (row == col).astype(F32)

    @pl.when(c == 0)
    def _():
        if has_s0:
            sel = (lax.broadcasted_iota(jnp.int32, (N, W), 0) == lax.broadcasted_iota(jnp.int32, (N, W), 1) % N)
            for d in range(2):
                for p in range(npair):
                    tiled = _mmh(s0_ref[d, 2 * p:2 * p + 2].reshape(W, N), sel.astype(F32))
                    st_ref[d * npair + p] = jnp.where(same_head, tiled, 0.0)
        else:
            st_ref[...] = jnp.zeros(st_ref.shape, F32)

    def order(shape, dim, bwd):
        t = lax.broadcasted_iota(jnp.int32, shape, dim) % C
        return C - 1 - t if bwd else t

    chains = []
    for d, (r_ref, kk_ref, v_ref, lw_ref, b_ref, kd_ref, y_ref) in enumerate(
            [(rf_ref, kkf_ref, vf_ref, lwf_ref, bf_ref, kdf_ref, yf_ref),
             (rb_ref, kkb_ref, vb_ref, lwb_ref, bb_ref, kdb_ref, yb_ref)]):
        rt, ct = order((W, W), 0, d == 1), order((W, W), 1, d == 1)
        masks = dict(strict=rt > ct, incl=rt >= ct)
        masks.update({s: ((rt // (2 * s)) == (ct // (2 * s))) & ((rt // s) % 2 == 1) & ((ct // s) % 2 == 0)
                      for s in (1, 2, 4, 8, 16, 32)})
        lw = lw_ref[...]
        g_in = _cumsum_rows(order((C, C), 0, d == 1) >= order((C, C), 1, d == 1), lw)
        g_tot = jnp.sum(lw, axis=0, keepdims=True)
        e_neg = jnp.exp(-g_in)
        e_rem = jnp.exp(g_tot - g_in)
        tok = dict(
            kk=(kk_ref[...] * jnp.exp(g_in - lw)).astype(BF16), r=(r_ref[...] * jnp.exp(g_in)).astype(BF16),
            b=(b_ref[...] * e_neg).astype(BF16), kd=(kd_ref[...] * e_neg).astype(BF16),
            b_end=(b_ref[...] * e_rem).astype(BF16), kd_end=(kd_ref[...] * e_rem).astype(BF16),
            v=v_ref[...].astype(BF16), e_tot=jnp.exp(g_tot))
        for p in range(npair):
            chains.append((d * npair + p, slice(p * W, (p + 1) * W), tok, masks, y_ref))

    def expand(x, sl):
        return jnp.where(same_head, jnp.concatenate([x[:, sl]] * 2, axis=0), zero)

    ak = [expand(t["kk"], sl) for _, sl, t, _, _ in chains]
    bk = [jnp.concatenate([expand(t["b"], sl), expand(t["kd"], sl)], axis=0) for _, sl, t, _, _ in chains]
    vb = [expand(t["v"], sl) for _, sl, t, _, _ in chains]
    S = [st_ref[i] for i, *_ in chains]
    Sb = [x.astype(BF16) for x in S]
    n = range(len(chains))
    mk = [ch[3] for ch in chains]
    lm = [_mm1(ak[i], bk[i], "nt") for i in n]
    L = [jnp.where(mk[i]["strict"], lm[i][:, :W], 0.0) for i in n]
    Lb = [x.astype(BF16) for x in L]
    M = [jnp.where(mk[i]["strict"], lm[i][:, W:], 0.0).astype(BF16) for i in n]
    rhs = [(_mm1(ak[i], Sb[i], "nt") + _mm1(M[i], vb[i])).astype(BF16) for i in n]
    X = [eye - jnp.where(mk[i][1], L[i], 0.0) for i in n]
    for s in (2, 4, 8, 16, 32):
        Xb = [x.astype(BF16) for x in X]
        t = [_mm1(jnp.where(mk[i][s], Lb[i], zero), Xb[i]).astype(BF16) for i in n]
        X = [X[i] - _mm1(Xb[i], t[i]) for i in n]
    Ub = [_mm1(X[i].astype(BF16), rhs[i]).astype(BF16) for i in n]
    for i, (slot, sl, tk, _, _) in enumerate(chains):
        st_ref[slot] = (S[i] * tk["e_tot"][:, sl] - _mm1(Ub[i], expand(tk["b_end"], sl), "tn")
                        + _mm1(vb[i], expand(tk["kd_end"], sl), "tn"))
    ar = [expand(t["r"], sl) for _, sl, t, _, _ in chains]
    nn = [_mm1(ar[i], bk[i], "nt") for i in n]
    nb = [jnp.where(mk[i]["incl"], nn[i][:, :W], 0.0).astype(BF16) for i in n]
    nk = [jnp.where(mk[i]["incl"], nn[i][:, W:], 0.0).astype(BF16) for i in n]
    for i, (_, sl, _, _, y_ref) in enumerate(chains):
        y = _mm1(ar[i], Sb[i], "nt") - _mm1(nb[i], Ub[i]) + _mm1(nk[i], vb[i])
        y_ref[:, sl] = y[:C] + y[C:]

    @pl.when(c == nc - 1)
    def _():
        fold = (lax.broadcasted_iota(jnp.int32, (W, N), 0) % N == lax.broadcasted_iota(jnp.int32, (W, N), 1))
        for d in range(2):
            for p in range(npair):
                so_ref[d, 2 * p:2 * p + 2] = _mmh(st_ref[d * npair + p], fold.astype(F32)).reshape(2, N, N)


def _rw_scan(r, kk, v, lw, b, kd, *, row0, B, T, s0=None, layer=0):
    C = r.shape[1]
    H = C // RW_HEAD
    CH = SCAN_CHUNK
    nc = T // CH
    blk0 = row0 // CH
    fwd = pl.BlockSpec((CH, C), lambda bi, c: (blk0 + bi * nc + c, 0))
    bwd = pl.BlockSpec((CH, C), lambda bi, c: (blk0 + bi * nc + nc - 1 - c, 0))
    fwd2 = pl.BlockSpec((None, CH, C), lambda bi, c: (0, blk0 + bi * nc + c, 0))
    bwd2 = pl.BlockSpec((None, CH, C), lambda bi, c: (1, blk0 + bi * nc + nc - 1 - c, 0))
    in_specs = [fwd, fwd, fwd, bwd, bwd, bwd, fwd2, fwd2, fwd2, bwd2, bwd2, bwd2]
    args = [r, kk, v, r, kk, v, lw, b, kd, lw, b, kd]
    if s0 is not None:
        in_specs = [pl.BlockSpec((None, None, 2, H, RW_HEAD, RW_HEAD),
                                 lambda bi, c: (bi, layer, 0, 0, 0, 0))] + in_specs
        args = [s0] + args
    y_shape = jax.ShapeDtypeStruct((B * T, C), F32)
    return pl.pallas_call(
        functools.partial(_rw_scan_kernel, has_s0=s0 is not None, nc=nc),
        grid=(B, nc),
        in_specs=in_specs,
        out_specs=[pl.BlockSpec((CH, C), lambda bi, c: (bi * nc + c, 0)),
                   pl.BlockSpec((CH, C), lambda bi, c: (bi * nc + nc - 1 - c, 0)),
                   pl.BlockSpec((None, 2, H, RW_HEAD, RW_HEAD), lambda bi, c: (bi, 0, 0, 0, 0))],
        out_shape=[y_shape, y_shape, jax.ShapeDtypeStruct((B, 2, H, RW_HEAD, RW_HEAD), F32)],
        scratch_shapes=[pltpu.VMEM((H, 2 * RW_HEAD, 2 * RW_HEAD), F32)],
        compiler_params=_params("parallel", "arbitrary"),
        name="rwkv_scan_ctx" if s0 is not None else "rwkv_scan",
    )(*args)


def _rw_post_kernel(yf_ref, yb_ref, bonus_ref, gate_ref, g_ref, b_ref, _mix_ref, o_ref):
    y = yf_ref[...] + yb_ref[...]
    inv_n = 1.0 / RW_HEAD
    mu = _head_sum(y) * inv_n
    yc = y - mu
    var = _head_sum(yc * yc) * inv_n
    yn = yc * lax.rsqrt(var + RW_GN_EPS) * g_ref[...] + b_ref[...]
    o_ref[...] = ((yn + bonus_ref[...]) * gate_ref[...]).astype(o_ref.dtype)


def _rw_post(y_f, y_b, bonus, gate, mix, gn_g, gn_b, *, row0, out_blk, tm=256):
    n_rows, C = y_f.shape
    r0 = row0 // tm
    own = pl.BlockSpec((tm, C), lambda i: (i, 0))
    tok = pl.BlockSpec((tm, C), lambda i: (r0 + i, 0))
    vec = pl.BlockSpec((1, C), lambda i: (0, 0))
    return pl.pallas_call(
        _rw_post_kernel,
        grid=(n_rows // tm,),
        in_specs=[own, own, tok, tok, vec, vec, _any_spec()],
        out_specs=pl.BlockSpec((tm, C), lambda i: (r0 + i, out_blk)),
        out_shape=jax.ShapeDtypeStruct(mix.shape, mix.dtype),
        input_output_aliases={6: 0},
        compiler_params=_params("parallel"),
        name="rwkv_post",
    )(y_f, y_b, bonus, gate, gn_g.reshape(1, C), gn_b.reshape(1, C), mix)


def _pad_rows(w, rows, at):
    return jnp.zeros((rows, w.shape[1]), w.dtype).at[at:at + w.shape[0]].set(w)


def _layer_weights(l, D, w_in, rw_conv_w, rw_w2, rw_a2, rw_g2, w_out, ffn_up, ffn_down):
    DA = D // 2
    C = D // 4
    o_rw = 3 * DA
    o_z = o_rw + 3 * C
    n_z = 4 * RW_LORA_R + RW_GATE_R
    o_uv = o_z + n_z
    zpad = 4 * LANES - n_z
    cw = rw_conv_w[l]
    w = w_in[l]
    return dict(
        w_qkv_rkv=w[:, :o_z].astype(BF16),
        w_uv=w[:, o_uv:].astype(BF16),
        w_z=jnp.concatenate([w[:, o_z:o_uv].astype(BF16), jnp.zeros((w.shape[0], zpad), BF16)], axis=1),
        cw_rkv=cw[:, :3 * C],
        cw_z=jnp.concatenate([cw[:, 3 * C:], jnp.zeros((CONV_W, zpad), F32)], axis=1),
        w2_pad=jnp.stack([_pad_rows(rw_w2[l, d], LANES, d * RW_LORA_R) for d in range(2)]),
        a2_pad=jnp.stack([_pad_rows(rw_a2[l, d], LANES, d * RW_LORA_R) for d in range(2)]),
        g2_pad=_pad_rows(rw_g2[l], 2 * LANES, 0),
        w_out=w_out[l].astype(BF16),
        ffn_up=ffn_up[l].astype(BF16),
        ffn_down=ffn_down[l].astype(BF16),
    )


def _rope_tables(T):
    n = DA_D // 4
    inv = ROPE_THETA ** (-jnp.arange(n, dtype=F32) / n)
    rows = T // GRID_W
    row = jnp.repeat(jnp.arange(rows), GRID_W).astype(F32)
    col = jnp.tile(jnp.arange(GRID_W), rows).astype(F32)
    sign = jnp.concatenate([-jnp.ones((n,), F32), jnp.ones((n,), F32)])
    cs, sn = [], []
    for pos in (row, col):
        ang = pos[:, None] * inv[None, :]
        cs.append(jnp.concatenate([jnp.cos(ang), jnp.cos(ang)], axis=1))
        sn.append(jnp.concatenate([jnp.sin(ang), jnp.sin(ang)], axis=1) * sign[None, :])
    cos = jnp.concatenate(cs, axis=1)
    sin = jnp.concatenate(sn, axis=1)
    return jnp.tile(cos, (1, 2)), jnp.tile(sin, (1, 2))


def kernel(x_prompt, x_sample, cache_da_k, cache_da_v, state_rwkv, c, c_ctx, mod_w, mod_b, norm1_g, norm2_g, w_in, da_lambda, da_subln_g, rw_conv_w, rw_w0, rw_w2, rw_a0, rw_a2, rw_g2, rw_k_k, rw_k_a, rw_r_k, rw_gn_g, rw_gn_b, cm_norm_g, cm_ws, cm_bs, w_out, ffn_up, ffn_conv_w, ffn_conv_b, ffn_down, final_norm_g):
    Bp, Tp, D = x_prompt.shape
    Bs, Ts, _ = x_sample.shape
    L = mod_w.shape[0]
    past = cache_da_k.shape[2]
    DA = D // 2
    H = DA // (2 * DA_D)
    C = D // 4
    F = ffn_down.shape[1]
    p_tok, s_tok = Bp * Tp, Bs * Ts
    assert Bs + 1 <= SUBLANES and p_tok % Ts == 0

    x = jnp.concatenate([x_prompt.reshape(p_tok, D), x_sample.reshape(s_tok, D)], axis=0)
    cond8 = jnp.concatenate([c_ctx[None, :], c, jnp.zeros((SUBLANES - 1 - Bs, D), F32)], axis=0)
    mod = _modulation(cond8, mod_w, mod_b)
    ck4 = cache_da_k.reshape(Bs, L, past, DA)
    cv4 = cache_da_v.reshape(Bs, L, past, DA)
    cos, sin = _rope_tables(Ts)
    ko = jnp.zeros((Bp, L, Tp, DA), F32)
    vo = jnp.zeros((Bp, L, Tp, DA), F32)

    new_s = []
    for l in range(L):
        lw_ = _layer_weights(l, D, w_in, rw_conv_w, rw_w2, rw_a2, rw_g2, w_out, ffn_up, ffn_down)
        lp = dict(lw_, rw_w0=rw_w0[l], rw_a0=rw_a0[l], rw_k_k=rw_k_k[l], rw_k_a=rw_k_a[l], rw_r_k=rw_r_k[l])
        lam_init = 0.8 - 0.6 * math.exp(-0.3 * l)

        h = _norm_mod(x, norm1_g[l], mod, l, 0, 1, p_tok, Ts)
        proj = _matmul(h, lp["w_qkv_rkv"], tm=1024, tn=512, name="proj_in")
        proj_u = _matmul(h, lp["w_uv"], tm=1024, tn=512, name="proj_in_uv")
        proj_z = _matmul(h, lp["w_z"], tm=1024, tn=512, name="proj_in_lora")
        ko, vo = _store_kv(proj, ko, vo, l, Bp=Bp, Tp=Tp, DA=DA)

        mix = jnp.zeros((p_tok + s_tok, D), BF16)
        mix = _attention(proj, mix, da_lambda[l], da_subln_g[l], lam_init, row0=0, B=Bp, T=Tp, H=H)
        mix = _attention(proj, mix, da_lambda[l], da_subln_g[l], lam_init, row0=p_tok, B=Bs, T=Ts, H=H,
                         ctx=(ck4, cv4, l, cos, sin))
        r_, kk_, v_, lg_, b_, kd_, gate_, bonus_ = _rw_prep(proj, proj_z, lp, r_blk=3 * DA // C,
                                                            p_tok=p_tok, tp=Tp, ts=Ts)
        yf_p, yb_p, s_p = _rw_scan(r_, kk_, v_, lg_, b_, kd_, row0=0, B=Bp, T=Tp)
        yf_s, yb_s, _ = _rw_scan(r_, kk_, v_, lg_, b_, kd_, row0=p_tok, B=Bs, T=Ts, s0=state_rwkv, layer=l)
        mix = _rw_post(yf_p, yb_p, bonus_, gate_, mix, rw_gn_g[l], rw_gn_b[l], row0=0, out_blk=DA // C)
        mix = _rw_post(yf_s, yb_s, bonus_, gate_, mix, rw_gn_g[l], rw_gn_b[l], row0=p_tok, out_blk=DA // C)
        mix = _chunk_mlp(proj_u, mix, cm_norm_g[l], cm_ws[l], cm_bs[l], u_blk=0, out_blk=DA // C + 1)

        x = _matmul(mix, lp["w_out"], tm=1024, tn=512, resid=(x, mod, l, 2 * D // 512, p_tok, Ts),
                    name="proj_out")

        h = _norm_mod(x, norm2_g[l], mod, l, 3, 4, p_tok, Ts)
        up = _matmul(h, lp["ffn_up"], tm=1024, tn=512, name="ffn_up")
        act = _ffn_act(up, ffn_conv_w[l], ffn_conv_b[l], p_tok=p_tok, tp=Tp, ts=Ts)
        x = _matmul(act, lp["ffn_down"], tm=1024, tn=512, tk=F // 2,
                    resid=(x, mod, l, 5 * D // 512, p_tok, Ts), name="ffn_down")

        new_s.append(s_p)

    y_p = _final_norm(x, final_norm_g, row0=0, n_rows=p_tok)
    y_s = _final_norm(x, final_norm_g, row0=p_tok, n_rows=s_tok)
    return (y_p.reshape(Bp, Tp, D), y_s.reshape(Bs, Ts, D), ko.reshape(Bp, L, Tp, H, 2, DA_D),
            vo.reshape(Bp, L, Tp, H, 2 * DA_D), jnp.stack(new_s, axis=1))
```

```python
import functools
import math

import jax
import jax.numpy as jnp
from jax import lax
from jax.experimental import pallas as pl
from jax.experimental.pallas import tpu as pltpu

F32 = jnp.float32
BF16 = jnp.bfloat16
HIGHEST = lax.Precision.HIGHEST

LANES = 128
SUBLANES = 8
VMEM_BYTES_V7X = 64 * 1024 * 1024
VMEM_BUDGET = VMEM_BYTES_V7X * 3 // 4

GRID_W = 64
DA_D = 128
RW_HEAD = 64
RW_LORA_R = 64
RW_GATE_R = 160
RW_GN_EPS = 64e-5
CM_GROUPS = 4
CM_CHUNK = 128
CONV_W = 3
ROPE_THETA = 10000.0
NORM_EPS = 1e-6
SCAN_CHUNK = 64


def _params(*sem):
    return pltpu.CompilerParams(dimension_semantics=sem, vmem_limit_bytes=VMEM_BUDGET)


def _any_spec():
    return pl.BlockSpec(memory_space=pl.ANY)


def _cond_row(tok0, p_tok, ts):
    return jnp.where(tok0 < p_tok, 0, 1 + jnp.maximum(tok0 - p_tok, 0) // ts)


def _mod_kernel(c_ref, w_ref, b_ref, o_ref):
    c = c_ref[...]
    s = (c * jax.nn.sigmoid(c)).astype(BF16)
    o_ref[...] = jnp.dot(s, w_ref[...].astype(BF16), preferred_element_type=F32) + b_ref[...]


def _modulation(cond8, mod_w, mod_b, tn=512):
    L, D, N = mod_w.shape
    return pl.pallas_call(
        _mod_kernel,
        grid=(L, N // tn),
        in_specs=[pl.BlockSpec((SUBLANES, D), lambda l, j: (0, 0)),
                  pl.BlockSpec((None, D, tn), lambda l, j: (l, 0, j)),
                  pl.BlockSpec((None, 1, tn), lambda l, j: (l, 0, j))],
        out_specs=pl.BlockSpec((None, SUBLANES, tn), lambda l, j: (l, 0, j)),
        out_shape=jax.ShapeDtypeStruct((L, SUBLANES, N), F32),
        compiler_params=_params("parallel", "parallel"),
        name="modulation",
    )(cond8, mod_w, mod_b.reshape(L, 1, N))


def _rows(refs, first_tiles):
    if len(refs) == 1:
        return refs[0][...]
    return jnp.where(pl.program_id(0) < first_tiles, refs[0][...], refs[1][...])


def _split_specs(xs, tm, block, idx):
    if len(xs) == 1:
        return [pl.BlockSpec(block, idx(lambda i: i))], 0
    first = xs[0].shape[0] // tm
    last = xs[1].shape[0] // tm - 1
    return [pl.BlockSpec(block, idx(lambda i: jnp.minimum(i, first - 1))),
            pl.BlockSpec(block, idx(lambda i: jnp.clip(i - first, 0, last)))], first


def _norm_mod_kernel(*refs, n_x, x_split, tm, p_tok, ts):
    x_refs, (g_ref, sh_ref, sc_ref, o_ref) = refs[:n_x], refs[n_x:]
    row = _cond_row(pl.program_id(0) * tm, p_tok, ts)
    x = _rows(x_refs, x_split)
    y = x * lax.rsqrt(jnp.mean(x * x, axis=-1, keepdims=True) + NORM_EPS)
    sc = sc_ref[pl.ds(row, 1), :]
    sh = sh_ref[pl.ds(row, 1), :]
    o_ref[...] = ((y * g_ref[...]) * (1.0 + sc) + sh).astype(o_ref.dtype)


def _norm_mod(xs, g, mod, layer, k_sh, k_sc, p_tok, ts, tm=256):
    n = sum(x.shape[0] for x in xs)
    D = xs[0].shape[1]
    assert p_tok % tm == 0 and ts % tm == 0, "a row tile must not mix modulation groups"
    x_specs, x_split = _split_specs(xs, tm, (tm, D), lambda r: (lambda i: (r(i), 0)))
    return pl.pallas_call(
        functools.partial(_norm_mod_kernel, n_x=len(xs), x_split=x_split, tm=tm, p_tok=p_tok, ts=ts),
        grid=(n // tm,),
        in_specs=x_specs + [pl.BlockSpec((1, D), lambda i: (0, 0)),
                            pl.BlockSpec((None, SUBLANES, D), lambda i: (layer, 0, k_sh)),
                            pl.BlockSpec((None, SUBLANES, D), lambda i: (layer, 0, k_sc))],
        out_specs=pl.BlockSpec((tm, D), lambda i: (i, 0)),
        out_shape=jax.ShapeDtypeStruct((n, D), BF16),
        compiler_params=_params("parallel"),
        name="norm_mod",
    )(*xs, g.reshape(1, D), mod, mod)


def _final_norm_kernel(x_ref, g_ref, o_ref):
    x = x_ref[...]
    o_ref[...] = x * lax.rsqrt(jnp.mean(x * x, axis=-1, keepdims=True) + NORM_EPS) * g_ref[...]


def _final_norm(x, g, *, row0, n_rows, tm=256):
    D = x.shape[1]
    r0 = row0 // tm
    return pl.pallas_call(
        _final_norm_kernel,
        grid=(n_rows // tm,),
        in_specs=[pl.BlockSpec((tm, D), lambda i: (r0 + i, 0)), pl.BlockSpec((1, D), lambda i: (0, 0))],
        out_specs=pl.BlockSpec((tm, D), lambda i: (i, 0)),
        out_shape=jax.ShapeDtypeStruct((n_rows, D), F32),
        compiler_params=_params("parallel"),
        name="final_norm",
    )(x, g.reshape(1, D))


def _mm_kernel(*refs, nk, n_x, x_split, tm, p_tok, ts):
    a_ref, b_ref = refs[:2]
    x_refs, g_ref = refs[2:2 + n_x], (refs[2 + n_x] if n_x else None)
    o_ref = refs[3 + n_x] if n_x else refs[2]
    acc_ref = refs[-1] if nk > 1 else None

    def finish(acc):
        if n_x:
            row = _cond_row(pl.program_id(0) * tm, p_tok, ts)
            o_ref[...] = _rows(x_refs, x_split) + g_ref[pl.ds(row, 1), :] * acc
        else:
            o_ref[...] = acc

    part = jnp.dot(a_ref[...], b_ref[...], preferred_element_type=F32)
    if nk == 1:
        finish(part)
    else:
        k = pl.program_id(2)

        @pl.when(k == 0)
        def _():
            acc_ref[...] = part

        @pl.when((k > 0) & (k < nk - 1))
        def _():
            acc_ref[...] += part

        @pl.when(k == nk - 1)
        def _():
            finish(acc_ref[...] + part)


def _matmul(a, b, *, tm, tn, tk=None, layer=None, n_cols=None, resid=None, name):
    M, K = a.shape
    N = b.shape[-1] if n_cols is None else n_cols
    tk = K if tk is None else tk
    nk = K // tk
    if layer is None:
        b_spec = pl.BlockSpec((tk, tn), lambda i, j, k: (k, j))
    else:
        b_spec = pl.BlockSpec((None, tk, tn), lambda i, j, k: (layer, k, j))
    in_specs = [pl.BlockSpec((tm, tk), lambda i, j, k: (i, k)), b_spec]
    args = [a, b]
    p_tok = ts = x_split = n_x = 0
    if resid is not None:
        xs, mod, mod_layer, gate_blk, p_tok, ts = resid
        assert p_tok % tm == 0 and ts % tm == 0, "a row tile must not mix modulation groups"
        x_specs, x_split = _split_specs(xs, tm, (tm, tn), lambda r: (lambda i, j, k: (r(i), j)))
        n_x = len(xs)
        in_specs += x_specs + [pl.BlockSpec((None, SUBLANES, tn), lambda i, j, k: (mod_layer, 0, gate_blk + j))]
        args += list(xs) + [mod]
    return pl.pallas_call(
        functools.partial(_mm_kernel, nk=nk, n_x=n_x, x_split=x_split, tm=tm, p_tok=p_tok, ts=ts),
        grid=(M // tm, N // tn, nk),
        in_specs=in_specs,
        out_specs=pl.BlockSpec((tm, tn), lambda i, j, k: (i, j)),
        out_shape=jax.ShapeDtypeStruct((M, N), F32),
        scratch_shapes=[pltpu.VMEM((tm, tn), F32)] if nk > 1 else [],
        compiler_params=_params("parallel", "parallel", "arbitrary"),
        name=name,
    )(*args)


def _store_kv_kernel(k_ref, v_ref, _ko_in, _vo_in, ko_ref, vo_ref):
    ko_ref[...] = k_ref[...]
    vo_ref[...] = v_ref[...]


def _store_kv(proj, ko, vo, layer, *, Bp, Tp, DA):
    out = pl.BlockSpec((None, None, Tp, DA), lambda b: (b, layer, 0, 0))
    return pl.pallas_call(
        _store_kv_kernel,
        grid=(Bp,),
        in_specs=[pl.BlockSpec((Tp, DA), lambda b: (b, 1)), pl.BlockSpec((Tp, DA), lambda b: (b, 2)),
                  _any_spec(), _any_spec()],
        out_specs=[out, out],
        out_shape=[jax.ShapeDtypeStruct(ko.shape, ko.dtype), jax.ShapeDtypeStruct(vo.shape, vo.dtype)],
        input_output_aliases={2: 0, 3: 1},
        compiler_params=_params("parallel"),
        name="store_kv",
    )(proj, proj, ko, vo)


def _rope(x, cos, sin_signed):
    lane = lax.broadcasted_iota(jnp.int32, x.shape, 1)
    width = x.shape[1]
    rot = jnp.where((lane % 64) < 32, pltpu.roll(x, width - 32, 1), pltpu.roll(x, 32, 1))
    return x * cos + rot * sin_signed


def _attn_kernel(*refs, rope, lam_init, key_block):
    if rope:
        (lam_ref, q_ref, k_ref, v_ref, kc_ref, vc_ref, cq_ref, sq_ref, ck_ref, sk_ref, g_ref, _mix_ref,
         o_ref, kr_ref, vr_ref) = refs
    else:
        lam_ref, q_ref, k_ref, v_ref, g_ref, _mix_ref, o_ref = refs
    lm = lam_ref[...]
    s1 = jnp.sum(lm[0:1] * lm[1:2], axis=-1, keepdims=True)
    s2 = jnp.sum(lm[2:3] * lm[3:4], axis=-1, keepdims=True)
    lam = jnp.exp(s1) - jnp.exp(s2) + lam_init
    qscale = DA_D ** -0.5 * math.log2(math.e)

    if rope:
        @pl.when(pl.program_id(2) == 0)
        def _():
            kr_ref[...] = _rope(k_ref[...], ck_ref[...], sk_ref[...]).astype(BF16)
            vr_ref[...] = v_ref[...].astype(BF16)

        q = (_rope(q_ref[...], cq_ref[...], sq_ref[...]) * qscale).astype(BF16)
        T = kr_ref.shape[0]
        blocks = [(kr_ref[j * key_block:(j + 1) * key_block], vr_ref[j * key_block:(j + 1) * key_block])
                  for j in range(T // key_block)]
        blocks.append((kc_ref[...].astype(BF16), vc_ref[...].astype(BF16)))
    else:
        q = (q_ref[...] * qscale).astype(BF16)
        blocks = [(k_ref[...].astype(BF16), v_ref[...].astype(BF16))]

    maps = []
    for c in range(2):
        cols = slice(c * DA_D, (c + 1) * DA_D)
        ss = [_mm1(q[:, cols], kb[:, cols], "nt") for kb, _ in blocks]
        m = functools.reduce(jnp.maximum, [jnp.max(s, axis=-1, keepdims=True) for s in ss])
        o = d = None
        for s, (_, vb) in zip(ss, blocks):
            e = jnp.exp2(s - m)
            t = jnp.sum(e, axis=-1, keepdims=True)
            u = _mm1(e.astype(BF16), vb)
            d = t if d is None else d + t
            o = u if o is None else o + u
        maps.append((o, d))
    o = maps[0][0] * (1.0 / maps[0][1]) - maps[1][0] * (lam / maps[1][1])
    y = o * lax.rsqrt(jnp.mean(o * o, axis=-1, keepdims=True) + NORM_EPS)
    o_ref[...] = (y * g_ref[...] * (1.0 - lam_init)).astype(o_ref.dtype)


def _attention(proj, mix, lam_p, g, lam_init, *, row0, B, T, H, ctx=None, tq=256, key_block=512):
    W = 2 * DA_D
    nq = T // tq
    qb0 = row0 // tq
    kb0 = row0 // T
    rope = ctx is not None
    in_specs = [pl.BlockSpec((4, DA_D), lambda b, h, i: (0, 0)),
                pl.BlockSpec((tq, W), lambda b, h, i: (qb0 + b * nq + i, h)),
                pl.BlockSpec((T, W), lambda b, h, i: (kb0 + b, H + h)),
                pl.BlockSpec((T, W), lambda b, h, i: (kb0 + b, 2 * H + h))]
    args = [lam_p, proj, proj, proj]
    scratch = []
    if rope:
        ck, cv, layer, cos, sin = ctx
        past = ck.shape[2]
        in_specs += [pl.BlockSpec((None, None, past, W), lambda b, h, i: (b, layer, 0, h)),
                     pl.BlockSpec((None, None, past, W), lambda b, h, i: (b, layer, 0, h)),
                     pl.BlockSpec((tq, W), lambda b, h, i: (i, 0)),
                     pl.BlockSpec((tq, W), lambda b, h, i: (i, 0)),
                     pl.BlockSpec((T, W), lambda b, h, i: (0, 0)),
                     pl.BlockSpec((T, W), lambda b, h, i: (0, 0))]
        args += [ck, cv, cos, sin, cos, sin]
        scratch = [pltpu.VMEM((T, W), BF16), pltpu.VMEM((T, W), BF16)]
    in_specs += [pl.BlockSpec((1, W), lambda b, h, i: (0, 0)), _any_spec()]
    args += [g.reshape(1, W), mix]
    return pl.pallas_call(
        functools.partial(_attn_kernel, rope=rope, lam_init=lam_init, key_block=key_block),
        grid=(B, H, nq),
        in_specs=in_specs,
        out_specs=pl.BlockSpec((tq, W), lambda b, h, i: (qb0 + b * nq + i, h)),
        out_shape=jax.ShapeDtypeStruct(mix.shape, mix.dtype),
        input_output_aliases={len(args) - 1: 0},
        scratch_shapes=scratch,
        compiler_params=_params("parallel", "parallel", "arbitrary"),
        name="diff_attn_ctx" if rope else "diff_attn",
    )(*args)


def _tile_edges(tok0, tm, p_tok, tp, ts):
    in_p = tok0 < p_tok
    pos = jnp.where(in_p, tok0 % tp, jnp.maximum(tok0 - p_tok, 0) % ts)
    length = jnp.where(in_p, tp, ts)
    return (pos != 0).astype(F32), (pos + tm != length).astype(F32)


def _conv3(x_ref, p_ref, n_ref, s_ref, w, keep_prev, keep_next, cols=slice(None)):
    tm = x_ref.shape[0]
    s_ref[SUBLANES:SUBLANES + tm, :] = x_ref[:, cols]
    s_ref[SUBLANES - 1:SUBLANES, :] = p_ref[SUBLANES - 1:SUBLANES, cols] * keep_prev
    s_ref[SUBLANES + tm:SUBLANES + tm + 1, :] = n_ref[0:1, cols] * keep_next
    return (s_ref[SUBLANES - 1:SUBLANES - 1 + tm, :] * w[0:1] + x_ref[:, cols] * w[1:2]
            + s_ref[SUBLANES + 1:SUBLANES + 1 + tm, :] * w[2:3])


def _halo_specs(tm, width, col_blk, n_tok, nidx=1):
    r = tm // SUBLANES
    last = n_tok // SUBLANES - 1
    if nidx == 1:
        return [pl.BlockSpec((tm, width), lambda i: (i, col_blk)),
                pl.BlockSpec((SUBLANES, width), lambda i: (jnp.maximum(i * r - 1, 0), col_blk)),
                pl.BlockSpec((SUBLANES, width), lambda i: (jnp.minimum((i + 1) * r, last), col_blk))]
    return [pl.BlockSpec((tm, width), lambda i, j: (i, col_blk + j)),
            pl.BlockSpec((SUBLANES, width), lambda i, j: (jnp.maximum(i * r - 1, 0), col_blk + j)),
            pl.BlockSpec((SUBLANES, width), lambda i, j: (jnp.minimum((i + 1) * r, last), col_blk + j))]


def _cmlp_kernel(u_ref, v_ref, gain_ref, ws_ref, bs_ref, _mix_ref, o_ref, *, tm):
    v = v_ref[...]
    z = (v * lax.rsqrt(jnp.mean(v * v, axis=-1, keepdims=True) + NORM_EPS) * gain_ref[...]).astype(BF16)
    gw = z.shape[1] // CM_GROUPS
    for n in range(tm // CM_CHUNK):
        rows = slice(n * CM_CHUNK, (n + 1) * CM_CHUNK)
        for g in range(CM_GROUPS):
            cols = slice(g * gw, (g + 1) * gw)
            t = jnp.dot(ws_ref[g].astype(BF16), z[rows, cols], preferred_element_type=F32) + bs_ref[g]
            o_ref[rows, cols] = (u_ref[rows, cols] * t).astype(o_ref.dtype)


def _chunk_mlp(proj, mix, gain, ws, bs, *, u_blk, out_blk, tm=256):
    n_tok = proj.shape[0]
    W = gain.shape[0]
    gw = W // CM_GROUPS
    bs_b = jnp.broadcast_to(bs[:, :, None], (CM_GROUPS, CM_CHUNK, gw))
    return pl.pallas_call(
        functools.partial(_cmlp_kernel, tm=tm),
        grid=(n_tok // tm,),
        in_specs=[pl.BlockSpec((tm, W), lambda i: (i, u_blk)),
                  pl.BlockSpec((tm, W), lambda i: (i, u_blk + 1)),
                  pl.BlockSpec((1, W), lambda i: (0, 0)),
                  pl.BlockSpec((CM_GROUPS, CM_CHUNK, CM_CHUNK), lambda i: (0, 0, 0)),
                  pl.BlockSpec((CM_GROUPS, CM_CHUNK, gw), lambda i: (0, 0, 0)),
                  _any_spec()],
        out_specs=pl.BlockSpec((tm, W), lambda i: (i, out_blk)),
        out_shape=jax.ShapeDtypeStruct(mix.shape, mix.dtype),
        input_output_aliases={5: 0},
        compiler_params=_params("parallel"),
        name="chunk_mlp",
    )(proj, proj, gain.reshape(1, W), ws, bs_b, mix)


def _ffn_act_kernel(a_ref, ap_ref, an_ref, b_ref, bp_ref, bn_ref, wa_ref, wb_ref, ba_ref, bb_ref, o_ref,
                    sa_ref, sb_ref, *, tm, tc, p_tok, tp, ts):
    keep_prev, keep_next = _tile_edges(pl.program_id(0) * tm, tm, p_tok, tp, ts)

    def body(j, carry):
        cols = pl.ds(pl.multiple_of(j * LANES, LANES), LANES)
        ga = _conv3(a_ref, ap_ref, an_ref, sa_ref, wa_ref[:, cols], keep_prev, keep_next, cols) + ba_ref[:, cols]
        gb = _conv3(b_ref, bp_ref, bn_ref, sb_ref, wb_ref[:, cols], keep_prev, keep_next, cols) + bb_ref[:, cols]
        o_ref[:, cols] = (ga * jax.nn.sigmoid(ga) * gb).astype(o_ref.dtype)
        return carry

    lax.fori_loop(0, tc // LANES, body, 0)


def _ffn_act(up, conv_w, conv_b, *, p_tok, tp, ts, tm=256, col_blocks=2):
    n_tok, two_f = up.shape
    F = two_f // 2
    tc = F // col_blocks
    assert tp % tm == 0 and ts % tm == 0 and tc % LANES == 0
    cb = conv_b.reshape(1, two_f)
    wspec = lambda off: pl.BlockSpec((CONV_W, tc), lambda i, j: (0, off + j))
    bspec = lambda off: pl.BlockSpec((1, tc), lambda i, j: (0, off + j))
    halo = pltpu.VMEM((tm + 2 * SUBLANES, LANES), F32)
    return pl.pallas_call(
        functools.partial(_ffn_act_kernel, tm=tm, tc=tc, p_tok=p_tok, tp=tp, ts=ts),
        grid=(n_tok // tm, col_blocks),
        in_specs=(_halo_specs(tm, tc, 0, n_tok, 2) + _halo_specs(tm, tc, col_blocks, n_tok, 2)
                  + [wspec(0), wspec(col_blocks), bspec(0), bspec(col_blocks)]),
        out_specs=pl.BlockSpec((tm, tc), lambda i, j: (i, j)),
        out_shape=jax.ShapeDtypeStruct((n_tok, F), BF16),
        scratch_shapes=[halo, halo],
        compiler_params=_params("parallel", "parallel"),
        name="ffn_act",
    )(up, up, up, up, up, up, conv_w, conv_w, cb, cb)


_DIMS = {"nn": ((1,), (0,)), "nt": ((1,), (1,)), "tn": ((0,), (0,))}


def _mm1(a, b, kind="nn"):
    return lax.dot_general(a, b, (_DIMS[kind], ((), ())), preferred_element_type=F32)


def _mmh(a, b, dims=(((1,), (0,)), ((), ()))):
    return lax.dot_general(a, b, dims, precision=HIGHEST, preferred_element_type=F32)


def _split3(x):
    hi = x.astype(BF16)
    r1 = x - hi.astype(F32)
    mid = r1.astype(BF16)
    return hi, mid, (r1 - mid.astype(F32)).astype(BF16)


def _head_sum(x):
    r = lax.broadcasted_iota(jnp.int32, (3 * LANES, LANES), 0) % LANES // RW_HEAD
    c = lax.broadcasted_iota(jnp.int32, (3 * LANES, LANES), 1) // RW_HEAD
    e3 = jnp.where(r == c, 1.0, 0.0).astype(BF16)
    parts = _split3(x)
    cols = [_mm1(jnp.concatenate([p[:, j * LANES:(j + 1) * LANES] for p in parts], axis=1), e3)
            for j in range(x.shape[1] // LANES)]
    return jnp.concatenate(cols, axis=1)


def _cumsum_rows(tri, x):
    t = tri.astype(BF16)
    return _mm1(jnp.concatenate([t, t, t], axis=1), jnp.concatenate(_split3(x), axis=0))


def _rw_prep_kernel(r_ref, rp_ref, rn_ref, k_ref, kp_ref, kn_ref, v_ref, vp_ref, vn_ref,
                    z_ref, zp_ref, zn_ref, cw_ref, cz_ref, w0_ref, w2_ref, a0_ref, a2_ref, g2_ref,
                    kk_ref, ka_ref, rk_ref,
                    ro_ref, kko_ref, vo_ref, lw_ref, b_ref, kd_ref, gate_ref, bonus_ref,
                    sr_ref, sk_ref, sv_ref, sz_ref, *, tm, p_tok, tp, ts, C):
    edges = _tile_edges(pl.program_id(0) * tm, tm, p_tok, tp, ts)

    def conv(x_ref, p_ref, n_ref, s_ref, w):
        chunks = []
        for j in range(x_ref.shape[1] // LANES):
            cols = slice(j * LANES, (j + 1) * LANES)
            chunks.append(_conv3(x_ref, p_ref, n_ref, s_ref.at[j], w[:, cols], *edges, cols))
        return jnp.concatenate(chunks, axis=1)

    cw = cw_ref[...]
    r = conv(r_ref, rp_ref, rn_ref, sr_ref, cw[:, 0:C])
    k = conv(k_ref, kp_ref, kn_ref, sk_ref, cw[:, C:2 * C])
    v = conv(v_ref, vp_ref, vn_ref, sv_ref, cw[:, 2 * C:3 * C])
    z = conv(z_ref, zp_ref, zn_ref, sz_ref, cz_ref[...])
    dec = jnp.tanh(z[:, 0:LANES]).astype(BF16)
    aa = z[:, LANES:2 * LANES].astype(BF16)
    gl = jax.nn.sigmoid(z[:, 2 * LANES:4 * LANES]).astype(BF16)
    gate_ref[...] = jnp.dot(gl, g2_ref[...].astype(BF16), preferred_element_type=F32)
    kk = k * kk_ref[...]
    kk = kk * lax.rsqrt(_head_sum(kk * kk) + 1e-12)
    ro_ref[...] = r
    kko_ref[...] = kk
    vo_ref[...] = v
    kd_sum = None
    for d in range(2):
        wl = w0_ref[d:d + 1, :] + jnp.dot(dec, w2_ref[d].astype(BF16), preferred_element_type=F32)
        lw_ref[d] = -math.exp(-0.5) * jax.nn.sigmoid(wl)
        a = jax.nn.sigmoid(a0_ref[d:d + 1, :] + jnp.dot(aa, a2_ref[d].astype(BF16), preferred_element_type=F32))
        b_ref[d] = kk * a
        kd = k * (1.0 + (a - 1.0) * ka_ref[...])
        kd_ref[d] = kd
        kd_sum = kd if kd_sum is None else kd_sum + kd
    bonus_ref[...] = _head_sum(r * kd_sum * rk_ref[...]) * v


def _rw_prep(proj, proj_z, lp, *, r_blk, z_blk, p_tok, tp, ts, tm=256):
    assert tp % tm == 0 and ts % tm == 0
    n_tok = proj.shape[0]
    C = lp["rw_k_k"].shape[0]
    ZW = 4 * LANES
    full = lambda shape: pl.BlockSpec(shape, lambda i: (0,) * len(shape))
    tok = pl.BlockSpec((tm, C), lambda i: (i, 0))
    tok2 = pl.BlockSpec((2, tm, C), lambda i: (0, i, 0))
    one = jax.ShapeDtypeStruct((n_tok, C), F32)
    two = jax.ShapeDtypeStruct((2, n_tok, C), F32)
    return pl.pallas_call(
        functools.partial(_rw_prep_kernel, tm=tm, p_tok=p_tok, tp=tp, ts=ts, C=C),
        grid=(n_tok // tm,),
        in_specs=(_halo_specs(tm, C, r_blk, n_tok) + _halo_specs(tm, C, r_blk + 1, n_tok)
                  + _halo_specs(tm, C, r_blk + 2, n_tok) + _halo_specs(tm, ZW, z_blk, n_tok)
                  + [full((CONV_W, 3 * C)), full((CONV_W, ZW)), full((2, C)), full((2, LANES, C)),
                     full((2, C)), full((2, LANES, C)), full((2 * LANES, C)),
                     full((1, C)), full((1, C)), full((1, C))]),
        out_specs=[tok, tok, tok, tok2, tok2, tok2, tok, tok],
        out_shape=[one, one, one, two, two, two, one, one],
        scratch_shapes=[pltpu.VMEM((w // LANES, tm + 2 * SUBLANES, LANES), F32) for w in (C, C, C, ZW)],
        compiler_params=_params("parallel"),
        name="rwkv_prep",
    )(*([proj] * 9 + [proj_z] * 3), lp["cw_rkv"], lp["cw_z"], lp["rw_w0"], lp["w2_pad"], lp["rw_a0"], lp["a2_pad"],
      lp["g2_pad"], lp["rw_k_k"].reshape(1, C), lp["rw_k_a"].reshape(1, C), lp["rw_r_k"].reshape(1, C))


def _rw_scan_kernel(*refs, has_s0, nc):
    if has_s0:
        s0_ref, refs = refs[0], refs[1:]
    (rf_ref, kkf_ref, vf_ref, rb_ref, kkb_ref, vb_ref, lwf_ref, bf_ref, kdf_ref, lwb_ref, bb_ref, kdb_ref,
     yf_ref, yb_ref, so_ref, st_ref) = refs
    C, N = SCAN_CHUNK, RW_HEAD
    W = 2 * N
    npair = rf_ref.shape[1] // W
    c = pl.program_id(1)
    zero = jnp.zeros((), BF16)

    row = lax.broadcasted_iota(jnp.int32, (W, W), 0)
    col = lax.broadcasted_iota(jnp.int32, (W, W), 1)
    same_head = (row // N) == (col // N)
    eye = (row == col).astype(F32)

    @pl.when(c == 0)
    def _():
        if has_s0:
            sel = (lax.broadcasted_iota(jnp.int32, (N, W), 0) == lax.broadcasted_iota(jnp.int32, (N, W), 1) % N)
            for d in range(2):
                for p in range(npair):
                    tiled = _mmh(s0_ref[d, 2 * p:2 * p + 2].reshape(W, N), sel.astype(F32))
                    st_ref[d * npair + p] = jnp.where(same_head, tiled, 0.0)
        else:
            st_ref[...] = jnp.zeros(st_ref.shape, F32)

    def order(shape, dim, bwd):
        t = lax.broadcasted_iota(jnp.int32, shape, dim) % C
        return C - 1 - t if bwd else t

    chains = []
    for d, (r_ref, kk_ref, v_ref, lw_ref, b_ref, kd_ref, y_ref) in enumerate(
            [(rf_ref, kkf_ref, vf_ref, lwf_ref, bf_ref, kdf_ref, yf_ref),
             (rb_ref, kkb_ref, vb_ref, lwb_ref, bb_ref, kdb_ref, yb_ref)]):
        rt, ct = order((W, W), 0, d == 1), order((W, W), 1, d == 1)
        masks = dict(strict=rt > ct, incl=rt >= ct)
        masks.update({s: ((rt // (2 * s)) == (ct // (2 * s))) & ((rt // s) % 2 == 1) & ((ct // s) % 2 == 0)
                      for s in (1, 2, 4, 8, 16, 32)})
        lw = lw_ref[...]
        g_in = _cumsum_rows(order((C, C), 0, d == 1) >= order((C, C), 1, d == 1), lw)
        g_tot = jnp.sum(lw, axis=0, keepdims=True)
        e_neg = jnp.exp(-g_in)
        e_rem = jnp.exp(g_tot - g_in)
        tok = dict(
            kk=(kk_ref[...] * jnp.exp(g_in - lw)).astype(BF16), r=(r_ref[...] * jnp.exp(g_in)).astype(BF16),
            b=(b_ref[...] * e_neg).astype(BF16), kd=(kd_ref[...] * e_neg).astype(BF16),
            b_end=(b_ref[...] * e_rem).astype(BF16), kd_end=(kd_ref[...] * e_rem).astype(BF16),
            v=v_ref[...].astype(BF16), e_tot=jnp.exp(g_tot))
        for p in range(npair):
            chains.append((d * npair + p, slice(p * W, (p + 1) * W), tok, masks, y_ref))

    def expand(x, sl):
        return jnp.where(same_head, jnp.concatenate([x[:, sl]] * 2, axis=0), zero)

    ak = [expand(t["kk"], sl) for _, sl, t, _, _ in chains]
    bk = [jnp.concatenate([expand(t["b"], sl), expand(t["kd"], sl)], axis=0) for _, sl, t, _, _ in chains]
    vb = [expand(t["v"], sl) for _, sl, t, _, _ in chains]
    S = [st_ref[i] for i, *_ in chains]
    Sb = [x.astype(BF16) for x in S]
    n = range(len(chains))
    mk = [ch[3] for ch in chains]
    ar = [expand(t["r"], sl) for _, sl, t, _, _ in chains]
    lmn = [_mm1(jnp.concatenate([ak[i], ar[i]], axis=0), bk[i], "nt") for i in n]
    L = [jnp.where(mk[i]["strict"], lmn[i][:W, :W], 0.0) for i in n]
    Lb = [x.astype(BF16) for x in L]
    M = [jnp.where(mk[i]["strict"], lmn[i][:W, W:], 0.0).astype(BF16) for i in n]
    nbk = [jnp.concatenate([jnp.where(mk[i]["incl"], -lmn[i][W:, :W], 0.0),
                            jnp.where(mk[i]["incl"], lmn[i][W:, W:], 0.0)], axis=1).astype(BF16) for i in n]
    rhs = [(_mm1(ak[i], Sb[i], "nt") + _mm1(M[i], vb[i])).astype(BF16) for i in n]
    X = [eye - jnp.where(mk[i][1], L[i], 0.0) for i in n]
    for s in (2, 4, 8, 16, 32):
        Xb = [x.astype(BF16) for x in X]
        t = [_mm1(jnp.where(mk[i][s], Lb[i], zero), Xb[i]).astype(BF16) for i in n]
        X = [X[i] - _mm1(Xb[i], t[i]) for i in n]
    uv = [jnp.concatenate([_mm1(X[i].astype(BF16), rhs[i]).astype(BF16), vb[i]], axis=0) for i in n]
    for i, (slot, sl, tk, _, _) in enumerate(chains):
        ends = jnp.concatenate([-expand(tk["b_end"], sl), expand(tk["kd_end"], sl)], axis=0)
        st_ref[slot] = S[i] * tk["e_tot"][:, sl] + _mm1(uv[i], ends, "tn")
    for i, (_, sl, _, _, y_ref) in enumerate(chains):
        y = _mm1(ar[i], Sb[i], "nt") + _mm1(nbk[i], uv[i])
        y_ref[:, sl] = y[:C] + y[C:]

    @pl.when(c == nc - 1)
    def _():
        fold = (lax.broadcasted_iota(jnp.int32, (W, N), 0) % N == lax.broadcasted_iota(jnp.int32, (W, N), 1))
        for d in range(2):
            for p in range(npair):
                so_ref[d, 2 * p:2 * p + 2] = _mmh(st_ref[d * npair + p], fold.astype(F32)).reshape(2, N, N)


def _rw_scan(r, kk, v, lw, b, kd, *, row0, B, T, s0=None, layer=0):
    C = r.shape[1]
    H = C // RW_HEAD
    CH = SCAN_CHUNK
    nc = T // CH
    blk0 = row0 // CH
    fwd = pl.BlockSpec((CH, C), lambda bi, c: (blk0 + bi * nc + c, 0))
    bwd = pl.BlockSpec((CH, C), lambda bi, c: (blk0 + bi * nc + nc - 1 - c, 0))
    fwd2 = pl.BlockSpec((None, CH, C), lambda bi, c: (0, blk0 + bi * nc + c, 0))
    bwd2 = pl.BlockSpec((None, CH, C), lambda bi, c: (1, blk0 + bi * nc + nc - 1 - c, 0))
    in_specs = [fwd, fwd, fwd, bwd, bwd, bwd, fwd2, fwd2, fwd2, bwd2, bwd2, bwd2]
    args = [r, kk, v, r, kk, v, lw, b, kd, lw, b, kd]
    if s0 is not None:
        in_specs = [pl.BlockSpec((None, None, 2, H, RW_HEAD, RW_HEAD),
                                 lambda bi, c: (bi, layer, 0, 0, 0, 0))] + in_specs
        args = [s0] + args
    y_shape = jax.ShapeDtypeStruct((B * T, C), F32)
    return pl.pallas_call(
        functools.partial(_rw_scan_kernel, has_s0=s0 is not None, nc=nc),
        grid=(B, nc),
        in_specs=in_specs,
        out_specs=[pl.BlockSpec((CH, C), lambda bi, c: (bi * nc + c, 0)),
                   pl.BlockSpec((CH, C), lambda bi, c: (bi * nc + nc - 1 - c, 0)),
                   pl.BlockSpec((None, 2, H, RW_HEAD, RW_HEAD), lambda bi, c: (bi, 0, 0, 0, 0))],
        out_shape=[y_shape, y_shape, jax.ShapeDtypeStruct((B, 2, H, RW_HEAD, RW_HEAD), F32)],
        scratch_shapes=[pltpu.VMEM((H, 2 * RW_HEAD, 2 * RW_HEAD), F32)],
        compiler_params=_params("parallel", "arbitrary"),
        name="rwkv_scan_ctx" if s0 is not None else "rwkv_scan",
    )(*args)


def _rw_post_kernel(yf_ref, yb_ref, bonus_ref, gate_ref, g_ref, b_ref, _mix_ref, o_ref):
    y = yf_ref[...] + yb_ref[...]
    inv_n = 1.0 / RW_HEAD
    mu = _head_sum(y) * inv_n
    yc = y - mu
    var = _head_sum(yc * yc) * inv_n
    yn = yc * lax.rsqrt(var + RW_GN_EPS) * g_ref[...] + b_ref[...]
    o_ref[...] = ((yn + bonus_ref[...]) * gate_ref[...]).astype(o_ref.dtype)


def _rw_post(y_f, y_b, bonus, gate, mix, gn_g, gn_b, *, row0, out_blk, tm=256):
    n_rows, C = y_f.shape
    r0 = row0 // tm
    own = pl.BlockSpec((tm, C), lambda i: (i, 0))
    tok = pl.BlockSpec((tm, C), lambda i: (r0 + i, 0))
    vec = pl.BlockSpec((1, C), lambda i: (0, 0))
    return pl.pallas_call(
        _rw_post_kernel,
        grid=(n_rows // tm,),
        in_specs=[own, own, tok, tok, vec, vec, _any_spec()],
        out_specs=pl.BlockSpec((tm, C), lambda i: (r0 + i, out_blk)),
        out_shape=jax.ShapeDtypeStruct(mix.shape, mix.dtype),
        input_output_aliases={6: 0},
        compiler_params=_params("parallel"),
        name="rwkv_post",
    )(y_f, y_b, bonus, gate, gn_g.reshape(1, C), gn_b.reshape(1, C), mix)


def _pad_rows(w, rows, at):
    return jnp.zeros((rows, w.shape[1]), w.dtype).at[at:at + w.shape[0]].set(w)


def _layer_weights(l, D, w_in_b, rw_conv_w, rw_w2, rw_a2, rw_g2):
    C = D // 4
    o_z = 3 * (D // 2) + 3 * C
    n_z = 4 * RW_LORA_R + RW_GATE_R
    zpad = 4 * LANES - n_z
    cw = rw_conv_w[l]
    w = w_in_b[l]
    return dict(
        w_uvz=jnp.concatenate([w[:, o_z + n_z:], w[:, o_z:o_z + n_z], jnp.zeros((D, zpad), BF16)], axis=1),
        cw_rkv=cw[:, :3 * C],
        cw_z=jnp.concatenate([cw[:, 3 * C:], jnp.zeros((CONV_W, zpad), F32)], axis=1),
        w2_pad=jnp.stack([_pad_rows(rw_w2[l, d], LANES, d * RW_LORA_R) for d in range(2)]),
        a2_pad=jnp.stack([_pad_rows(rw_a2[l, d], LANES, d * RW_LORA_R) for d in range(2)]),
        g2_pad=_pad_rows(rw_g2[l], 2 * LANES, 0),
    )


def _rope_tables(T):
    n = DA_D // 4
    inv = ROPE_THETA ** (-jnp.arange(n, dtype=F32) / n)
    rows = T // GRID_W
    row = jnp.repeat(jnp.arange(rows), GRID_W).astype(F32)
    col = jnp.tile(jnp.arange(GRID_W), rows).astype(F32)
    sign = jnp.concatenate([-jnp.ones((n,), F32), jnp.ones((n,), F32)])
    cs, sn = [], []
    for pos in (row, col):
        ang = pos[:, None] * inv[None, :]
        cs.append(jnp.concatenate([jnp.cos(ang), jnp.cos(ang)], axis=1))
        sn.append(jnp.concatenate([jnp.sin(ang), jnp.sin(ang)], axis=1) * sign[None, :])
    cos = jnp.concatenate(cs, axis=1)
    sin = jnp.concatenate(sn, axis=1)
    return jnp.tile(cos, (1, 2)), jnp.tile(sin, (1, 2))


def kernel(x_prompt, x_sample, cache_da_k, cache_da_v, state_rwkv, c, c_ctx, mod_w, mod_b, norm1_g, norm2_g, w_in, da_lambda, da_subln_g, rw_conv_w, rw_w0, rw_w2, rw_a0, rw_a2, rw_g2, rw_k_k, rw_k_a, rw_r_k, rw_gn_g, rw_gn_b, cm_norm_g, cm_ws, cm_bs, w_out, ffn_up, ffn_conv_w, ffn_conv_b, ffn_down, final_norm_g):
    Bp, Tp, D = x_prompt.shape
    Bs, Ts, _ = x_sample.shape
    L = mod_w.shape[0]
    past = cache_da_k.shape[2]
    DA = D // 2
    H = DA // (2 * DA_D)
    C = D // 4
    F = ffn_down.shape[1]
    p_tok, s_tok = Bp * Tp, Bs * Ts
    assert Bs + 1 <= SUBLANES and p_tok % Ts == 0

    xs = (x_prompt.reshape(p_tok, D), x_sample.reshape(s_tok, D))
    cond8 = jnp.concatenate([c_ctx[None, :], c, jnp.zeros((SUBLANES - 1 - Bs, D), F32)], axis=0)
    mod = _modulation(cond8, mod_w, mod_b)
    ck4 = cache_da_k.reshape(Bs, L, past, DA)
    cv4 = cache_da_v.reshape(Bs, L, past, DA)
    cos, sin = _rope_tables(Ts)
    ko = jnp.zeros((Bp, L, Tp, DA), F32)
    vo = jnp.zeros((Bp, L, Tp, DA), F32)
    w_in_b, w_out_b = w_in.astype(BF16), w_out.astype(BF16)
    ffn_up_b, ffn_down_b = ffn_up.astype(BF16), ffn_down.astype(BF16)
    n_qkv_rkv = 3 * DA + 3 * C

    new_s = []
    for l in range(L):
        lw_ = _layer_weights(l, D, w_in_b, rw_conv_w, rw_w2, rw_a2, rw_g2)
        lp = dict(lw_, rw_w0=rw_w0[l], rw_a0=rw_a0[l], rw_k_k=rw_k_k[l], rw_k_a=rw_k_a[l], rw_r_k=rw_r_k[l])
        lam_init = 0.8 - 0.6 * math.exp(-0.3 * l)

        h = _norm_mod(xs, norm1_g[l], mod, l, 0, 1, p_tok, Ts)
        proj = _matmul(h, w_in_b, layer=l, n_cols=n_qkv_rkv, tm=1024, tn=512, name="proj_in")
        proj_uz = _matmul(h, lp["w_uvz"], tm=1024, tn=512, name="proj_in_uvz")
        ko, vo = _store_kv(proj, ko, vo, l, Bp=Bp, Tp=Tp, DA=DA)

        mix = jnp.zeros((p_tok + s_tok, D), BF16)
        mix = _attention(proj, mix, da_lambda[l], da_subln_g[l], lam_init, row0=0, B=Bp, T=Tp, H=H)
        mix = _attention(proj, mix, da_lambda[l], da_subln_g[l], lam_init, row0=p_tok, B=Bs, T=Ts, H=H,
                         ctx=(ck4, cv4, l, cos, sin))
        r_, kk_, v_, lg_, b_, kd_, gate_, bonus_ = _rw_prep(proj, proj_uz, lp, r_blk=3 * DA // C,
                                                            z_blk=2 * C // (4 * LANES), p_tok=p_tok, tp=Tp, ts=Ts)
        yf_p, yb_p, s_p = _rw_scan(r_, kk_, v_, lg_, b_, kd_, row0=0, B=Bp, T=Tp)
        yf_s, yb_s, _ = _rw_scan(r_, kk_, v_, lg_, b_, kd_, row0=p_tok, B=Bs, T=Ts, s0=state_rwkv, layer=l)
        mix = _rw_post(yf_p, yb_p, bonus_, gate_, mix, rw_gn_g[l], rw_gn_b[l], row0=0, out_blk=DA // C)
        mix = _rw_post(yf_s, yb_s, bonus_, gate_, mix, rw_gn_g[l], rw_gn_b[l], row0=p_tok, out_blk=DA // C)
        mix = _chunk_mlp(proj_uz, mix, cm_norm_g[l], cm_ws[l], cm_bs[l], u_blk=0, out_blk=DA // C + 1)

        x = _matmul(mix, w_out_b, layer=l, tm=1024, tn=512, resid=(xs, mod, l, 2 * D // 512, p_tok, Ts),
                    name="proj_out")
        xs = (x,)

        h = _norm_mod(xs, norm2_g[l], mod, l, 3, 4, p_tok, Ts)
        up = _matmul(h, ffn_up_b, layer=l, tm=1024, tn=512, name="ffn_up")
        act = _ffn_act(up, ffn_conv_w[l], ffn_conv_b[l], p_tok=p_tok, tp=Tp, ts=Ts)
        x = _matmul(act, ffn_down_b, layer=l, tm=1024, tn=512, tk=F // 2,
                    resid=(xs, mod, l, 5 * D // 512, p_tok, Ts), name="ffn_down")
        xs = (x,)

        new_s.append(s_p)

    y_p = _final_norm(x, final_norm_g, row0=0, n_rows=p_tok)
    y_s = _final_norm(x, final_norm_g, row0=p_tok, n_rows=s_tok)
    return (y_p.reshape(Bp, Tp, D), y_s.reshape(Bs, Ts, D), ko.reshape(Bp, L, Tp, H, 2, DA_D),
            vo.reshape(Bp, L, Tp, H, 2 * DA_D), jnp.stack(new_s, axis=1))
```

```python
import functools
import math

import jax
import jax.numpy as jnp
from jax import lax
from jax.experimental import pallas as pl
from jax.experimental.pallas import tpu as pltpu

F32 = jnp.float32
BF16 = jnp.bfloat16
HIGHEST = lax.Precision.HIGHEST

LANES = 128
SUBLANES = 8
VMEM_BYTES_V7X = 64 * 1024 * 1024
VMEM_BUDGET = VMEM_BYTES_V7X * 3 // 4

GRID_W = 64
DA_D = 128
RW_HEAD = 64
RW_LORA_R = 64
RW_GATE_R = 160
RW_GN_EPS = 64e-5
CM_GROUPS = 4
CM_CHUNK = 128
CONV_W = 3
ROPE_THETA = 10000.0
NORM_EPS = 1e-6
SCAN_CHUNK = 64


def _params(*sem):
    return pltpu.CompilerParams(dimension_semantics=sem, vmem_limit_bytes=VMEM_BUDGET)


def _any_spec():
    return pl.BlockSpec(memory_space=pl.ANY)


def _cond_row(tok0, p_tok, ts):
    return jnp.where(tok0 < p_tok, 0, 1 + jnp.maximum(tok0 - p_tok, 0) // ts)


def _mod_kernel(c_ref, w_ref, b_ref, o_ref):
    c = c_ref[...]
    s = (c * jax.nn.sigmoid(c)).astype(BF16)
    o_ref[...] = jnp.dot(s, w_ref[...].astype(BF16), preferred_element_type=F32) + b_ref[...]


def _modulation(cond8, mod_w, mod_b, tn=512):
    L, D, N = mod_w.shape
    return pl.pallas_call(
        _mod_kernel,
        grid=(L, N // tn),
        in_specs=[pl.BlockSpec((SUBLANES, D), lambda l, j: (0, 0)),
                  pl.BlockSpec((None, D, tn), lambda l, j: (l, 0, j)),
                  pl.BlockSpec((None, 1, tn), lambda l, j: (l, 0, j))],
        out_specs=pl.BlockSpec((None, SUBLANES, tn), lambda l, j: (l, 0, j)),
        out_shape=jax.ShapeDtypeStruct((L, SUBLANES, N), F32),
        compiler_params=_params("parallel", "parallel"),
        name="modulation",
    )(cond8, mod_w, mod_b.reshape(L, 1, N))


def _rows(refs, first_tiles):
    if len(refs) == 1:
        return refs[0][...]
    return jnp.where(pl.program_id(0) < first_tiles, refs[0][...], refs[1][...])


def _split_specs(xs, tm, block, idx):
    if len(xs) == 1:
        return [pl.BlockSpec(block, idx(lambda i: i))], 0
    first = xs[0].shape[0] // tm
    last = xs[1].shape[0] // tm - 1
    return [pl.BlockSpec(block, idx(lambda i: jnp.minimum(i, first - 1))),
            pl.BlockSpec(block, idx(lambda i: jnp.clip(i - first, 0, last)))], first


def _norm_mod_kernel(*refs, n_x, x_split, tm, p_tok, ts):
    x_refs, (g_ref, sh_ref, sc_ref, o_ref) = refs[:n_x], refs[n_x:]
    row = _cond_row(pl.program_id(0) * tm, p_tok, ts)
    x = _rows(x_refs, x_split)
    y = x * lax.rsqrt(jnp.mean(x * x, axis=-1, keepdims=True) + NORM_EPS)
    sc = sc_ref[pl.ds(row, 1), :]
    sh = sh_ref[pl.ds(row, 1), :]
    o_ref[...] = ((y * g_ref[...]) * (1.0 + sc) + sh).astype(o_ref.dtype)


def _norm_mod(xs, g, mod, layer, k_sh, k_sc, p_tok, ts, tm=256):
    n = sum(x.shape[0] for x in xs)
    D = xs[0].shape[1]
    assert p_tok % tm == 0 and ts % tm == 0, "a row tile must not mix modulation groups"
    x_specs, x_split = _split_specs(xs, tm, (tm, D), lambda r: (lambda i: (r(i), 0)))
    return pl.pallas_call(
        functools.partial(_norm_mod_kernel, n_x=len(xs), x_split=x_split, tm=tm, p_tok=p_tok, ts=ts),
        grid=(n // tm,),
        in_specs=x_specs + [pl.BlockSpec((1, D), lambda i: (0, 0)),
                            pl.BlockSpec((None, SUBLANES, D), lambda i: (layer, 0, k_sh)),
                            pl.BlockSpec((None, SUBLANES, D), lambda i: (layer, 0, k_sc))],
        out_specs=pl.BlockSpec((tm, D), lambda i: (i, 0)),
        out_shape=jax.ShapeDtypeStruct((n, D), BF16),
        compiler_params=_params("parallel"),
        name="norm_mod",
    )(*xs, g.reshape(1, D), mod, mod)


def _final_norm_kernel(x_ref, g_ref, o_ref):
    x = x_ref[...]
    o_ref[...] = x * lax.rsqrt(jnp.mean(x * x, axis=-1, keepdims=True) + NORM_EPS) * g_ref[...]


def _final_norm(x, g, *, row0, n_rows, tm=256):
    D = x.shape[1]
    r0 = row0 // tm
    return pl.pallas_call(
        _final_norm_kernel,
        grid=(n_rows // tm,),
        in_specs=[pl.BlockSpec((tm, D), lambda i: (r0 + i, 0)), pl.BlockSpec((1, D), lambda i: (0, 0))],
        out_specs=pl.BlockSpec((tm, D), lambda i: (i, 0)),
        out_shape=jax.ShapeDtypeStruct((n_rows, D), F32),
        compiler_params=_params("parallel"),
        name="final_norm",
    )(x, g.reshape(1, D))


def _mm_kernel(*refs, nk, n_x, x_split, tm, p_tok, ts):
    a_ref, b_ref = refs[:2]
    x_refs, g_ref = refs[2:2 + n_x], (refs[2 + n_x] if n_x else None)
    o_ref = refs[3 + n_x] if n_x else refs[2]
    acc_ref = refs[-1] if nk > 1 else None

    def finish(acc):
        if n_x:
            row = _cond_row(pl.program_id(0) * tm, p_tok, ts)
            o_ref[...] = _rows(x_refs, x_split) + g_ref[pl.ds(row, 1), :] * acc
        else:
            o_ref[...] = acc

    part = jnp.dot(a_ref[...], b_ref[...].astype(BF16), preferred_element_type=F32)
    if nk == 1:
        finish(part)
    else:
        k = pl.program_id(2)

        @pl.when(k == 0)
        def _():
            acc_ref[...] = part

        @pl.when((k > 0) & (k < nk - 1))
        def _():
            acc_ref[...] += part

        @pl.when(k == nk - 1)
        def _():
            finish(acc_ref[...] + part)


def _matmul(a, b, *, tm, tn, tk=None, layer=None, n_cols=None, resid=None, name):
    M, K = a.shape
    N = b.shape[-1] if n_cols is None else n_cols
    tk = K if tk is None else tk
    nk = K // tk
    if layer is None:
        b_spec = pl.BlockSpec((tk, tn), lambda i, j, k: (k, j))
    else:
        b_spec = pl.BlockSpec((None, tk, tn), lambda i, j, k: (layer, k, j))
    in_specs = [pl.BlockSpec((tm, tk), lambda i, j, k: (i, k)), b_spec]
    args = [a, b]
    p_tok = ts = x_split = n_x = 0
    if resid is not None:
        xs, mod, mod_layer, gate_blk, p_tok, ts = resid
        assert p_tok % tm == 0 and ts % tm == 0, "a row tile must not mix modulation groups"
        x_specs, x_split = _split_specs(xs, tm, (tm, tn), lambda r: (lambda i, j, k: (r(i), j)))
        n_x = len(xs)
        in_specs += x_specs + [pl.BlockSpec((None, SUBLANES, tn), lambda i, j, k: (mod_layer, 0, gate_blk + j))]
        args += list(xs) + [mod]
    return pl.pallas_call(
        functools.partial(_mm_kernel, nk=nk, n_x=n_x, x_split=x_split, tm=tm, p_tok=p_tok, ts=ts),
        grid=(M // tm, N // tn, nk),
        in_specs=in_specs,
        out_specs=pl.BlockSpec((tm, tn), lambda i, j, k: (i, j)),
        out_shape=jax.ShapeDtypeStruct((M, N), F32),
        scratch_shapes=[pltpu.VMEM((tm, tn), F32)] if nk > 1 else [],
        compiler_params=_params("parallel", "parallel", "arbitrary"),
        name=name,
    )(*args)


def _store_kv_kernel(k_ref, v_ref, _ko_in, _vo_in, ko_ref, vo_ref):
    ko_ref[...] = k_ref[...]
    vo_ref[...] = v_ref[...]


def _store_kv(proj, ko, vo, layer, *, Bp, Tp, DA):
    out = pl.BlockSpec((None, None, Tp, DA), lambda b: (b, layer, 0, 0))
    return pl.pallas_call(
        _store_kv_kernel,
        grid=(Bp,),
        in_specs=[pl.BlockSpec((Tp, DA), lambda b: (b, 1)), pl.BlockSpec((Tp, DA), lambda b: (b, 2)),
                  _any_spec(), _any_spec()],
        out_specs=[out, out],
        out_shape=[jax.ShapeDtypeStruct(ko.shape, ko.dtype), jax.ShapeDtypeStruct(vo.shape, vo.dtype)],
        input_output_aliases={2: 0, 3: 1},
        compiler_params=_params("parallel"),
        name="store_kv",
    )(proj, proj, ko, vo)


def _rope(x, cos, sin_signed):
    lane = lax.broadcasted_iota(jnp.int32, x.shape, 1)
    width = x.shape[1]
    rot = jnp.where((lane % 64) < 32, pltpu.roll(x, width - 32, 1), pltpu.roll(x, 32, 1))
    return x * cos + rot * sin_signed


def _attn_kernel(*refs, rope, lam_init, key_block):
    if rope:
        (lam_ref, q_ref, k_ref, v_ref, kc_ref, vc_ref, cq_ref, sq_ref, ck_ref, sk_ref, g_ref, _mix_ref,
         o_ref, kr_ref, vr_ref) = refs
    else:
        lam_ref, q_ref, k_ref, v_ref, g_ref, _mix_ref, o_ref = refs
    lm = lam_ref[...]
    s1 = jnp.sum(lm[0:1] * lm[1:2], axis=-1, keepdims=True)
    s2 = jnp.sum(lm[2:3] * lm[3:4], axis=-1, keepdims=True)
    lam = jnp.exp(s1) - jnp.exp(s2) + lam_init
    qscale = DA_D ** -0.5 * math.log2(math.e)

    if rope:
        @pl.when(pl.program_id(2) == 0)
        def _():
            kr_ref[...] = _rope(k_ref[...], ck_ref[...], sk_ref[...]).astype(BF16)
            vr_ref[...] = v_ref[...].astype(BF16)

        q = (_rope(q_ref[...], cq_ref[...], sq_ref[...]) * qscale).astype(BF16)
        T = kr_ref.shape[0]
        blocks = [(kr_ref[j * key_block:(j + 1) * key_block], vr_ref[j * key_block:(j + 1) * key_block])
                  for j in range(T // key_block)]
        blocks.append((kc_ref[...].astype(BF16), vc_ref[...].astype(BF16)))
    else:
        q = (q_ref[...] * qscale).astype(BF16)
        blocks = [(k_ref[...].astype(BF16), v_ref[...].astype(BF16))]

    cols = [slice(c * DA_D, (c + 1) * DA_D) for c in range(2)]
    ss = [[_mm1(q[:, cols[c]], kb[:, cols[c]], "nt") for c in range(2)] for kb, _ in blocks]
    m = [functools.reduce(jnp.maximum, [jnp.max(s[c], axis=-1, keepdims=True) for s in ss]) for c in range(2)]
    o = [None, None]
    d = [None, None]
    for s, (_, vb) in zip(ss, blocks):
        for c in range(2):
            e = jnp.exp2(s[c] - m[c])
            t = jnp.sum(e, axis=-1, keepdims=True)
            u = _mm1(e.astype(BF16), vb)
            d[c] = t if d[c] is None else d[c] + t
            o[c] = u if o[c] is None else o[c] + u
    maps = list(zip(o, d))
    o = maps[0][0] * (1.0 / maps[0][1]) - maps[1][0] * (lam / maps[1][1])
    y = o * lax.rsqrt(jnp.mean(o * o, axis=-1, keepdims=True) + NORM_EPS)
    o_ref[...] = (y * g_ref[...] * (1.0 - lam_init)).astype(o_ref.dtype)


def _attention(proj, mix, lam_p, g, lam_init, *, row0, B, T, H, ctx=None, tq=256, key_block=512):
    W = 2 * DA_D
    nq = T // tq
    qb0 = row0 // tq
    kb0 = row0 // T
    rope = ctx is not None
    in_specs = [pl.BlockSpec((4, DA_D), lambda b, h, i: (0, 0)),
                pl.BlockSpec((tq, W), lambda b, h, i: (qb0 + b * nq + i, h)),
                pl.BlockSpec((T, W), lambda b, h, i: (kb0 + b, H + h)),
                pl.BlockSpec((T, W), lambda b, h, i: (kb0 + b, 2 * H + h))]
    args = [lam_p, proj, proj, proj]
    scratch = []
    if rope:
        ck, cv, layer, cos, sin = ctx
        past = ck.shape[2]
        in_specs += [pl.BlockSpec((None, None, past, W), lambda b, h, i: (b, layer, 0, h)),
                     pl.BlockSpec((None, None, past, W), lambda b, h, i: (b, layer, 0, h)),
                     pl.BlockSpec((tq, W), lambda b, h, i: (i, 0)),
                     pl.BlockSpec((tq, W), lambda b, h, i: (i, 0)),
                     pl.BlockSpec((T, W), lambda b, h, i: (0, 0)),
                     pl.BlockSpec((T, W), lambda b, h, i: (0, 0))]
        args += [ck, cv, cos, sin, cos, sin]
        scratch = [pltpu.VMEM((T, W), BF16), pltpu.VMEM((T, W), BF16)]
    in_specs += [pl.BlockSpec((1, W), lambda b, h, i: (0, 0)), _any_spec()]
    args += [g.reshape(1, W), mix]
    return pl.pallas_call(
        functools.partial(_attn_kernel, rope=rope, lam_init=lam_init, key_block=key_block),
        grid=(B, H, nq),
        in_specs=in_specs,
        out_specs=pl.BlockSpec((tq, W), lambda b, h, i: (qb0 + b * nq + i, h)),
        out_shape=jax.ShapeDtypeStruct(mix.shape, mix.dtype),
        input_output_aliases={len(args) - 1: 0},
        scratch_shapes=scratch,
        compiler_params=_params("parallel", "parallel", "arbitrary"),
        name="diff_attn_ctx" if rope else "diff_attn",
    )(*args)


def _tile_edges(tok0, tm, p_tok, tp, ts):
    in_p = tok0 < p_tok
    pos = jnp.where(in_p, tok0 % tp, jnp.maximum(tok0 - p_tok, 0) % ts)
    length = jnp.where(in_p, tp, ts)
    return (pos != 0).astype(F32), (pos + tm != length).astype(F32)


def _conv3(x_ref, p_ref, n_ref, s_ref, w, keep_prev, keep_next, cols=slice(None)):
    tm = x_ref.shape[0]
    s_ref[SUBLANES:SUBLANES + tm, :] = x_ref[:, cols]
    s_ref[SUBLANES - 1:SUBLANES, :] = p_ref[SUBLANES - 1:SUBLANES, cols] * keep_prev
    s_ref[SUBLANES + tm:SUBLANES + tm + 1, :] = n_ref[0:1, cols] * keep_next
    return (s_ref[SUBLANES - 1:SUBLANES - 1 + tm, :] * w[0:1] + x_ref[:, cols] * w[1:2]
            + s_ref[SUBLANES + 1:SUBLANES + 1 + tm, :] * w[2:3])


def _halo_specs(tm, width, col_blk, n_tok, nidx=1):
    r = tm // SUBLANES
    last = n_tok // SUBLANES - 1
    if nidx == 1:
        return [pl.BlockSpec((tm, width), lambda i: (i, col_blk)),
                pl.BlockSpec((SUBLANES, width), lambda i: (jnp.maximum(i * r - 1, 0), col_blk)),
                pl.BlockSpec((SUBLANES, width), lambda i: (jnp.minimum((i + 1) * r, last), col_blk))]
    return [pl.BlockSpec((tm, width), lambda i, j: (i, col_blk + j)),
            pl.BlockSpec((SUBLANES, width), lambda i, j: (jnp.maximum(i * r - 1, 0), col_blk + j)),
            pl.BlockSpec((SUBLANES, width), lambda i, j: (jnp.minimum((i + 1) * r, last), col_blk + j))]


def _cmlp_kernel(u_ref, v_ref, gain_ref, ws_ref, bs_ref, _mix_ref, o_ref, *, tm):
    v = v_ref[...]
    z = (v * lax.rsqrt(jnp.mean(v * v, axis=-1, keepdims=True) + NORM_EPS) * gain_ref[...]).astype(BF16)
    gw = z.shape[1] // CM_GROUPS
    for n in range(tm // CM_CHUNK):
        rows = slice(n * CM_CHUNK, (n + 1) * CM_CHUNK)
        for g in range(CM_GROUPS):
            cols = slice(g * gw, (g + 1) * gw)
            t = jnp.dot(ws_ref[g].astype(BF16), z[rows, cols], preferred_element_type=F32) + bs_ref[g]
            o_ref[rows, cols] = (u_ref[rows, cols] * t).astype(o_ref.dtype)


def _chunk_mlp(proj, mix, gain, ws, bs, *, u_blk, out_blk, tm=256):
    n_tok = proj.shape[0]
    W = gain.shape[0]
    gw = W // CM_GROUPS
    bs_b = jnp.broadcast_to(bs[:, :, None], (CM_GROUPS, CM_CHUNK, gw))
    return pl.pallas_call(
        functools.partial(_cmlp_kernel, tm=tm),
        grid=(n_tok // tm,),
        in_specs=[pl.BlockSpec((tm, W), lambda i: (i, u_blk)),
                  pl.BlockSpec((tm, W), lambda i: (i, u_blk + 1)),
                  pl.BlockSpec((1, W), lambda i: (0, 0)),
                  pl.BlockSpec((CM_GROUPS, CM_CHUNK, CM_CHUNK), lambda i: (0, 0, 0)),
                  pl.BlockSpec((CM_GROUPS, CM_CHUNK, gw), lambda i: (0, 0, 0)),
                  _any_spec()],
        out_specs=pl.BlockSpec((tm, W), lambda i: (i, out_blk)),
        out_shape=jax.ShapeDtypeStruct(mix.shape, mix.dtype),
        input_output_aliases={5: 0},
        compiler_params=_params("parallel"),
        name="chunk_mlp",
    )(proj, proj, gain.reshape(1, W), ws, bs_b, mix)


def _ffn_act_kernel(a_ref, ap_ref, an_ref, b_ref, bp_ref, bn_ref, wa_ref, wb_ref, ba_ref, bb_ref, o_ref,
                    sa_ref, sb_ref, *, tm, tc, unroll, p_tok, tp, ts):
    keep_prev, keep_next = _tile_edges(pl.program_id(0) * tm, tm, p_tok, tp, ts)

    def chunk(j, slab):
        cols = pl.ds(pl.multiple_of(j * LANES, LANES), LANES)
        ga = _conv3(a_ref, ap_ref, an_ref, sa_ref.at[slab], wa_ref[:, cols], keep_prev, keep_next, cols)
        gb = _conv3(b_ref, bp_ref, bn_ref, sb_ref.at[slab], wb_ref[:, cols], keep_prev, keep_next, cols)
        ga = ga + ba_ref[:, cols]
        o_ref[:, cols] = (ga * jax.nn.sigmoid(ga) * (gb + bb_ref[:, cols])).astype(o_ref.dtype)

    n_chunks = tc // LANES

    def body(g, carry):
        for u in range(unroll):
            chunk(g * unroll + u, u)
        return carry

    lax.fori_loop(0, n_chunks // unroll, body, 0)
    for u in range(n_chunks % unroll):
        chunk(n_chunks - n_chunks % unroll + u, u)


def _ffn_act(up, conv_w, conv_b, *, p_tok, tp, ts, tm=256, col_blocks=2):
    n_tok, two_f = up.shape
    F = two_f // 2
    tc = F // col_blocks
    assert tp % tm == 0 and ts % tm == 0 and tc % LANES == 0
    cb = conv_b.reshape(1, two_f)
    wspec = lambda off: pl.BlockSpec((CONV_W, tc), lambda i, j: (0, off + j))
    bspec = lambda off: pl.BlockSpec((1, tc), lambda i, j: (0, off + j))
    unroll = 4
    halo = pltpu.VMEM((unroll, tm + 2 * SUBLANES, LANES), F32)
    return pl.pallas_call(
        functools.partial(_ffn_act_kernel, tm=tm, tc=tc, unroll=unroll, p_tok=p_tok, tp=tp, ts=ts),
        grid=(n_tok // tm, col_blocks),
        in_specs=(_halo_specs(tm, tc, 0, n_tok, 2) + _halo_specs(tm, tc, col_blocks, n_tok, 2)
                  + [wspec(0), wspec(col_blocks), bspec(0), bspec(col_blocks)]),
        out_specs=pl.BlockSpec((tm, tc), lambda i, j: (i, j)),
        out_shape=jax.ShapeDtypeStruct((n_tok, F), BF16),
        scratch_shapes=[halo, halo],
        compiler_params=_params("parallel", "parallel"),
        name="ffn_act",
    )(up, up, up, up, up, up, conv_w, conv_w, cb, cb)


_DIMS = {"nn": ((1,), (0,)), "nt": ((1,), (1,)), "tn": ((0,), (0,))}


def _mm1(a, b, kind="nn"):
    return lax.dot_general(a, b, (_DIMS[kind], ((), ())), preferred_element_type=F32)


def _mmh(a, b, dims=(((1,), (0,)), ((), ()))):
    return lax.dot_general(a, b, dims, precision=HIGHEST, preferred_element_type=F32)


def _split3(x):
    hi = x.astype(BF16)
    r1 = x - hi.astype(F32)
    mid = r1.astype(BF16)
    return hi, mid, (r1 - mid.astype(F32)).astype(BF16)


def _head_sum(x):
    r = lax.broadcasted_iota(jnp.int32, (3 * LANES, LANES), 0) % LANES // RW_HEAD
    c = lax.broadcasted_iota(jnp.int32, (3 * LANES, LANES), 1) // RW_HEAD
    e3 = jnp.where(r == c, 1.0, 0.0).astype(BF16)
    parts = _split3(x)
    cols = [_mm1(jnp.concatenate([p[:, j * LANES:(j + 1) * LANES] for p in parts], axis=1), e3)
            for j in range(x.shape[1] // LANES)]
    return jnp.concatenate(cols, axis=1)


def _cumsum_rows(tri, x):
    t = tri.astype(BF16)
    return _mm1(jnp.concatenate([t, t, t], axis=1), jnp.concatenate(_split3(x), axis=0))


def _rw_prep_kernel(r_ref, rp_ref, rn_ref, k_ref, kp_ref, kn_ref, v_ref, vp_ref, vn_ref,
                    z_ref, zp_ref, zn_ref, cw_ref, cz_ref, w0_ref, w2_ref, a0_ref, a2_ref, g2_ref,
                    kk_ref, ka_ref, rk_ref,
                    ro_ref, kko_ref, vo_ref, lw_ref, b_ref, kd_ref, gate_ref, bonus_ref,
                    sr_ref, sk_ref, sv_ref, sz_ref, *, tm, p_tok, tp, ts, C):
    edges = _tile_edges(pl.program_id(0) * tm, tm, p_tok, tp, ts)

    def conv(x_ref, p_ref, n_ref, s_ref, w):
        chunks = []
        for j in range(x_ref.shape[1] // LANES):
            cols = slice(j * LANES, (j + 1) * LANES)
            chunks.append(_conv3(x_ref, p_ref, n_ref, s_ref.at[j], w[:, cols], *edges, cols))
        return jnp.concatenate(chunks, axis=1)

    cw = cw_ref[...]
    r = conv(r_ref, rp_ref, rn_ref, sr_ref, cw[:, 0:C])
    k = conv(k_ref, kp_ref, kn_ref, sk_ref, cw[:, C:2 * C])
    v = conv(v_ref, vp_ref, vn_ref, sv_ref, cw[:, 2 * C:3 * C])
    z = conv(z_ref, zp_ref, zn_ref, sz_ref, cz_ref[...])
    dec = jnp.tanh(z[:, 0:LANES]).astype(BF16)
    aa = z[:, LANES:2 * LANES].astype(BF16)
    gl = jax.nn.sigmoid(z[:, 2 * LANES:4 * LANES]).astype(BF16)
    gate_ref[...] = jnp.dot(gl, g2_ref[...].astype(BF16), preferred_element_type=F32)
    kk = k * kk_ref[...]
    kk = kk * lax.rsqrt(_head_sum(kk * kk) + 1e-12)
    ro_ref[...] = r
    kko_ref[...] = kk
    vo_ref[...] = v
    kd_sum = None
    for d in range(2):
        wl = w0_ref[d:d + 1, :] + jnp.dot(dec, w2_ref[d].astype(BF16), preferred_element_type=F32)
        lw_ref[d] = -math.exp(-0.5) * jax.nn.sigmoid(wl)
        a = jax.nn.sigmoid(a0_ref[d:d + 1, :] + jnp.dot(aa, a2_ref[d].astype(BF16), preferred_element_type=F32))
        b_ref[d] = kk * a
        kd = k * (1.0 + (a - 1.0) * ka_ref[...])
        kd_ref[d] = kd
        kd_sum = kd if kd_sum is None else kd_sum + kd
    bonus_ref[...] = _head_sum(r * kd_sum * rk_ref[...]) * v


def _rw_prep(proj, proj_z, lp, *, r_blk, z_blk, p_tok, tp, ts, tm=256):
    assert tp % tm == 0 and ts % tm == 0
    n_tok = proj.shape[0]
    C = lp["rw_k_k"].shape[0]
    ZW = 4 * LANES
    full = lambda shape: pl.BlockSpec(shape, lambda i: (0,) * len(shape))
    tok = pl.BlockSpec((tm, C), lambda i: (i, 0))
    tok2 = pl.BlockSpec((2, tm, C), lambda i: (0, i, 0))
    one = jax.ShapeDtypeStruct((n_tok, C), F32)
    two = jax.ShapeDtypeStruct((2, n_tok, C), F32)
    return pl.pallas_call(
        functools.partial(_rw_prep_kernel, tm=tm, p_tok=p_tok, tp=tp, ts=ts, C=C),
        grid=(n_tok // tm,),
        in_specs=(_halo_specs(tm, C, r_blk, n_tok) + _halo_specs(tm, C, r_blk + 1, n_tok)
                  + _halo_specs(tm, C, r_blk + 2, n_tok) + _halo_specs(tm, ZW, z_blk, n_tok)
                  + [full((CONV_W, 3 * C)), full((CONV_W, ZW)), full((2, C)), full((2, LANES, C)),
                     full((2, C)), full((2, LANES, C)), full((2 * LANES, C)),
                     full((1, C)), full((1, C)), full((1, C))]),
        out_specs=[tok, tok, tok, tok2, tok2, tok2, tok, tok],
        out_shape=[one, one, one, two, two, two, one, one],
        scratch_shapes=[pltpu.VMEM((w // LANES, tm + 2 * SUBLANES, LANES), F32) for w in (C, C, C, ZW)],
        compiler_params=_params("parallel"),
        name="rwkv_prep",
    )(*([proj] * 9 + [proj_z] * 3), lp["cw_rkv"], lp["cw_z"], lp["rw_w0"], lp["w2_pad"], lp["rw_a0"], lp["a2_pad"],
      lp["g2_pad"], lp["rw_k_k"].reshape(1, C), lp["rw_k_a"].reshape(1, C), lp["rw_r_k"].reshape(1, C))


def _rw_scan_kernel(*refs, has_s0, nc):
    if has_s0:
        s0_ref, refs = refs[0], refs[1:]
    (rf_ref, kkf_ref, vf_ref, rb_ref, kkb_ref, vb_ref, lwf_ref, bf_ref, kdf_ref, lwb_ref, bb_ref, kdb_ref,
     yf_ref, yb_ref, so_ref, st_ref) = refs
    C, N = SCAN_CHUNK, RW_HEAD
    W = 2 * N
    npair = rf_ref.shape[1] // W
    c = pl.program_id(1)
    zero = jnp.zeros((), BF16)

    row = lax.broadcasted_iota(jnp.int32, (W, W), 0)
    col = lax.broadcasted_iota(jnp.int32, (W, W), 1)
    same_head = (row // N) == (col // N)
    eye = (row == col).astype(F32)

    @pl.when(c == 0)
    def _():
        if has_s0:
            sel = (lax.broadcasted_iota(jnp.int32, (N, W), 0) == lax.broadcasted_iota(jnp.int32, (N, W), 1) % N)
            for d in range(2):
                for p in range(npair):
                    tiled = _mmh(s0_ref[d, 2 * p:2 * p + 2].reshape(W, N), sel.astype(F32))
                    st_ref[d * npair + p] = jnp.where(same_head, tiled, 0.0)
        else:
            st_ref[...] = jnp.zeros(st_ref.shape, F32)

    def order(shape, dim, bwd):
        t = lax.broadcasted_iota(jnp.int32, shape, dim) % C
        return C - 1 - t if bwd else t

    chains = []
    for d, (r_ref, kk_ref, v_ref, lw_ref, b_ref, kd_ref, y_ref) in enumerate(
            [(rf_ref, kkf_ref, vf_ref, lwf_ref, bf_ref, kdf_ref, yf_ref),
             (rb_ref, kkb_ref, vb_ref, lwb_ref, bb_ref, kdb_ref, yb_ref)]):
        rt, ct = order((W, W), 0, d == 1), order((W, W), 1, d == 1)
        masks = dict(strict=rt > ct, incl=rt >= ct)
        masks.update({s: ((rt // (2 * s)) == (ct // (2 * s))) & ((rt // s) % 2 == 1) & ((ct // s) % 2 == 0)
                      for s in (1, 2, 4, 8, 16, 32)})
        lw = lw_ref[...]
        g_in = _cumsum_rows(order((C, C), 0, d == 1) >= order((C, C), 1, d == 1), lw)
        g_tot = jnp.sum(lw, axis=0, keepdims=True)
        e_neg = jnp.exp(-g_in)
        e_rem = jnp.exp(g_tot - g_in)
        tok = dict(
            kk=(kk_ref[...] * jnp.exp(g_in - lw)).astype(BF16), r=(r_ref[...] * jnp.exp(g_in)).astype(BF16),
            b=(b_ref[...] * e_neg).astype(BF16), kd=(kd_ref[...] * e_neg).astype(BF16),
            b_end=(b_ref[...] * e_rem).astype(BF16), kd_end=(kd_ref[...] * e_rem).astype(BF16),
            v=v_ref[...].astype(BF16), e_tot=jnp.exp(g_tot))
        for p in range(npair):
            chains.append((d * npair + p, slice(p * W, (p + 1) * W), tok, masks, y_ref))

    def expand(x, sl):
        return jnp.where(same_head, jnp.concatenate([x[:, sl]] * 2, axis=0), zero)

    ak = [expand(t["kk"], sl) for _, sl, t, _, _ in chains]
    bk = [jnp.concatenate([expand(t["b"], sl), expand(t["kd"], sl)], axis=0) for _, sl, t, _, _ in chains]
    vb = [expand(t["v"], sl) for _, sl, t, _, _ in chains]
    S = [st_ref[i] for i, *_ in chains]
    Sb = [x.astype(BF16) for x in S]
    n = range(len(chains))
    mk = [ch[3] for ch in chains]
    ar = [expand(t["r"], sl) for _, sl, t, _, _ in chains]
    lmn = [_mm1(jnp.concatenate([ak[i], ar[i]], axis=0), bk[i], "nt") for i in n]
    L = [jnp.where(mk[i]["strict"], lmn[i][:W, :W], 0.0) for i in n]
    Lb = [x.astype(BF16) for x in L]
    M = [jnp.where(mk[i]["strict"], lmn[i][:W, W:], 0.0).astype(BF16) for i in n]
    nbk = [jnp.concatenate([jnp.where(mk[i]["incl"], -lmn[i][W:, :W], 0.0),
                            jnp.where(mk[i]["incl"], lmn[i][W:, W:], 0.0)], axis=1).astype(BF16) for i in n]
    rhs = [(_mm1(ak[i], Sb[i], "nt") + _mm1(M[i], vb[i])).astype(BF16) for i in n]
    X = [eye - jnp.where(mk[i][1], L[i], 0.0) for i in n]
    for s in (2, 4, 8, 16, 32):
        Xb = [x.astype(BF16) for x in X]
        t = [_mm1(jnp.where(mk[i][s], Lb[i], zero), Xb[i]).astype(BF16) for i in n]
        X = [X[i] - _mm1(Xb[i], t[i]) for i in n]
    uv = [jnp.concatenate([_mm1(X[i].astype(BF16), rhs[i]).astype(BF16), vb[i]], axis=0) for i in n]
    for i, (slot, sl, tk, _, _) in enumerate(chains):
        ends = jnp.concatenate([-expand(tk["b_end"], sl), expand(tk["kd_end"], sl)], axis=0)
        st_ref[slot] = S[i] * tk["e_tot"][:, sl] + _mm1(uv[i], ends, "tn")
    for i, (_, sl, _, _, y_ref) in enumerate(chains):
        y = _mm1(ar[i], Sb[i], "nt") + _mm1(nbk[i], uv[i])
        y_ref[:, sl] = y[:C] + y[C:]

    @pl.when(c == nc - 1)
    def _():
        fold = (lax.broadcasted_iota(jnp.int32, (W, N), 0) % N == lax.broadcasted_iota(jnp.int32, (W, N), 1))
        for d in range(2):
            for p in range(npair):
                so_ref[d, 2 * p:2 * p + 2] = _mmh(st_ref[d * npair + p], fold.astype(F32)).reshape(2, N, N)


def _rw_scan(r, kk, v, lw, b, kd, *, row0, B, T, s0=None, layer=0):
    C = r.shape[1]
    H = C // RW_HEAD
    CH = SCAN_CHUNK
    nc = T // CH
    blk0 = row0 // CH
    fwd = pl.BlockSpec((CH, C), lambda bi, c: (blk0 + bi * nc + c, 0))
    bwd = pl.BlockSpec((CH, C), lambda bi, c: (blk0 + bi * nc + nc - 1 - c, 0))
    fwd2 = pl.BlockSpec((None, CH, C), lambda bi, c: (0, blk0 + bi * nc + c, 0))
    bwd2 = pl.BlockSpec((None, CH, C), lambda bi, c: (1, blk0 + bi * nc + nc - 1 - c, 0))
    in_specs = [fwd, fwd, fwd, bwd, bwd, bwd, fwd2, fwd2, fwd2, bwd2, bwd2, bwd2]
    args = [r, kk, v, r, kk, v, lw, b, kd, lw, b, kd]
    if s0 is not None:
        in_specs = [pl.BlockSpec((None, None, 2, H, RW_HEAD, RW_HEAD),
                                 lambda bi, c: (bi, layer, 0, 0, 0, 0))] + in_specs
        args = [s0] + args
    y_shape = jax.ShapeDtypeStruct((B * T, C), F32)
    return pl.pallas_call(
        functools.partial(_rw_scan_kernel, has_s0=s0 is not None, nc=nc),
        grid=(B, nc),
        in_specs=in_specs,
        out_specs=[pl.BlockSpec((CH, C), lambda bi, c: (bi * nc + c, 0)),
                   pl.BlockSpec((CH, C), lambda bi, c: (bi * nc + nc - 1 - c, 0)),
                   pl.BlockSpec((None, 2, H, RW_HEAD, RW_HEAD), lambda bi, c: (bi, 0, 0, 0, 0))],
        out_shape=[y_shape, y_shape, jax.ShapeDtypeStruct((B, 2, H, RW_HEAD, RW_HEAD), F32)],
        scratch_shapes=[pltpu.VMEM((H, 2 * RW_HEAD, 2 * RW_HEAD), F32)],
        compiler_params=_params("parallel", "arbitrary"),
        name="rwkv_scan_ctx" if s0 is not None else "rwkv_scan",
    )(*args)


def _rw_post_kernel(yf_ref, yb_ref, bonus_ref, gate_ref, g_ref, b_ref, _mix_ref, o_ref):
    y = yf_ref[...] + yb_ref[...]
    inv_n = 1.0 / RW_HEAD
    mu = _head_sum(y) * inv_n
    yc = y - mu
    var = _head_sum(yc * yc) * inv_n
    yn = yc * lax.rsqrt(var + RW_GN_EPS) * g_ref[...] + b_ref[...]
    o_ref[...] = ((yn + bonus_ref[...]) * gate_ref[...]).astype(o_ref.dtype)


def _rw_post(y_f, y_b, bonus, gate, mix, gn_g, gn_b, *, row0, out_blk, tm=256):
    n_rows, C = y_f.shape
    r0 = row0 // tm
    own = pl.BlockSpec((tm, C), lambda i: (i, 0))
    tok = pl.BlockSpec((tm, C), lambda i: (r0 + i, 0))
    vec = pl.BlockSpec((1, C), lambda i: (0, 0))
    return pl.pallas_call(
        _rw_post_kernel,
        grid=(n_rows // tm,),
        in_specs=[own, own, tok, tok, vec, vec, _any_spec()],
        out_specs=pl.BlockSpec((tm, C), lambda i: (r0 + i, out_blk)),
        out_shape=jax.ShapeDtypeStruct(mix.shape, mix.dtype),
        input_output_aliases={6: 0},
        compiler_params=_params("parallel"),
        name="rwkv_post",
    )(y_f, y_b, bonus, gate, gn_g.reshape(1, C), gn_b.reshape(1, C), mix)


def _pad_rows(w, rows, at):
    return jnp.zeros((rows, w.shape[1]), w.dtype).at[at:at + w.shape[0]].set(w)


def _layer_weights(l, D, w_in, rw_conv_w, rw_w2, rw_a2, rw_g2):
    C = D // 4
    o_z = 3 * (D // 2) + 3 * C
    n_z = 4 * RW_LORA_R + RW_GATE_R
    zpad = 4 * LANES - n_z
    cw = rw_conv_w[l]
    w = w_in[l, :, o_z:].astype(BF16)
    return dict(
        w_uvz=jnp.concatenate([w[:, n_z:], w[:, :n_z], jnp.zeros((D, zpad), BF16)], axis=1),
        cw_rkv=cw[:, :3 * C],
        cw_z=jnp.concatenate([cw[:, 3 * C:], jnp.zeros((CONV_W, zpad), F32)], axis=1),
        w2_pad=jnp.stack([_pad_rows(rw_w2[l, d], LANES, d * RW_LORA_R) for d in range(2)]),
        a2_pad=jnp.stack([_pad_rows(rw_a2[l, d], LANES, d * RW_LORA_R) for d in range(2)]),
        g2_pad=_pad_rows(rw_g2[l], 2 * LANES, 0),
    )


def _rope_tables(T):
    n = DA_D // 4
    inv = ROPE_THETA ** (-jnp.arange(n, dtype=F32) / n)
    rows = T // GRID_W
    row = jnp.repeat(jnp.arange(rows), GRID_W).astype(F32)
    col = jnp.tile(jnp.arange(GRID_W), rows).astype(F32)
    sign = jnp.concatenate([-jnp.ones((n,), F32), jnp.ones((n,), F32)])
    cs, sn = [], []
    for pos in (row, col):
        ang = pos[:, None] * inv[None, :]
        cs.append(jnp.concatenate([jnp.cos(ang), jnp.cos(ang)], axis=1))
        sn.append(jnp.concatenate([jnp.sin(ang), jnp.sin(ang)], axis=1) * sign[None, :])
    cos = jnp.concatenate(cs, axis=1)
    sin = jnp.concatenate(sn, axis=1)
    return jnp.tile(cos, (1, 2)), jnp.tile(sin, (1, 2))


def kernel(x_prompt, x_sample, cache_da_k, cache_da_v, state_rwkv, c, c_ctx, mod_w, mod_b, norm1_g, norm2_g, w_in, da_lambda, da_subln_g, rw_conv_w, rw_w0, rw_w2, rw_a0, rw_a2, rw_g2, rw_k_k, rw_k_a, rw_r_k, rw_gn_g, rw_gn_b, cm_norm_g, cm_ws, cm_bs, w_out, ffn_up, ffn_conv_w, ffn_conv_b, ffn_down, final_norm_g):
    Bp, Tp, D = x_prompt.shape
    Bs, Ts, _ = x_sample.shape
    L = mod_w.shape[0]
    past = cache_da_k.shape[2]
    DA = D // 2
    H = DA // (2 * DA_D)
    C = D // 4
    F = ffn_down.shape[1]
    p_tok, s_tok = Bp * Tp, Bs * Ts
    assert Bs + 1 <= SUBLANES and p_tok % Ts == 0

    xs = (x_prompt.reshape(p_tok, D), x_sample.reshape(s_tok, D))
    cond8 = jnp.concatenate([c_ctx[None, :], c, jnp.zeros((SUBLANES - 1 - Bs, D), F32)], axis=0)
    mod = _modulation(cond8, mod_w, mod_b)
    ck4 = cache_da_k.reshape(Bs, L, past, DA)
    cv4 = cache_da_v.reshape(Bs, L, past, DA)
    cos, sin = _rope_tables(Ts)
    ko = jnp.zeros((Bp, L, Tp, DA), F32)
    vo = jnp.zeros((Bp, L, Tp, DA), F32)
    ffn_down_b = ffn_down.astype(BF16)
    n_qkv_rkv = 3 * DA + 3 * C

    new_s = []
    for l in range(L):
        lw_ = _layer_weights(l, D, w_in, rw_conv_w, rw_w2, rw_a2, rw_g2)
        lp = dict(lw_, rw_w0=rw_w0[l], rw_a0=rw_a0[l], rw_k_k=rw_k_k[l], rw_k_a=rw_k_a[l], rw_r_k=rw_r_k[l])
        lam_init = 0.8 - 0.6 * math.exp(-0.3 * l)

        h = _norm_mod(xs, norm1_g[l], mod, l, 0, 1, p_tok, Ts)
        proj = _matmul(h, w_in, layer=l, n_cols=n_qkv_rkv, tm=1024, tn=512, name="proj_in")
        proj_uz = _matmul(h, lp["w_uvz"], tm=1024, tn=512, name="proj_in_uvz")
        ko, vo = _store_kv(proj, ko, vo, l, Bp=Bp, Tp=Tp, DA=DA)

        mix = jnp.zeros((p_tok + s_tok, D), BF16)
        mix = _attention(proj, mix, da_lambda[l], da_subln_g[l], lam_init, row0=0, B=Bp, T=Tp, H=H)
        mix = _attention(proj, mix, da_lambda[l], da_subln_g[l], lam_init, row0=p_tok, B=Bs, T=Ts, H=H,
                         ctx=(ck4, cv4, l, cos, sin))
        r_, kk_, v_, lg_, b_, kd_, gate_, bonus_ = _rw_prep(proj, proj_uz, lp, r_blk=3 * DA // C,
                                                            z_blk=2 * C // (4 * LANES), p_tok=p_tok, tp=Tp, ts=Ts)
        yf_p, yb_p, s_p = _rw_scan(r_, kk_, v_, lg_, b_, kd_, row0=0, B=Bp, T=Tp)
        yf_s, yb_s, _ = _rw_scan(r_, kk_, v_, lg_, b_, kd_, row0=p_tok, B=Bs, T=Ts, s0=state_rwkv, layer=l)
        mix = _rw_post(yf_p, yb_p, bonus_, gate_, mix, rw_gn_g[l], rw_gn_b[l], row0=0, out_blk=DA // C)
        mix = _rw_post(yf_s, yb_s, bonus_, gate_, mix, rw_gn_g[l], rw_gn_b[l], row0=p_tok, out_blk=DA // C)
        mix = _chunk_mlp(proj_uz, mix, cm_norm_g[l], cm_ws[l], cm_bs[l], u_blk=0, out_blk=DA // C + 1)

        x = _matmul(mix, w_out, layer=l, tm=1024, tn=512, resid=(xs, mod, l, 2 * D // 512, p_tok, Ts),
                    name="proj_out")
        xs = (x,)

        h = _norm_mod(xs, norm2_g[l], mod, l, 3, 4, p_tok, Ts)
        up = _matmul(h, ffn_up, layer=l, tm=1024, tn=512, name="ffn_up")
        act = _ffn_act(up, ffn_conv_w[l], ffn_conv_b[l], p_tok=p_tok, tp=Tp, ts=Ts)
        x = _matmul(act, ffn_down_b, layer=l, tm=1024, tn=512, tk=F // 2,
                    resid=(xs, mod, l, 5 * D // 512, p_tok, Ts), name="ffn_down")
        xs = (x,)

        new_s.append(s_p)

    y_p = _final_norm(x, final_norm_g, row0=0, n_rows=p_tok)
    y_s = _final_norm(x, final_norm_g, row0=p_tok, n_rows=s_tok)
    return (y_p.reshape(Bp, Tp, D), y_s.reshape(Bs, Ts, D), ko.reshape(Bp, L, Tp, H, 2, DA_D),
            vo.reshape(Bp, L, Tp, H, 2 * DA_D), jnp.stack(new_s, axis=1))
```

```python
import functools
import math

import jax
import jax.numpy as jnp
from jax import lax
from jax.experimental import pallas as pl
from jax.experimental.pallas import tpu as pltpu

F32 = jnp.float32
BF16 = jnp.bfloat16
HIGHEST = lax.Precision.HIGHEST

LANES = 128
SUBLANES = 8
VMEM_BYTES_V7X = 64 * 1024 * 1024
VMEM_BUDGET = VMEM_BYTES_V7X * 3 // 4

GRID_W = 64
DA_D = 128
RW_HEAD = 64
RW_LORA_R = 64
RW_GATE_R = 160
RW_GN_EPS = 64e-5
CM_GROUPS = 4
CM_CHUNK = 128
CONV_W = 3
ROPE_THETA = 10000.0
NORM_EPS = 1e-6
SCAN_CHUNK = 64

MM_TM, MM_TN = 1024, 512
ROW_TM = 256
ATT_TQ, ATT_KEYS = 256, 512


def _params(*sem):
    return pltpu.CompilerParams(dimension_semantics=sem, vmem_limit_bytes=VMEM_BUDGET)


def _any_spec():
    return pl.BlockSpec(memory_space=pl.ANY)


def _cond_row(tok0, p_tok, ts):
    return jnp.where(tok0 < p_tok, 0, 1 + jnp.maximum(tok0 - p_tok, 0) // ts)


def _mod_kernel(c_ref, w_ref, b_ref, o_ref):
    c = c_ref[...]
    s = (c * jax.nn.sigmoid(c)).astype(BF16)
    o_ref[...] = jnp.dot(s, w_ref[...].astype(BF16), preferred_element_type=F32) + b_ref[...]


def _modulation(cond8, mod_w, mod_b, tn=MM_TN):
    L, D, N = mod_w.shape
    return pl.pallas_call(
        _mod_kernel,
        grid=(L, N // tn),
        in_specs=[pl.BlockSpec((SUBLANES, D), lambda l, j: (0, 0)),
                  pl.BlockSpec((None, D, tn), lambda l, j: (l, 0, j)),
                  pl.BlockSpec((None, 1, tn), lambda l, j: (l, 0, j))],
        out_specs=pl.BlockSpec((None, SUBLANES, tn), lambda l, j: (l, 0, j)),
        out_shape=jax.ShapeDtypeStruct((L, SUBLANES, N), F32),
        compiler_params=_params("parallel", "parallel"),
        name="modulation",
    )(cond8, mod_w, mod_b.reshape(L, 1, N))


def _rows(refs, first_tiles):
    if len(refs) == 1:
        return refs[0][...]
    return jnp.where(pl.program_id(0) < first_tiles, refs[0][...], refs[1][...])


def _split_specs(xs, tm, block, idx):
    if len(xs) == 1:
        return [pl.BlockSpec(block, idx(lambda i: i))], 0
    first = xs[0].shape[0] // tm
    last = xs[1].shape[0] // tm - 1
    return [pl.BlockSpec(block, idx(lambda i: jnp.minimum(i, first - 1))),
            pl.BlockSpec(block, idx(lambda i: jnp.clip(i - first, 0, last)))], first


def _norm_mod_kernel(*refs, n_x, x_split, tm, p_tok, ts):
    x_refs, (g_ref, sh_ref, sc_ref, o_ref) = refs[:n_x], refs[n_x:]
    row = _cond_row(pl.program_id(0) * tm, p_tok, ts)
    x = _rows(x_refs, x_split)
    y = x * lax.rsqrt(jnp.mean(x * x, axis=-1, keepdims=True) + NORM_EPS)
    sc = sc_ref[pl.ds(row, 1), :]
    sh = sh_ref[pl.ds(row, 1), :]
    o_ref[...] = ((y * g_ref[...]) * (1.0 + sc) + sh).astype(o_ref.dtype)


def _norm_mod(xs, g, mod, layer, k_sh, k_sc, p_tok, ts, tm=ROW_TM):
    n = sum(x.shape[0] for x in xs)
    D = xs[0].shape[1]
    assert p_tok % tm == 0 and ts % tm == 0, "a row tile must not mix modulation groups"
    x_specs, x_split = _split_specs(xs, tm, (tm, D), lambda r: (lambda i: (r(i), 0)))
    return pl.pallas_call(
        functools.partial(_norm_mod_kernel, n_x=len(xs), x_split=x_split, tm=tm, p_tok=p_tok, ts=ts),
        grid=(n // tm,),
        in_specs=x_specs + [pl.BlockSpec((1, D), lambda i: (0, 0)),
                            pl.BlockSpec((None, SUBLANES, D), lambda i: (layer, 0, k_sh)),
                            pl.BlockSpec((None, SUBLANES, D), lambda i: (layer, 0, k_sc))],
        out_specs=pl.BlockSpec((tm, D), lambda i: (i, 0)),
        out_shape=jax.ShapeDtypeStruct((n, D), BF16),
        compiler_params=_params("parallel"),
        name="norm_mod",
    )(*xs, g.reshape(1, D), mod, mod)


def _final_norm_kernel(x_ref, g_ref, o_ref):
    x = x_ref[...]
    o_ref[...] = x * lax.rsqrt(jnp.mean(x * x, axis=-1, keepdims=True) + NORM_EPS) * g_ref[...]


def _final_norm(x, g, *, row0, n_rows, tm=ROW_TM):
    D = x.shape[1]
    r0 = row0 // tm
    return pl.pallas_call(
        _final_norm_kernel,
        grid=(n_rows // tm,),
        in_specs=[pl.BlockSpec((tm, D), lambda i: (r0 + i, 0)), pl.BlockSpec((1, D), lambda i: (0, 0))],
        out_specs=pl.BlockSpec((tm, D), lambda i: (i, 0)),
        out_shape=jax.ShapeDtypeStruct((n_rows, D), F32),
        compiler_params=_params("parallel"),
        name="final_norm",
    )(x, g.reshape(1, D))


def _mm_kernel(*refs, nk, n_x, x_split, tm, p_tok, ts):
    a_ref, b_ref = refs[:2]
    x_refs, g_ref = refs[2:2 + n_x], (refs[2 + n_x] if n_x else None)
    o_ref = refs[3 + n_x] if n_x else refs[2]
    acc_ref = refs[-1] if nk > 1 else None

    def finish(acc):
        if n_x:
            row = _cond_row(pl.program_id(0) * tm, p_tok, ts)
            o_ref[...] = _rows(x_refs, x_split) + g_ref[pl.ds(row, 1), :] * acc
        else:
            o_ref[...] = acc

    part = jnp.dot(a_ref[...], b_ref[...].astype(BF16), preferred_element_type=F32)
    if nk == 1:
        finish(part)
    else:
        k = pl.program_id(2)

        @pl.when(k == 0)
        def _():
            acc_ref[...] = part

        @pl.when((k > 0) & (k < nk - 1))
        def _():
            acc_ref[...] += part

        @pl.when(k == nk - 1)
        def _():
            finish(acc_ref[...] + part)


def _matmul(a, b, *, tm=MM_TM, tn=MM_TN, tk=None, layer=None, resid=None, name):
    M, K = a.shape
    N = b.shape[-1]
    tk = K if tk is None else tk
    nk = K // tk
    if layer is None:
        b_spec = pl.BlockSpec((tk, tn), lambda i, j, k: (k, j))
    else:
        b_spec = pl.BlockSpec((None, tk, tn), lambda i, j, k: (layer, k, j))
    in_specs = [pl.BlockSpec((tm, tk), lambda i, j, k: (i, k)), b_spec]
    args = [a, b]
    p_tok = ts = x_split = n_x = 0
    if resid is not None:
        xs, mod, mod_layer, gate_blk, p_tok, ts = resid
        assert p_tok % tm == 0 and ts % tm == 0, "a row tile must not mix modulation groups"
        x_specs, x_split = _split_specs(xs, tm, (tm, tn), lambda r: (lambda i, j, k: (r(i), j)))
        n_x = len(xs)
        in_specs += x_specs + [pl.BlockSpec((None, SUBLANES, tn), lambda i, j, k: (mod_layer, 0, gate_blk + j))]
        args += list(xs) + [mod]
    return pl.pallas_call(
        functools.partial(_mm_kernel, nk=nk, n_x=n_x, x_split=x_split, tm=tm, p_tok=p_tok, ts=ts),
        grid=(M // tm, N // tn, nk),
        in_specs=in_specs,
        out_specs=pl.BlockSpec((tm, tn), lambda i, j, k: (i, j)),
        out_shape=jax.ShapeDtypeStruct((M, N), F32),
        scratch_shapes=[pltpu.VMEM((tm, tn), F32)] if nk > 1 else [],
        compiler_params=_params("parallel", "parallel", "arbitrary"),
        name=name,
    )(*args)


def _store_kv_kernel(k_ref, v_ref, _ko_in, _vo_in, ko_ref, vo_ref):
    ko_ref[...] = k_ref[...]
    vo_ref[...] = v_ref[...]


def _store_kv(proj, ko, vo, layer, *, Bp, Tp, DA):
    out = pl.BlockSpec((None, None, Tp, DA), lambda b: (b, layer, 0, 0))
    return pl.pallas_call(
        _store_kv_kernel,
        grid=(Bp,),
        in_specs=[pl.BlockSpec((Tp, DA), lambda b: (b, 1)), pl.BlockSpec((Tp, DA), lambda b: (b, 2)),
                  _any_spec(), _any_spec()],
        out_specs=[out, out],
        out_shape=[jax.ShapeDtypeStruct(ko.shape, ko.dtype), jax.ShapeDtypeStruct(vo.shape, vo.dtype)],
        input_output_aliases={2: 0, 3: 1},
        compiler_params=_params("parallel"),
        name="store_kv",
    )(proj, proj, ko, vo)


def _rope(x, cos, sin_signed):
    lane = lax.broadcasted_iota(jnp.int32, x.shape, 1)
    width = x.shape[1]
    rot = jnp.where((lane % 64) < 32, pltpu.roll(x, width - 32, 1), pltpu.roll(x, 32, 1))
    return x * cos + rot * sin_signed


def _attn_kernel(*refs, rope, lam_init, key_block):
    if rope:
        (lam_ref, q_ref, k_ref, v_ref, kc_ref, vc_ref, cq_ref, sq_ref, ck_ref, sk_ref, g_ref, _mix_ref,
         o_ref, kr_ref, vr_ref) = refs
    else:
        lam_ref, q_ref, k_ref, v_ref, g_ref, _mix_ref, o_ref = refs
    lm = lam_ref[...]
    s1 = jnp.sum(lm[0:1] * lm[1:2], axis=-1, keepdims=True)
    s2 = jnp.sum(lm[2:3] * lm[3:4], axis=-1, keepdims=True)
    lam = jnp.exp(s1) - jnp.exp(s2) + lam_init
    qscale = DA_D ** -0.5 * math.log2(math.e)

    if rope:
        @pl.when(pl.program_id(2) == 0)
        def _():
            kr_ref[...] = _rope(k_ref[...], ck_ref[...], sk_ref[...]).astype(BF16)
            vr_ref[...] = v_ref[...].astype(BF16)

        q = (_rope(q_ref[...], cq_ref[...], sq_ref[...]) * qscale).astype(BF16)
        T = kr_ref.shape[0]
        blocks = [(kr_ref[j * key_block:(j + 1) * key_block], vr_ref[j * key_block:(j + 1) * key_block])
                  for j in range(T // key_block)]
        blocks.append((kc_ref[...].astype(BF16), vc_ref[...].astype(BF16)))
    else:
        q = (q_ref[...] * qscale).astype(BF16)
        blocks = [(k_ref[...].astype(BF16), v_ref[...].astype(BF16))]

    cols = [slice(c * DA_D, (c + 1) * DA_D) for c in range(2)]
    ss = [[_mm1(q[:, cols[c]], kb[:, cols[c]], "nt") for c in range(2)] for kb, _ in blocks]
    m = [functools.reduce(jnp.maximum, [jnp.max(s[c], axis=-1, keepdims=True) for s in ss]) for c in range(2)]
    o = None
    d = [None, None]
    tq = q.shape[0]
    for s, (_, vb) in zip(ss, blocks):
        e = [jnp.exp2(s[c] - m[c]) for c in range(2)]
        for c in range(2):
            t = jnp.sum(e[c], axis=-1, keepdims=True)
            d[c] = t if d[c] is None else d[c] + t
        u = _mm1(jnp.concatenate([e[0].astype(BF16), e[1].astype(BF16)], axis=0), vb)
        o = u if o is None else o + u
    o = [o[:tq], o[tq:]]
    maps = list(zip(o, d))
    o = maps[0][0] * (1.0 / maps[0][1]) - maps[1][0] * (lam / maps[1][1])
    y = o * lax.rsqrt(jnp.mean(o * o, axis=-1, keepdims=True) + NORM_EPS)
    o_ref[...] = (y * g_ref[...] * (1.0 - lam_init)).astype(o_ref.dtype)


def _attention(proj, mix, lam_p, g, lam_init, *, row0, B, T, H, ctx=None, tq=ATT_TQ, key_block=ATT_KEYS):
    W = 2 * DA_D
    nq = T // tq
    qb0 = row0 // tq
    kb0 = row0 // T
    rope = ctx is not None
    in_specs = [pl.BlockSpec((4, DA_D), lambda b, h, i: (0, 0)),
                pl.BlockSpec((tq, W), lambda b, h, i: (qb0 + b * nq + i, h)),
                pl.BlockSpec((T, W), lambda b, h, i: (kb0 + b, H + h)),
                pl.BlockSpec((T, W), lambda b, h, i: (kb0 + b, 2 * H + h))]
    args = [lam_p, proj, proj, proj]
    scratch = []
    if rope:
        ck, cv, layer, cos, sin = ctx
        past = ck.shape[2]
        in_specs += [pl.BlockSpec((None, None, past, W), lambda b, h, i: (b, layer, 0, h)),
                     pl.BlockSpec((None, None, past, W), lambda b, h, i: (b, layer, 0, h)),
                     pl.BlockSpec((tq, W), lambda b, h, i: (i, 0)),
                     pl.BlockSpec((tq, W), lambda b, h, i: (i, 0)),
                     pl.BlockSpec((T, W), lambda b, h, i: (0, 0)),
                     pl.BlockSpec((T, W), lambda b, h, i: (0, 0))]
        args += [ck, cv, cos, sin, cos, sin]
        scratch = [pltpu.VMEM((T, W), BF16), pltpu.VMEM((T, W), BF16)]
    in_specs += [pl.BlockSpec((1, W), lambda b, h, i: (0, 0)), _any_spec()]
    args += [g.reshape(1, W), mix]
    return pl.pallas_call(
        functools.partial(_attn_kernel, rope=rope, lam_init=lam_init, key_block=key_block),
        grid=(B, H, nq),
        in_specs=in_specs,
        out_specs=pl.BlockSpec((tq, W), lambda b, h, i: (qb0 + b * nq + i, h)),
        out_shape=jax.ShapeDtypeStruct(mix.shape, mix.dtype),
        input_output_aliases={len(args) - 1: 0},
        scratch_shapes=scratch,
        compiler_params=_params("parallel", "parallel", "arbitrary"),
        name="diff_attn_ctx" if rope else "diff_attn",
    )(*args)


def _tile_edges(tok0, tm, p_tok, tp, ts):
    in_p = tok0 < p_tok
    pos = jnp.where(in_p, tok0 % tp, jnp.maximum(tok0 - p_tok, 0) % ts)
    length = jnp.where(in_p, tp, ts)
    return (pos != 0).astype(F32), (pos + tm != length).astype(F32)


def _conv3(x_ref, p_ref, n_ref, s_ref, w, keep_prev, keep_next, cols=slice(None)):
    tm = x_ref.shape[0]
    s_ref[SUBLANES:SUBLANES + tm, :] = x_ref[:, cols]
    s_ref[SUBLANES - 1:SUBLANES, :] = p_ref[SUBLANES - 1:SUBLANES, cols] * keep_prev
    s_ref[SUBLANES + tm:SUBLANES + tm + 1, :] = n_ref[0:1, cols] * keep_next
    return (s_ref[SUBLANES - 1:SUBLANES - 1 + tm, :] * w[0:1] + x_ref[:, cols] * w[1:2]
            + s_ref[SUBLANES + 1:SUBLANES + 1 + tm, :] * w[2:3])


def _halo_specs(tm, width, col_blk, n_tok, nidx=1):
    r = tm // SUBLANES
    last = n_tok // SUBLANES - 1
    if nidx == 1:
        return [pl.BlockSpec((tm, width), lambda i: (i, col_blk)),
                pl.BlockSpec((SUBLANES, width), lambda i: (jnp.maximum(i * r - 1, 0), col_blk)),
                pl.BlockSpec((SUBLANES, width), lambda i: (jnp.minimum((i + 1) * r, last), col_blk))]
    return [pl.BlockSpec((tm, width), lambda i, j: (i, col_blk + j)),
            pl.BlockSpec((SUBLANES, width), lambda i, j: (jnp.maximum(i * r - 1, 0), col_blk + j)),
            pl.BlockSpec((SUBLANES, width), lambda i, j: (jnp.minimum((i + 1) * r, last), col_blk + j))]


def _cmlp_kernel(u_ref, v_ref, gain_ref, ws_ref, bs_ref, _mix_ref, o_ref, *, tm):
    v = v_ref[...]
    z = (v * lax.rsqrt(jnp.mean(v * v, axis=-1, keepdims=True) + NORM_EPS) * gain_ref[...]).astype(BF16)
    gw = z.shape[1] // CM_GROUPS
    for n in range(tm // CM_CHUNK):
        rows = slice(n * CM_CHUNK, (n + 1) * CM_CHUNK)
        for g in range(CM_GROUPS):
            cols = slice(g * gw, (g + 1) * gw)
            t = jnp.dot(ws_ref[g].astype(BF16), z[rows, cols], preferred_element_type=F32) + bs_ref[g]
            o_ref[rows, cols] = (u_ref[rows, cols] * t).astype(o_ref.dtype)


def _chunk_mlp(proj, mix, gain, ws, bs, *, u_blk, out_blk, tm=ROW_TM):
    n_tok = proj.shape[0]
    W = gain.shape[0]
    gw = W // CM_GROUPS
    bs_b = jnp.broadcast_to(bs[:, :, None], (CM_GROUPS, CM_CHUNK, gw))
    return pl.pallas_call(
        functools.partial(_cmlp_kernel, tm=tm),
        grid=(n_tok // tm,),
        in_specs=[pl.BlockSpec((tm, W), lambda i: (i, u_blk)),
                  pl.BlockSpec((tm, W), lambda i: (i, u_blk + 1)),
                  pl.BlockSpec((1, W), lambda i: (0, 0)),
                  pl.BlockSpec((CM_GROUPS, CM_CHUNK, CM_CHUNK), lambda i: (0, 0, 0)),
                  pl.BlockSpec((CM_GROUPS, CM_CHUNK, gw), lambda i: (0, 0, 0)),
                  _any_spec()],
        out_specs=pl.BlockSpec((tm, W), lambda i: (i, out_blk)),
        out_shape=jax.ShapeDtypeStruct(mix.shape, mix.dtype),
        input_output_aliases={5: 0},
        compiler_params=_params("parallel"),
        name="chunk_mlp",
    )(proj, proj, gain.reshape(1, W), ws, bs_b, mix)


def _ffn_act_kernel(a_ref, ap_ref, an_ref, b_ref, bp_ref, bn_ref, wa_ref, wb_ref, ba_ref, bb_ref, o_ref,
                    sa_ref, sb_ref, *, tm, tc, unroll, p_tok, tp, ts):
    keep_prev, keep_next = _tile_edges(pl.program_id(0) * tm, tm, p_tok, tp, ts)

    def chunk(j, slab):
        cols = pl.ds(pl.multiple_of(j * LANES, LANES), LANES)
        ga = _conv3(a_ref, ap_ref, an_ref, sa_ref.at[slab], wa_ref[:, cols], keep_prev, keep_next, cols)
        gb = _conv3(b_ref, bp_ref, bn_ref, sb_ref.at[slab], wb_ref[:, cols], keep_prev, keep_next, cols)
        ga = ga + ba_ref[:, cols]
        o_ref[:, cols] = (ga * jax.nn.sigmoid(ga) * (gb + bb_ref[:, cols])).astype(o_ref.dtype)

    n_chunks = tc // LANES

    def body(g, carry):
        for u in range(unroll):
            chunk(g * unroll + u, u)
        return carry

    lax.fori_loop(0, n_chunks // unroll, body, 0)
    for u in range(n_chunks % unroll):
        chunk(n_chunks - n_chunks % unroll + u, u)


def _ffn_act(up, conv_w, conv_b, *, p_tok, tp, ts, tm=ROW_TM, col_blocks=2):
    n_tok, two_f = up.shape
    F = two_f // 2
    tc = F // col_blocks
    assert tp % tm == 0 and ts % tm == 0 and tc % LANES == 0
    cb = conv_b.reshape(1, two_f)
    wspec = lambda off: pl.BlockSpec((CONV_W, tc), lambda i, j: (0, off + j))
    bspec = lambda off: pl.BlockSpec((1, tc), lambda i, j: (0, off + j))
    unroll = 4
    halo = pltpu.VMEM((unroll, tm + 2 * SUBLANES, LANES), F32)
    return pl.pallas_call(
        functools.partial(_ffn_act_kernel, tm=tm, tc=tc, unroll=unroll, p_tok=p_tok, tp=tp, ts=ts),
        grid=(n_tok // tm, col_blocks),
        in_specs=(_halo_specs(tm, tc, 0, n_tok, 2) + _halo_specs(tm, tc, col_blocks, n_tok, 2)
                  + [wspec(0), wspec(col_blocks), bspec(0), bspec(col_blocks)]),
        out_specs=pl.BlockSpec((tm, tc), lambda i, j: (i, j)),
        out_shape=jax.ShapeDtypeStruct((n_tok, F), BF16),
        scratch_shapes=[halo, halo],
        compiler_params=_params("parallel", "parallel"),
        name="ffn_act",
    )(up, up, up, up, up, up, conv_w, conv_w, cb, cb)


_DIMS = {"nn": ((1,), (0,)), "nt": ((1,), (1,)), "tn": ((0,), (0,))}


def _mm1(a, b, kind="nn"):
    return lax.dot_general(a, b, (_DIMS[kind], ((), ())), preferred_element_type=F32)


def _mmh(a, b, dims=(((1,), (0,)), ((), ()))):
    return lax.dot_general(a, b, dims, precision=HIGHEST, preferred_element_type=F32)


def _split3(x):
    hi = x.astype(BF16)
    r1 = x - hi.astype(F32)
    mid = r1.astype(BF16)
    return hi, mid, (r1 - mid.astype(F32)).astype(BF16)


def _head_sum(x):
    r = lax.broadcasted_iota(jnp.int32, (3 * LANES, LANES), 0) % LANES // RW_HEAD
    c = lax.broadcasted_iota(jnp.int32, (3 * LANES, LANES), 1) // RW_HEAD
    e3 = jnp.where(r == c, 1.0, 0.0).astype(BF16)
    parts = _split3(x)
    cols = [_mm1(jnp.concatenate([p[:, j * LANES:(j + 1) * LANES] for p in parts], axis=1), e3)
            for j in range(x.shape[1] // LANES)]
    return jnp.concatenate(cols, axis=1)


def _cumsum_rows(tri, x):
    t = tri.astype(BF16)
    return _mm1(jnp.concatenate([t, t, t], axis=1), jnp.concatenate(_split3(x), axis=0))


def _rw_prep_kernel(r_ref, rp_ref, rn_ref, k_ref, kp_ref, kn_ref, v_ref, vp_ref, vn_ref,
                    z_ref, zp_ref, zn_ref, cw_ref, cz_ref, w0_ref, w2_ref, a0_ref, a2_ref, g2_ref,
                    kk_ref, ka_ref, rk_ref,
                    ro_ref, kko_ref, vo_ref, lw_ref, b_ref, kd_ref, gate_ref, bonus_ref,
                    sr_ref, sk_ref, sv_ref, sz_ref, *, tm, p_tok, tp, ts, C):
    edges = _tile_edges(pl.program_id(0) * tm, tm, p_tok, tp, ts)

    def conv(x_ref, p_ref, n_ref, s_ref, w):
        chunks = []
        for j in range(x_ref.shape[1] // LANES):
            cols = slice(j * LANES, (j + 1) * LANES)
            chunks.append(_conv3(x_ref, p_ref, n_ref, s_ref.at[j], w[:, cols], *edges, cols))
        return jnp.concatenate(chunks, axis=1)

    cw = cw_ref[...]
    r = conv(r_ref, rp_ref, rn_ref, sr_ref, cw[:, 0:C])
    k = conv(k_ref, kp_ref, kn_ref, sk_ref, cw[:, C:2 * C])
    v = conv(v_ref, vp_ref, vn_ref, sv_ref, cw[:, 2 * C:3 * C])
    z = conv(z_ref, zp_ref, zn_ref, sz_ref, cz_ref[...])
    dec = jnp.tanh(z[:, 0:LANES]).astype(BF16)
    aa = z[:, LANES:2 * LANES].astype(BF16)
    gl = jax.nn.sigmoid(z[:, 2 * LANES:4 * LANES]).astype(BF16)
    gate_ref[...] = jnp.dot(gl, g2_ref[...].astype(BF16), preferred_element_type=F32)
    kk = k * kk_ref[...]
    kk = kk * lax.rsqrt(_head_sum(kk * kk) + 1e-12)
    ro_ref[...] = r
    kko_ref[...] = kk
    vo_ref[...] = v
    kd_sum = None
    for d in range(2):
        wl = w0_ref[d:d + 1, :] + jnp.dot(dec, w2_ref[d].astype(BF16), preferred_element_type=F32)
        lw_ref[d] = -math.exp(-0.5) * jax.nn.sigmoid(wl)
        a = jax.nn.sigmoid(a0_ref[d:d + 1, :] + jnp.dot(aa, a2_ref[d].astype(BF16), preferred_element_type=F32))
        b_ref[d] = kk * a
        kd = k * (1.0 + (a - 1.0) * ka_ref[...])
        kd_ref[d] = kd
        kd_sum = kd if kd_sum is None else kd_sum + kd
    bonus_ref[...] = _head_sum(r * kd_sum * rk_ref[...]) * v


def _rw_prep(proj, proj_z, lp, *, r_blk, z_blk, p_tok, tp, ts, tm=ROW_TM):
    assert tp % tm == 0 and ts % tm == 0
    n_tok = proj.shape[0]
    C = lp["rw_k_k"].shape[0]
    ZW = 4 * LANES
    full = lambda shape: pl.BlockSpec(shape, lambda i: (0,) * len(shape))
    tok = pl.BlockSpec((tm, C), lambda i: (i, 0))
    tok2 = pl.BlockSpec((2, tm, C), lambda i: (0, i, 0))
    one = jax.ShapeDtypeStruct((n_tok, C), F32)
    two = jax.ShapeDtypeStruct((2, n_tok, C), F32)
    return pl.pallas_call(
        functools.partial(_rw_prep_kernel, tm=tm, p_tok=p_tok, tp=tp, ts=ts, C=C),
        grid=(n_tok // tm,),
        in_specs=(_halo_specs(tm, C, r_blk, n_tok) + _halo_specs(tm, C, r_blk + 1, n_tok)
                  + _halo_specs(tm, C, r_blk + 2, n_tok) + _halo_specs(tm, ZW, z_blk, n_tok)
                  + [full((CONV_W, 3 * C)), full((CONV_W, ZW)), full((2, C)), full((2, LANES, C)),
                     full((2, C)), full((2, LANES, C)), full((2 * LANES, C)),
                     full((1, C)), full((1, C)), full((1, C))]),
        out_specs=[tok, tok, tok, tok2, tok2, tok2, tok, tok],
        out_shape=[one, one, one, two, two, two, one, one],
        scratch_shapes=[pltpu.VMEM((w // LANES, tm + 2 * SUBLANES, LANES), F32) for w in (C, C, C, ZW)],
        compiler_params=_params("parallel"),
        name="rwkv_prep",
    )(*([proj] * 9 + [proj_z] * 3), lp["cw_rkv"], lp["cw_z"], lp["rw_w0"], lp["w2_pad"], lp["rw_a0"], lp["a2_pad"],
      lp["g2_pad"], lp["rw_k_k"].reshape(1, C), lp["rw_k_a"].reshape(1, C), lp["rw_r_k"].reshape(1, C))


def _rw_scan_kernel(*refs, has_s0, nc):
    if has_s0:
        s0_ref, refs = refs[0], refs[1:]
    (rf_ref, kkf_ref, vf_ref, rb_ref, kkb_ref, vb_ref, lwf_ref, bf_ref, kdf_ref, lwb_ref, bb_ref, kdb_ref,
     yf_ref, yb_ref, so_ref, st_ref) = refs
    C, N = SCAN_CHUNK, RW_HEAD
    W = 2 * N
    npair = rf_ref.shape[1] // W
    c = pl.program_id(1)
    zero = jnp.zeros((), BF16)

    row = lax.broadcasted_iota(jnp.int32, (W, W), 0)
    col = lax.broadcasted_iota(jnp.int32, (W, W), 1)
    same_head = (row // N) == (col // N)
    eye = (row == col).astype(F32)

    @pl.when(c == 0)
    def _():
        if has_s0:
            sel = (lax.broadcasted_iota(jnp.int32, (N, W), 0) == lax.broadcasted_iota(jnp.int32, (N, W), 1) % N)
            for d in range(2):
                for p in range(npair):
                    tiled = _mmh(s0_ref[d, 2 * p:2 * p + 2].reshape(W, N), sel.astype(F32))
                    st_ref[d * npair + p] = jnp.where(same_head, tiled, 0.0)
        else:
            st_ref[...] = jnp.zeros(st_ref.shape, F32)

    def order(shape, dim, bwd):
        t = lax.broadcasted_iota(jnp.int32, shape, dim) % C
        return C - 1 - t if bwd else t

    chains = []
    for d, (r_ref, kk_ref, v_ref, lw_ref, b_ref, kd_ref, y_ref) in enumerate(
            [(rf_ref, kkf_ref, vf_ref, lwf_ref, bf_ref, kdf_ref, yf_ref),
             (rb_ref, kkb_ref, vb_ref, lwb_ref, bb_ref, kdb_ref, yb_ref)]):
        rt, ct = order((W, W), 0, d == 1), order((W, W), 1, d == 1)
        masks = dict(strict=rt > ct, incl=rt >= ct)
        masks.update({s: ((rt // (2 * s)) == (ct // (2 * s))) & ((rt // s) % 2 == 1) & ((ct // s) % 2 == 0)
                      for s in (1, 2, 4, 8, 16, 32)})
        lw = lw_ref[...]
        g_in = _cumsum_rows(order((C, C), 0, d == 1) >= order((C, C), 1, d == 1), lw)
        g_tot = jnp.sum(lw, axis=0, keepdims=True)
        e_neg = jnp.exp(-g_in)
        e_rem = jnp.exp(g_tot - g_in)
        tok = dict(
            kk=(kk_ref[...] * jnp.exp(g_in - lw)).astype(BF16), r=(r_ref[...] * jnp.exp(g_in)).astype(BF16),
            b=(b_ref[...] * e_neg).astype(BF16), kd=(kd_ref[...] * e_neg).astype(BF16),
            b_end=(b_ref[...] * e_rem).astype(BF16), kd_end=(kd_ref[...] * e_rem).astype(BF16),
            v=v_ref[...].astype(BF16), e_tot=jnp.exp(g_tot))
        for p in range(npair):
            chains.append((d * npair + p, slice(p * W, (p + 1) * W), tok, masks, y_ref))

    def expand(x, sl):
        return jnp.where(same_head, jnp.concatenate([x[:, sl]] * 2, axis=0), zero)

    ak = [expand(t["kk"], sl) for _, sl, t, _, _ in chains]
    bk = [jnp.concatenate([expand(t["b"], sl), expand(t["kd"], sl)], axis=0) for _, sl, t, _, _ in chains]
    vb = [expand(t["v"], sl) for _, sl, t, _, _ in chains]
    S = [st_ref[i] for i, *_ in chains]
    Sb = [x.astype(BF16) for x in S]
    n = range(len(chains))
    mk = [ch[3] for ch in chains]
    ar = [expand(t["r"], sl) for _, sl, t, _, _ in chains]
    lmn = [_mm1(jnp.concatenate([ak[i], ar[i]], axis=0), bk[i], "nt") for i in n]
    L = [jnp.where(mk[i]["strict"], lmn[i][:W, :W], 0.0) for i in n]
    Lb = [x.astype(BF16) for x in L]
    M = [jnp.where(mk[i]["strict"], lmn[i][:W, W:], 0.0).astype(BF16) for i in n]
    nbk = [jnp.concatenate([jnp.where(mk[i]["incl"], -lmn[i][W:, :W], 0.0),
                            jnp.where(mk[i]["incl"], lmn[i][W:, W:], 0.0)], axis=1).astype(BF16) for i in n]
    rhs = [(_mm1(ak[i], Sb[i], "nt") + _mm1(M[i], vb[i])).astype(BF16) for i in n]
    X = [eye - jnp.where(mk[i][1], L[i], 0.0) for i in n]
    for s in (2, 4, 8, 16, 32):
        Xb = [x.astype(BF16) for x in X]
        t = [_mm1(jnp.where(mk[i][s], Lb[i], zero), Xb[i]).astype(BF16) for i in n]
        X = [X[i] - _mm1(Xb[i], t[i]) for i in n]
    uv = [jnp.concatenate([_mm1(X[i].astype(BF16), rhs[i]).astype(BF16), vb[i]], axis=0) for i in n]
    for i, (slot, sl, tk, _, _) in enumerate(chains):
        ends = jnp.concatenate([-expand(tk["b_end"], sl), expand(tk["kd_end"], sl)], axis=0)
        st_ref[slot] = S[i] * tk["e_tot"][:, sl] + _mm1(uv[i], ends, "tn")
    for i, (_, sl, _, _, y_ref) in enumerate(chains):
        y = _mm1(ar[i], Sb[i], "nt") + _mm1(nbk[i], uv[i])
        y_ref[:, sl] = y[:C] + y[C:]

    @pl.when(c == nc - 1)
    def _():
        fold = (lax.broadcasted_iota(jnp.int32, (W, N), 0) % N == lax.broadcasted_iota(jnp.int32, (W, N), 1))
        for d in range(2):
            for p in range(npair):
                so_ref[d, 2 * p:2 * p + 2] = _mmh(st_ref[d * npair + p], fold.astype(F32)).reshape(2, N, N)


def _rw_scan(r, kk, v, lw, b, kd, *, row0, B, T, s0=None, layer=0):
    C = r.shape[1]
    H = C // RW_HEAD
    CH = SCAN_CHUNK
    nc = T // CH
    blk0 = row0 // CH
    fwd = pl.BlockSpec((CH, C), lambda bi, c: (blk0 + bi * nc + c, 0))
    bwd = pl.BlockSpec((CH, C), lambda bi, c: (blk0 + bi * nc + nc - 1 - c, 0))
    fwd2 = pl.BlockSpec((None, CH, C), lambda bi, c: (0, blk0 + bi * nc + c, 0))
    bwd2 = pl.BlockSpec((None, CH, C), lambda bi, c: (1, blk0 + bi * nc + nc - 1 - c, 0))
    in_specs = [fwd, fwd, fwd, bwd, bwd, bwd, fwd2, fwd2, fwd2, bwd2, bwd2, bwd2]
    args = [r, kk, v, r, kk, v, lw, b, kd, lw, b, kd]
    if s0 is not None:
        in_specs = [pl.BlockSpec((None, None, 2, H, RW_HEAD, RW_HEAD),
                                 lambda bi, c: (bi, layer, 0, 0, 0, 0))] + in_specs
        args = [s0] + args
    y_shape = jax.ShapeDtypeStruct((B * T, C), F32)
    return pl.pallas_call(
        functools.partial(_rw_scan_kernel, has_s0=s0 is not None, nc=nc),
        grid=(B, nc),
        in_specs=in_specs,
        out_specs=[pl.BlockSpec((CH, C), lambda bi, c: (bi * nc + c, 0)),
                   pl.BlockSpec((CH, C), lambda bi, c: (bi * nc + nc - 1 - c, 0)),
                   pl.BlockSpec((None, 2, H, RW_HEAD, RW_HEAD), lambda bi, c: (bi, 0, 0, 0, 0))],
        out_shape=[y_shape, y_shape, jax.ShapeDtypeStruct((B, 2, H, RW_HEAD, RW_HEAD), F32)],
        scratch_shapes=[pltpu.VMEM((H, 2 * RW_HEAD, 2 * RW_HEAD), F32)],
        compiler_params=_params("parallel", "arbitrary"),
        name="rwkv_scan_ctx" if s0 is not None else "rwkv_scan",
    )(*args)


def _rw_post_kernel(yf_ref, yb_ref, bonus_ref, gate_ref, g_ref, b_ref, _mix_ref, o_ref):
    y = yf_ref[...] + yb_ref[...]
    inv_n = 1.0 / RW_HEAD
    mu = _head_sum(y) * inv_n
    yc = y - mu
    var = _head_sum(yc * yc) * inv_n
    yn = yc * lax.rsqrt(var + RW_GN_EPS) * g_ref[...] + b_ref[...]
    o_ref[...] = ((yn + bonus_ref[...]) * gate_ref[...]).astype(o_ref.dtype)


def _rw_post(y_f, y_b, bonus, gate, mix, gn_g, gn_b, *, row0, out_blk, tm=ROW_TM):
    n_rows, C = y_f.shape
    r0 = row0 // tm
    own = pl.BlockSpec((tm, C), lambda i: (i, 0))
    tok = pl.BlockSpec((tm, C), lambda i: (r0 + i, 0))
    vec = pl.BlockSpec((1, C), lambda i: (0, 0))
    return pl.pallas_call(
        _rw_post_kernel,
        grid=(n_rows // tm,),
        in_specs=[own, own, tok, tok, vec, vec, _any_spec()],
        out_specs=pl.BlockSpec((tm, C), lambda i: (r0 + i, out_blk)),
        out_shape=jax.ShapeDtypeStruct(mix.shape, mix.dtype),
        input_output_aliases={6: 0},
        compiler_params=_params("parallel"),
        name="rwkv_post",
    )(y_f, y_b, bonus, gate, gn_g.reshape(1, C), gn_b.reshape(1, C), mix)


def _pad_rows(w, rows, at):
    return jnp.zeros((rows, w.shape[1]), w.dtype).at[at:at + w.shape[0]].set(w)


def _layer_weights(l, D, w_in, rw_conv_w, rw_w2, rw_a2, rw_g2):
    C = D // 4
    o_z = 3 * (D // 2) + 3 * C
    n_z = 4 * RW_LORA_R + RW_GATE_R
    zpad = 4 * LANES - n_z
    cw = rw_conv_w[l]
    w = w_in[l]
    return dict(
        w_uvz=jnp.concatenate([w[:, o_z + n_z:], w[:, o_z:o_z + n_z], jnp.zeros((D, zpad), F32)], axis=1),
        cw_rkv=cw[:, :3 * C],
        cw_z=jnp.concatenate([cw[:, 3 * C:], jnp.zeros((CONV_W, zpad), F32)], axis=1),
        w2_pad=jnp.stack([_pad_rows(rw_w2[l, d], LANES, d * RW_LORA_R) for d in range(2)]),
        a2_pad=jnp.stack([_pad_rows(rw_a2[l, d], LANES, d * RW_LORA_R) for d in range(2)]),
        g2_pad=_pad_rows(rw_g2[l], 2 * LANES, 0),
    )


def _rope_tables(T):
    n = DA_D // 4
    inv = ROPE_THETA ** (-jnp.arange(n, dtype=F32) / n)
    rows = T // GRID_W
    row = jnp.repeat(jnp.arange(rows), GRID_W).astype(F32)
    col = jnp.tile(jnp.arange(GRID_W), rows).astype(F32)
    sign = jnp.concatenate([-jnp.ones((n,), F32), jnp.ones((n,), F32)])
    cs, sn = [], []
    for pos in (row, col):
        ang = pos[:, None] * inv[None, :]
        cs.append(jnp.concatenate([jnp.cos(ang), jnp.cos(ang)], axis=1))
        sn.append(jnp.concatenate([jnp.sin(ang), jnp.sin(ang)], axis=1) * sign[None, :])
    cos = jnp.concatenate(cs, axis=1)
    sin = jnp.concatenate(sn, axis=1)
    return jnp.tile(cos, (1, 2)), jnp.tile(sin, (1, 2))


def kernel(x_prompt, x_sample, cache_da_k, cache_da_v, state_rwkv, c, c_ctx, mod_w, mod_b, norm1_g, norm2_g, w_in, da_lambda, da_subln_g, rw_conv_w, rw_w0, rw_w2, rw_a0, rw_a2, rw_g2, rw_k_k, rw_k_a, rw_r_k, rw_gn_g, rw_gn_b, cm_norm_g, cm_ws, cm_bs, w_out, ffn_up, ffn_conv_w, ffn_conv_b, ffn_down, final_norm_g):
    Bp, Tp, D = x_prompt.shape
    Bs, Ts, _ = x_sample.shape
    L = mod_w.shape[0]
    past = cache_da_k.shape[2]
    DA = D // 2
    H = DA // (2 * DA_D)
    C = D // 4
    F = ffn_down.shape[1]
    p_tok, s_tok = Bp * Tp, Bs * Ts
    assert Bs + 1 <= SUBLANES and p_tok % Ts == 0

    xs = (x_prompt.reshape(p_tok, D), x_sample.reshape(s_tok, D))
    cond8 = jnp.concatenate([c_ctx[None, :], c, jnp.zeros((SUBLANES - 1 - Bs, D), F32)], axis=0)
    mod = _modulation(cond8, mod_w, mod_b)
    ck4 = cache_da_k.reshape(Bs, L, past, DA)
    cv4 = cache_da_v.reshape(Bs, L, past, DA)
    cos, sin = _rope_tables(Ts)
    ko = jnp.zeros((Bp, L, Tp, DA), F32)
    vo = jnp.zeros((Bp, L, Tp, DA), F32)
    ffn_down_b = ffn_down.astype(BF16)
    w_qkv_rkv = w_in[:, :, :3 * DA + 3 * C]

    new_s = []
    for l in range(L):
        lw_ = _layer_weights(l, D, w_in, rw_conv_w, rw_w2, rw_a2, rw_g2)
        lp = dict(lw_, rw_w0=rw_w0[l], rw_a0=rw_a0[l], rw_k_k=rw_k_k[l], rw_k_a=rw_k_a[l], rw_r_k=rw_r_k[l])
        lam_init = 0.8 - 0.6 * math.exp(-0.3 * l)

        h = _norm_mod(xs, norm1_g[l], mod, l, 0, 1, p_tok, Ts)
        proj = _matmul(h, w_qkv_rkv, layer=l, name="proj_in")
        proj_uz = _matmul(h, lp["w_uvz"], name="proj_in_uvz")
        ko, vo = _store_kv(proj, ko, vo, l, Bp=Bp, Tp=Tp, DA=DA)

        mix = jnp.zeros((p_tok + s_tok, D), BF16)
        mix = _attention(proj, mix, da_lambda[l], da_subln_g[l], lam_init, row0=0, B=Bp, T=Tp, H=H)
        mix = _attention(proj, mix, da_lambda[l], da_subln_g[l], lam_init, row0=p_tok, B=Bs, T=Ts, H=H,
                         ctx=(ck4, cv4, l, cos, sin))
        r_, kk_, v_, lg_, b_, kd_, gate_, bonus_ = _rw_prep(proj, proj_uz, lp, r_blk=3 * DA // C,
                                                            z_blk=2 * C // (4 * LANES), p_tok=p_tok, tp=Tp, ts=Ts)
        yf_p, yb_p, s_p = _rw_scan(r_, kk_, v_, lg_, b_, kd_, row0=0, B=Bp, T=Tp)
        yf_s, yb_s, _ = _rw_scan(r_, kk_, v_, lg_, b_, kd_, row0=p_tok, B=Bs, T=Ts, s0=state_rwkv, layer=l)
        mix = _rw_post(yf_p, yb_p, bonus_, gate_, mix, rw_gn_g[l], rw_gn_b[l], row0=0, out_blk=DA // C)
        mix = _rw_post(yf_s, yb_s, bonus_, gate_, mix, rw_gn_g[l], rw_gn_b[l], row0=p_tok, out_blk=DA // C)
        mix = _chunk_mlp(proj_uz, mix, cm_norm_g[l], cm_ws[l], cm_bs[l], u_blk=0, out_blk=DA // C + 1)

        x = _matmul(mix, w_out, layer=l, resid=(xs, mod, l, 2 * D // MM_TN, p_tok, Ts), name="proj_out")
        xs = (x,)

        h = _norm_mod(xs, norm2_g[l], mod, l, 3, 4, p_tok, Ts)
        up = _matmul(h, ffn_up, layer=l, name="ffn_up")
        act = _ffn_act(up, ffn_conv_w[l], ffn_conv_b[l], p_tok=p_tok, tp=Tp, ts=Ts)
        x = _matmul(act, ffn_down_b, layer=l, tk=F // 2, resid=(xs, mod, l, 5 * D // MM_TN, p_tok, Ts),
                    name="ffn_down")
        xs = (x,)

        new_s.append(s_p)

    y_p = _final_norm(x, final_norm_g, row0=0, n_rows=p_tok)
    y_s = _final_norm(x, final_norm_g, row0=p_tok, n_rows=s_tok)
    return (y_p.reshape(Bp, Tp, D), y_s.reshape(Bs, Ts, D), ko.reshape(Bp, L, Tp, H, 2, DA_D),
            vo.reshape(Bp, L, Tp, H, 2 * DA_D), jnp.stack(new_s, axis=1))
```

```python
import functools
import math

import jax
import jax.numpy as jnp
from jax import lax
from jax.experimental import pallas as pl
from jax.experimental.pallas import tpu as pltpu

F32 = jnp.float32
BF16 = jnp.bfloat16
HIGHEST = lax.Precision.HIGHEST

LANES = 128
SUBLANES = 8
VMEM_BYTES_V7X = 64 * 1024 * 1024
VMEM_BUDGET = VMEM_BYTES_V7X * 3 // 4

GRID_W = 64
DA_D = 128
RW_HEAD = 64
RW_LORA_R = 64
RW_GATE_R = 160
RW_GN_EPS = 64e-5
CM_GROUPS = 4
CM_CHUNK = 128
CONV_W = 3
ROPE_THETA = 10000.0
NORM_EPS = 1e-6
SCAN_CHUNK = 64

MM_TM, MM_TN = 1024, 512
ROW_TM = 256
ATT_TQ, ATT_KEYS = 256, 512


def _params(*sem):
    return pltpu.CompilerParams(dimension_semantics=sem, vmem_limit_bytes=VMEM_BUDGET)


def _any_spec():
    return pl.BlockSpec(memory_space=pl.ANY)


def _cond_row(tok0, p_tok, ts):
    return jnp.where(tok0 < p_tok, 0, 1 + jnp.maximum(tok0 - p_tok, 0) // ts)


def _mod_kernel(c_ref, w_ref, b_ref, o_ref):
    c = c_ref[...]
    s = (c * jax.nn.sigmoid(c)).astype(BF16)
    o_ref[...] = jnp.dot(s, w_ref[...].astype(BF16), preferred_element_type=F32) + b_ref[...]


def _modulation(cond8, mod_w, mod_b, tn=MM_TN):
    L, D, N = mod_w.shape
    return pl.pallas_call(
        _mod_kernel,
        grid=(L, N // tn),
        in_specs=[pl.BlockSpec((SUBLANES, D), lambda l, j: (0, 0)),
                  pl.BlockSpec((None, D, tn), lambda l, j: (l, 0, j)),
                  pl.BlockSpec((None, 1, tn), lambda l, j: (l, 0, j))],
        out_specs=pl.BlockSpec((None, SUBLANES, tn), lambda l, j: (l, 0, j)),
        out_shape=jax.ShapeDtypeStruct((L, SUBLANES, N), F32),
        compiler_params=_params("parallel", "parallel"),
        name="modulation",
    )(cond8, mod_w, mod_b.reshape(L, 1, N))


def _rows(refs, first_tiles):
    if len(refs) == 1:
        return refs[0][...]
    return jnp.where(pl.program_id(0) < first_tiles, refs[0][...], refs[1][...])


def _split_specs(xs, tm, block, idx):
    if len(xs) == 1:
        return [pl.BlockSpec(block, idx(lambda i: i))], 0
    first = xs[0].shape[0] // tm
    last = xs[1].shape[0] // tm - 1
    return [pl.BlockSpec(block, idx(lambda i: jnp.minimum(i, first - 1))),
            pl.BlockSpec(block, idx(lambda i: jnp.clip(i - first, 0, last)))], first


def _norm_mod_kernel(*refs, n_x, x_split, tm, p_tok, ts):
    x_refs, (g_ref, sh_ref, sc_ref, o_ref) = refs[:n_x], refs[n_x:]
    row = _cond_row(pl.program_id(0) * tm, p_tok, ts)
    x = _rows(x_refs, x_split)
    y = x * lax.rsqrt(jnp.mean(x * x, axis=-1, keepdims=True) + NORM_EPS)
    sc = sc_ref[pl.ds(row, 1), :]
    sh = sh_ref[pl.ds(row, 1), :]
    o_ref[...] = ((y * g_ref[...]) * (1.0 + sc) + sh).astype(o_ref.dtype)


def _norm_mod(xs, g, mod, layer, k_sh, k_sc, p_tok, ts, tm=ROW_TM):
    n = sum(x.shape[0] for x in xs)
    D = xs[0].shape[1]
    assert p_tok % tm == 0 and ts % tm == 0, "a row tile must not mix modulation groups"
    x_specs, x_split = _split_specs(xs, tm, (tm, D), lambda r: (lambda i: (r(i), 0)))
    return pl.pallas_call(
        functools.partial(_norm_mod_kernel, n_x=len(xs), x_split=x_split, tm=tm, p_tok=p_tok, ts=ts),
        grid=(n // tm,),
        in_specs=x_specs + [pl.BlockSpec((1, D), lambda i: (0, 0)),
                            pl.BlockSpec((None, SUBLANES, D), lambda i: (layer, 0, k_sh)),
                            pl.BlockSpec((None, SUBLANES, D), lambda i: (layer, 0, k_sc))],
        out_specs=pl.BlockSpec((tm, D), lambda i: (i, 0)),
        out_shape=jax.ShapeDtypeStruct((n, D), BF16),
        compiler_params=_params("parallel"),
        name="norm_mod",
    )(*xs, g.reshape(1, D), mod, mod)


def _final_norm_kernel(x_ref, g_ref, o_ref):
    x = x_ref[...]
    o_ref[...] = x * lax.rsqrt(jnp.mean(x * x, axis=-1, keepdims=True) + NORM_EPS) * g_ref[...]


def _final_norm(x, g, *, row0, n_rows, tm=ROW_TM):
    D = x.shape[1]
    r0 = row0 // tm
    return pl.pallas_call(
        _final_norm_kernel,
        grid=(n_rows // tm,),
        in_specs=[pl.BlockSpec((tm, D), lambda i: (r0 + i, 0)), pl.BlockSpec((1, D), lambda i: (0, 0))],
        out_specs=pl.BlockSpec((tm, D), lambda i: (i, 0)),
        out_shape=jax.ShapeDtypeStruct((n_rows, D), F32),
        compiler_params=_params("parallel"),
        name="final_norm",
    )(x, g.reshape(1, D))


def _mm_kernel(*refs, nk, n_x, x_split, tm, p_tok, ts):
    a_ref, b_ref = refs[:2]
    x_refs, g_ref = refs[2:2 + n_x], (refs[2 + n_x] if n_x else None)
    o_ref = refs[3 + n_x] if n_x else refs[2]
    acc_ref = refs[-1] if nk > 1 else None

    def finish(acc):
        if n_x:
            row = _cond_row(pl.program_id(0) * tm, p_tok, ts)
            o_ref[...] = _rows(x_refs, x_split) + g_ref[pl.ds(row, 1), :] * acc
        else:
            o_ref[...] = acc

    part = jnp.dot(a_ref[...], b_ref[...].astype(BF16), preferred_element_type=F32)
    if nk == 1:
        finish(part)
    else:
        k = pl.program_id(2)

        @pl.when(k == 0)
        def _():
            acc_ref[...] = part

        @pl.when((k > 0) & (k < nk - 1))
        def _():
            acc_ref[...] += part

        @pl.when(k == nk - 1)
        def _():
            finish(acc_ref[...] + part)


def _matmul(a, b, *, tm=MM_TM, tn=MM_TN, tk=None, layer=None, n_cols=None, resid=None, name):
    M, K = a.shape
    N = b.shape[-1] if n_cols is None else n_cols
    tk = K if tk is None else tk
    nk = K // tk
    if layer is None:
        b_spec = pl.BlockSpec((tk, tn), lambda i, j, k: (k, j))
    else:
        b_spec = pl.BlockSpec((None, tk, tn), lambda i, j, k: (layer, k, j))
    in_specs = [pl.BlockSpec((tm, tk), lambda i, j, k: (i, k)), b_spec]
    args = [a, b]
    p_tok = ts = x_split = n_x = 0
    if resid is not None:
        xs, mod, mod_layer, gate_blk, p_tok, ts = resid
        assert p_tok % tm == 0 and ts % tm == 0, "a row tile must not mix modulation groups"
        x_specs, x_split = _split_specs(xs, tm, (tm, tn), lambda r: (lambda i, j, k: (r(i), j)))
        n_x = len(xs)
        in_specs += x_specs + [pl.BlockSpec((None, SUBLANES, tn), lambda i, j, k: (mod_layer, 0, gate_blk + j))]
        args += list(xs) + [mod]
    return pl.pallas_call(
        functools.partial(_mm_kernel, nk=nk, n_x=n_x, x_split=x_split, tm=tm, p_tok=p_tok, ts=ts),
        grid=(M // tm, N // tn, nk),
        in_specs=in_specs,
        out_specs=pl.BlockSpec((tm, tn), lambda i, j, k: (i, j)),
        out_shape=jax.ShapeDtypeStruct((M, N), F32),
        scratch_shapes=[pltpu.VMEM((tm, tn), F32)] if nk > 1 else [],
        compiler_params=_params("parallel", "parallel", "arbitrary"),
        name=name,
    )(*args)


def _store_kv_kernel(k_ref, v_ref, _ko_in, _vo_in, ko_ref, vo_ref):
    ko_ref[...] = k_ref[...]
    vo_ref[...] = v_ref[...]


def _store_kv(proj, ko, vo, layer, *, Bp, Tp, DA):
    out = pl.BlockSpec((None, None, Tp, DA), lambda b: (b, layer, 0, 0))
    return pl.pallas_call(
        _store_kv_kernel,
        grid=(Bp,),
        in_specs=[pl.BlockSpec((Tp, DA), lambda b: (b, 1)), pl.BlockSpec((Tp, DA), lambda b: (b, 2)),
                  _any_spec(), _any_spec()],
        out_specs=[out, out],
        out_shape=[jax.ShapeDtypeStruct(ko.shape, ko.dtype), jax.ShapeDtypeStruct(vo.shape, vo.dtype)],
        input_output_aliases={2: 0, 3: 1},
        compiler_params=_params("parallel"),
        name="store_kv",
    )(proj, proj, ko, vo)


def _rope(x, cos, sin_signed):
    lane = lax.broadcasted_iota(jnp.int32, x.shape, 1)
    width = x.shape[1]
    rot = jnp.where((lane % 64) < 32, pltpu.roll(x, width - 32, 1), pltpu.roll(x, 32, 1))
    return x * cos + rot * sin_signed


def _attn_kernel(*refs, rope, lam_init, key_block):
    if rope:
        (lam_ref, q_ref, k_ref, v_ref, kc_ref, vc_ref, cq_ref, sq_ref, ck_ref, sk_ref, g_ref, _mix_ref,
         o_ref, kr_ref, vr_ref) = refs
    else:
        lam_ref, q_ref, k_ref, v_ref, g_ref, _mix_ref, o_ref = refs
    lm = lam_ref[...]
    s1 = jnp.sum(lm[0:1] * lm[1:2], axis=-1, keepdims=True)
    s2 = jnp.sum(lm[2:3] * lm[3:4], axis=-1, keepdims=True)
    lam = jnp.exp(s1) - jnp.exp(s2) + lam_init
    qscale = DA_D ** -0.5 * math.log2(math.e)

    if rope:
        @pl.when(pl.program_id(2) == 0)
        def _():
            kr_ref[...] = _rope(k_ref[...], ck_ref[...], sk_ref[...]).astype(BF16)
            vr_ref[...] = v_ref[...].astype(BF16)

        q = (_rope(q_ref[...], cq_ref[...], sq_ref[...]) * qscale).astype(BF16)
        T = kr_ref.shape[0]
        blocks = [(kr_ref[j * key_block:(j + 1) * key_block], vr_ref[j * key_block:(j + 1) * key_block])
                  for j in range(T // key_block)]
        blocks.append((kc_ref[...].astype(BF16), vc_ref[...].astype(BF16)))
    else:
        q = (q_ref[...] * qscale).astype(BF16)
        blocks = [(k_ref[...].astype(BF16), v_ref[...].astype(BF16))]

    cols = [slice(c * DA_D, (c + 1) * DA_D) for c in range(2)]
    ss = [[_mm1(q[:, cols[c]], kb[:, cols[c]], "nt") for c in range(2)] for kb, _ in blocks]
    m = [functools.reduce(jnp.maximum, [jnp.max(s[c], axis=-1, keepdims=True) for s in ss]) for c in range(2)]
    o = None
    d = [None, None]
    tq = q.shape[0]
    for s, (_, vb) in zip(ss, blocks):
        e = [jnp.exp2(s[c] - m[c]) for c in range(2)]
        for c in range(2):
            t = jnp.sum(e[c], axis=-1, keepdims=True)
            d[c] = t if d[c] is None else d[c] + t
        u = _mm1(jnp.concatenate([e[0].astype(BF16), e[1].astype(BF16)], axis=0), vb)
        o = u if o is None else o + u
    o = [o[:tq], o[tq:]]
    maps = list(zip(o, d))
    o = maps[0][0] * (1.0 / maps[0][1]) - maps[1][0] * (lam / maps[1][1])
    y = o * lax.rsqrt(jnp.mean(o * o, axis=-1, keepdims=True) + NORM_EPS)
    o_ref[...] = (y * g_ref[...] * (1.0 - lam_init)).astype(o_ref.dtype)


def _attention(proj, mix, lam_p, g, lam_init, *, row0, B, T, H, ctx=None, tq=ATT_TQ, key_block=ATT_KEYS):
    W = 2 * DA_D
    nq = T // tq
    qb0 = row0 // tq
    kb0 = row0 // T
    rope = ctx is not None
    in_specs = [pl.BlockSpec((4, DA_D), lambda b, h, i: (0, 0)),
                pl.BlockSpec((tq, W), lambda b, h, i: (qb0 + b * nq + i, h)),
                pl.BlockSpec((T, W), lambda b, h, i: (kb0 + b, H + h)),
                pl.BlockSpec((T, W), lambda b, h, i: (kb0 + b, 2 * H + h))]
    args = [lam_p, proj, proj, proj]
    scratch = []
    if rope:
        ck, cv, layer, cos, sin = ctx
        past = ck.shape[2]
        in_specs += [pl.BlockSpec((None, None, past, W), lambda b, h, i: (b, layer, 0, h)),
                     pl.BlockSpec((None, None, past, W), lambda b, h, i: (b, layer, 0, h)),
                     pl.BlockSpec((tq, W), lambda b, h, i: (i, 0)),
                     pl.BlockSpec((tq, W), lambda b, h, i: (i, 0)),
                     pl.BlockSpec((T, W), lambda b, h, i: (0, 0)),
                     pl.BlockSpec((T, W), lambda b, h, i: (0, 0))]
        args += [ck, cv, cos, sin, cos, sin]
        scratch = [pltpu.VMEM((T, W), BF16), pltpu.VMEM((T, W), BF16)]
    in_specs += [pl.BlockSpec((1, W), lambda b, h, i: (0, 0)), _any_spec()]
    args += [g.reshape(1, W), mix]
    return pl.pallas_call(
        functools.partial(_attn_kernel, rope=rope, lam_init=lam_init, key_block=key_block),
        grid=(B, H, nq),
        in_specs=in_specs,
        out_specs=pl.BlockSpec((tq, W), lambda b, h, i: (qb0 + b * nq + i, h)),
        out_shape=jax.ShapeDtypeStruct(mix.shape, mix.dtype),
        input_output_aliases={len(args) - 1: 0},
        scratch_shapes=scratch,
        compiler_params=_params("parallel", "parallel", "arbitrary"),
        name="diff_attn_ctx" if rope else "diff_attn",
    )(*args)


def _tile_edges(tok0, tm, p_tok, tp, ts):
    in_p = tok0 < p_tok
    pos = jnp.where(in_p, tok0 % tp, jnp.maximum(tok0 - p_tok, 0) % ts)
    length = jnp.where(in_p, tp, ts)
    return (pos != 0).astype(F32), (pos + tm != length).astype(F32)


def _conv3(x_ref, p_ref, n_ref, s_ref, w, keep_prev, keep_next, cols=slice(None)):
    tm = x_ref.shape[0]
    s_ref[SUBLANES:SUBLANES + tm, :] = x_ref[:, cols]
    s_ref[SUBLANES - 1:SUBLANES, :] = p_ref[SUBLANES - 1:SUBLANES, cols] * keep_prev
    s_ref[SUBLANES + tm:SUBLANES + tm + 1, :] = n_ref[0:1, cols] * keep_next
    return (s_ref[SUBLANES - 1:SUBLANES - 1 + tm, :] * w[0:1] + x_ref[:, cols] * w[1:2]
            + s_ref[SUBLANES + 1:SUBLANES + 1 + tm, :] * w[2:3])


def _halo_specs(tm, width, col_blk, n_tok, nidx=1):
    r = tm // SUBLANES
    last = n_tok // SUBLANES - 1
    if nidx == 1:
        return [pl.BlockSpec((tm, width), lambda i: (i, col_blk)),
                pl.BlockSpec((SUBLANES, width), lambda i: (jnp.maximum(i * r - 1, 0), col_blk)),
                pl.BlockSpec((SUBLANES, width), lambda i: (jnp.minimum((i + 1) * r, last), col_blk))]
    return [pl.BlockSpec((tm, width), lambda i, j: (i, col_blk + j)),
            pl.BlockSpec((SUBLANES, width), lambda i, j: (jnp.maximum(i * r - 1, 0), col_blk + j)),
            pl.BlockSpec((SUBLANES, width), lambda i, j: (jnp.minimum((i + 1) * r, last), col_blk + j))]


def _cmlp_kernel(u_ref, v_ref, gain_ref, ws_ref, bs_ref, _mix_ref, o_ref, *, tm):
    v = v_ref[...]
    z = (v * lax.rsqrt(jnp.mean(v * v, axis=-1, keepdims=True) + NORM_EPS) * gain_ref[...]).astype(BF16)
    gw = z.shape[1] // CM_GROUPS
    for n in range(tm // CM_CHUNK):
        rows = slice(n * CM_CHUNK, (n + 1) * CM_CHUNK)
        for g in range(CM_GROUPS):
            cols = slice(g * gw, (g + 1) * gw)
            t = jnp.dot(ws_ref[g].astype(BF16), z[rows, cols], preferred_element_type=F32) + bs_ref[g]
            o_ref[rows, cols] = (u_ref[rows, cols] * t).astype(o_ref.dtype)


def _chunk_mlp(proj, mix, gain, ws, bs, *, u_blk, out_blk, tm=ROW_TM):
    n_tok = proj.shape[0]
    W = gain.shape[0]
    gw = W // CM_GROUPS
    bs_b = jnp.broadcast_to(bs[:, :, None], (CM_GROUPS, CM_CHUNK, gw))
    return pl.pallas_call(
        functools.partial(_cmlp_kernel, tm=tm),
        grid=(n_tok // tm,),
        in_specs=[pl.BlockSpec((tm, W), lambda i: (i, u_blk)),
                  pl.BlockSpec((tm, W), lambda i: (i, u_blk + 1)),
                  pl.BlockSpec((1, W), lambda i: (0, 0)),
                  pl.BlockSpec((CM_GROUPS, CM_CHUNK, CM_CHUNK), lambda i: (0, 0, 0)),
                  pl.BlockSpec((CM_GROUPS, CM_CHUNK, gw), lambda i: (0, 0, 0)),
                  _any_spec()],
        out_specs=pl.BlockSpec((tm, W), lambda i: (i, out_blk)),
        out_shape=jax.ShapeDtypeStruct(mix.shape, mix.dtype),
        input_output_aliases={5: 0},
        compiler_params=_params("parallel"),
        name="chunk_mlp",
    )(proj, proj, gain.reshape(1, W), ws, bs_b, mix)


def _ffn_act_kernel(a_ref, ap_ref, an_ref, b_ref, bp_ref, bn_ref, wa_ref, wb_ref, ba_ref, bb_ref, o_ref,
                    sa_ref, sb_ref, *, tm, tc, unroll, p_tok, tp, ts):
    keep_prev, keep_next = _tile_edges(pl.program_id(0) * tm, tm, p_tok, tp, ts)

    def chunk(j, slab):
        cols = pl.ds(pl.multiple_of(j * LANES, LANES), LANES)
        ga = _conv3(a_ref, ap_ref, an_ref, sa_ref.at[slab], wa_ref[:, cols], keep_prev, keep_next, cols)
        gb = _conv3(b_ref, bp_ref, bn_ref, sb_ref.at[slab], wb_ref[:, cols], keep_prev, keep_next, cols)
        ga = ga + ba_ref[:, cols]
        o_ref[:, cols] = (ga * jax.nn.sigmoid(ga) * (gb + bb_ref[:, cols])).astype(o_ref.dtype)

    n_chunks = tc // LANES

    def body(g, carry):
        for u in range(unroll):
            chunk(g * unroll + u, u)
        return carry

    lax.fori_loop(0, n_chunks // unroll, body, 0)
    for u in range(n_chunks % unroll):
        chunk(n_chunks - n_chunks % unroll + u, u)


def _ffn_act(up, conv_w, conv_b, *, p_tok, tp, ts, tm=ROW_TM, col_blocks=2):
    n_tok, two_f = up.shape
    F = two_f // 2
    tc = F // col_blocks
    assert tp % tm == 0 and ts % tm == 0 and tc % LANES == 0
    cb = conv_b.reshape(1, two_f)
    wspec = lambda off: pl.BlockSpec((CONV_W, tc), lambda i, j: (0, off + j))
    bspec = lambda off: pl.BlockSpec((1, tc), lambda i, j: (0, off + j))
    unroll = 4
    halo = pltpu.VMEM((unroll, tm + 2 * SUBLANES, LANES), F32)
    return pl.pallas_call(
        functools.partial(_ffn_act_kernel, tm=tm, tc=tc, unroll=unroll, p_tok=p_tok, tp=tp, ts=ts),
        grid=(n_tok // tm, col_blocks),
        in_specs=(_halo_specs(tm, tc, 0, n_tok, 2) + _halo_specs(tm, tc, col_blocks, n_tok, 2)
                  + [wspec(0), wspec(col_blocks), bspec(0), bspec(col_blocks)]),
        out_specs=pl.BlockSpec((tm, tc), lambda i, j: (i, j)),
        out_shape=jax.ShapeDtypeStruct((n_tok, F), BF16),
        scratch_shapes=[halo, halo],
        compiler_params=_params("parallel", "parallel"),
        name="ffn_act",
    )(up, up, up, up, up, up, conv_w, conv_w, cb, cb)


_DIMS = {"nn": ((1,), (0,)), "nt": ((1,), (1,)), "tn": ((0,), (0,))}


def _mm1(a, b, kind="nn"):
    return lax.dot_general(a, b, (_DIMS[kind], ((), ())), preferred_element_type=F32)


def _mmh(a, b, dims=(((1,), (0,)), ((), ()))):
    return lax.dot_general(a, b, dims, precision=HIGHEST, preferred_element_type=F32)


def _split3(x):
    hi = x.astype(BF16)
    r1 = x - hi.astype(F32)
    mid = r1.astype(BF16)
    return hi, mid, (r1 - mid.astype(F32)).astype(BF16)


def _head_sum(x):
    r = lax.broadcasted_iota(jnp.int32, (3 * LANES, LANES), 0) % LANES // RW_HEAD
    c = lax.broadcasted_iota(jnp.int32, (3 * LANES, LANES), 1) // RW_HEAD
    e3 = jnp.where(r == c, 1.0, 0.0).astype(BF16)
    parts = _split3(x)
    cols = [_mm1(jnp.concatenate([p[:, j * LANES:(j + 1) * LANES] for p in parts], axis=1), e3)
            for j in range(x.shape[1] // LANES)]
    return jnp.concatenate(cols, axis=1)


def _cumsum_rows(tri, x):
    t = tri.astype(BF16)
    return _mm1(jnp.concatenate([t, t, t], axis=1), jnp.concatenate(_split3(x), axis=0))


def _rw_prep_kernel(r_ref, rp_ref, rn_ref, k_ref, kp_ref, kn_ref, v_ref, vp_ref, vn_ref,
                    z_ref, zp_ref, zn_ref, cw_ref, cz_ref, w0_ref, w2_ref, a0_ref, a2_ref, g2_ref,
                    kk_ref, ka_ref, rk_ref,
                    ro_ref, kko_ref, vo_ref, lw_ref, b_ref, kd_ref, gate_ref, bonus_ref,
                    sr_ref, sk_ref, sv_ref, sz_ref, *, tm, p_tok, tp, ts, C):
    edges = _tile_edges(pl.program_id(0) * tm, tm, p_tok, tp, ts)

    def conv(x_ref, p_ref, n_ref, s_ref, w):
        chunks = []
        for j in range(x_ref.shape[1] // LANES):
            cols = slice(j * LANES, (j + 1) * LANES)
            chunks.append(_conv3(x_ref, p_ref, n_ref, s_ref.at[j], w[:, cols], *edges, cols))
        return jnp.concatenate(chunks, axis=1)

    cw = cw_ref[...]
    r = conv(r_ref, rp_ref, rn_ref, sr_ref, cw[:, 0:C])
    k = conv(k_ref, kp_ref, kn_ref, sk_ref, cw[:, C:2 * C])
    v = conv(v_ref, vp_ref, vn_ref, sv_ref, cw[:, 2 * C:3 * C])
    z = conv(z_ref, zp_ref, zn_ref, sz_ref, cz_ref[...])
    dec = jnp.tanh(z[:, 0:LANES]).astype(BF16)
    aa = z[:, LANES:2 * LANES].astype(BF16)
    gl = jax.nn.sigmoid(z[:, 2 * LANES:4 * LANES]).astype(BF16)
    gate_ref[...] = jnp.dot(gl, g2_ref[...].astype(BF16), preferred_element_type=F32)
    kk = k * kk_ref[...]
    kk = kk * lax.rsqrt(_head_sum(kk * kk) + 1e-12)
    ro_ref[...] = r
    kko_ref[...] = kk
    vo_ref[...] = v
    kd_sum = None
    for d in range(2):
        wl = w0_ref[d:d + 1, :] + jnp.dot(dec, w2_ref[d].astype(BF16), preferred_element_type=F32)
        lw_ref[d] = -math.exp(-0.5) * jax.nn.sigmoid(wl)
        a = jax.nn.sigmoid(a0_ref[d:d + 1, :] + jnp.dot(aa, a2_ref[d].astype(BF16), preferred_element_type=F32))
        b_ref[d] = kk * a
        kd = k * (1.0 + (a - 1.0) * ka_ref[...])
        kd_ref[d] = kd
        kd_sum = kd if kd_sum is None else kd_sum + kd
    bonus_ref[...] = _head_sum(r * kd_sum * rk_ref[...]) * v


def _rw_prep(proj, proj_z, lp, *, r_blk, z_blk, p_tok, tp, ts, tm=ROW_TM):
    assert tp % tm == 0 and ts % tm == 0
    n_tok = proj.shape[0]
    C = lp["rw_k_k"].shape[0]
    ZW = 4 * LANES
    full = lambda shape: pl.BlockSpec(shape, lambda i: (0,) * len(shape))
    tok = pl.BlockSpec((tm, C), lambda i: (i, 0))
    tok2 = pl.BlockSpec((2, tm, C), lambda i: (0, i, 0))
    one = jax.ShapeDtypeStruct((n_tok, C), F32)
    two = jax.ShapeDtypeStruct((2, n_tok, C), F32)
    return pl.pallas_call(
        functools.partial(_rw_prep_kernel, tm=tm, p_tok=p_tok, tp=tp, ts=ts, C=C),
        grid=(n_tok // tm,),
        in_specs=(_halo_specs(tm, C, r_blk, n_tok) + _halo_specs(tm, C, r_blk + 1, n_tok)
                  + _halo_specs(tm, C, r_blk + 2, n_tok) + _halo_specs(tm, ZW, z_blk, n_tok)
                  + [full((CONV_W, 3 * C)), full((CONV_W, ZW)), full((2, C)), full((2, LANES, C)),
                     full((2, C)), full((2, LANES, C)), full((2 * LANES, C)),
                     full((1, C)), full((1, C)), full((1, C))]),
        out_specs=[tok, tok, tok, tok2, tok2, tok2, tok, tok],
        out_shape=[one, one, one, two, two, two, one, one],
        scratch_shapes=[pltpu.VMEM((w // LANES, tm + 2 * SUBLANES, LANES), F32) for w in (C, C, C, ZW)],
        compiler_params=_params("parallel"),
        name="rwkv_prep",
    )(*([proj] * 9 + [proj_z] * 3), lp["cw_rkv"], lp["cw_z"], lp["rw_w0"], lp["w2_pad"], lp["rw_a0"], lp["a2_pad"],
      lp["g2_pad"], lp["rw_k_k"].reshape(1, C), lp["rw_k_a"].reshape(1, C), lp["rw_r_k"].reshape(1, C))


def _rw_scan_kernel(*refs, has_s0, nc):
    if has_s0:
        s0_ref, refs = refs[0], refs[1:]
    (rf_ref, kkf_ref, vf_ref, rb_ref, kkb_ref, vb_ref, lwf_ref, bf_ref, kdf_ref, lwb_ref, bb_ref, kdb_ref,
     yf_ref, yb_ref, so_ref, st_ref) = refs
    C, N = SCAN_CHUNK, RW_HEAD
    W = 2 * N
    npair = rf_ref.shape[1] // W
    c = pl.program_id(1)
    zero = jnp.zeros((), BF16)

    row = lax.broadcasted_iota(jnp.int32, (W, W), 0)
    col = lax.broadcasted_iota(jnp.int32, (W, W), 1)
    same_head = (row // N) == (col // N)
    eye = (row == col).astype(F32)

    @pl.when(c == 0)
    def _():
        if has_s0:
            sel = (lax.broadcasted_iota(jnp.int32, (N, W), 0) == lax.broadcasted_iota(jnp.int32, (N, W), 1) % N)
            for d in range(2):
                for p in range(npair):
                    tiled = _mmh(s0_ref[d, 2 * p:2 * p + 2].reshape(W, N), sel.astype(F32))
                    st_ref[d * npair + p] = jnp.where(same_head, tiled, 0.0)
        else:
            st_ref[...] = jnp.zeros(st_ref.shape, F32)

    def order(shape, dim, bwd):
        t = lax.broadcasted_iota(jnp.int32, shape, dim) % C
        return C - 1 - t if bwd else t

    chains = []
    for d, (r_ref, kk_ref, v_ref, lw_ref, b_ref, kd_ref, y_ref) in enumerate(
            [(rf_ref, kkf_ref, vf_ref, lwf_ref, bf_ref, kdf_ref, yf_ref),
             (rb_ref, kkb_ref, vb_ref, lwb_ref, bb_ref, kdb_ref, yb_ref)]):
        rt, ct = order((W, W), 0, d == 1), order((W, W), 1, d == 1)
        masks = dict(strict=rt > ct, incl=rt >= ct)
        masks.update({s: ((rt // (2 * s)) == (ct // (2 * s))) & ((rt // s) % 2 == 1) & ((ct // s) % 2 == 0)
                      for s in (1, 2, 4, 8, 16, 32)})
        lw = lw_ref[...]
        g_in = _cumsum_rows(order((C, C), 0, d == 1) >= order((C, C), 1, d == 1), lw)
        g_tot = jnp.sum(lw, axis=0, keepdims=True)
        e_neg = jnp.exp(-g_in)
        e_rem = jnp.exp(g_tot - g_in)
        tok = dict(
            kk=(kk_ref[...] * jnp.exp(g_in - lw)).astype(BF16), r=(r_ref[...] * jnp.exp(g_in)).astype(BF16),
            b=(b_ref[...] * e_neg).astype(BF16), kd=(kd_ref[...] * e_neg).astype(BF16),
            b_end=(b_ref[...] * e_rem).astype(BF16), kd_end=(kd_ref[...] * e_rem).astype(BF16),
            v=v_ref[...].astype(BF16), e_tot=jnp.exp(g_tot))
        for p in range(npair):
            chains.append((d * npair + p, slice(p * W, (p + 1) * W), tok, masks, y_ref))

    def expand(x, sl):
        return jnp.where(same_head, jnp.concatenate([x[:, sl]] * 2, axis=0), zero)

    ak = [expand(t["kk"], sl) for _, sl, t, _, _ in chains]
    bk = [jnp.concatenate([expand(t["b"], sl), expand(t["kd"], sl)], axis=0) for _, sl, t, _, _ in chains]
    vb = [expand(t["v"], sl) for _, sl, t, _, _ in chains]
    S = [st_ref[i] for i, *_ in chains]
    Sb = [x.astype(BF16) for x in S]
    n = range(len(chains))
    mk = [ch[3] for ch in chains]
    ar = [expand(t["r"], sl) for _, sl, t, _, _ in chains]
    lmn = [_mm1(jnp.concatenate([ak[i], ar[i]], axis=0), bk[i], "nt") for i in n]
    L = [jnp.where(mk[i]["strict"], lmn[i][:W, :W], 0.0) for i in n]
    Lb = [x.astype(BF16) for x in L]
    M = [jnp.where(mk[i]["strict"], lmn[i][:W, W:], 0.0).astype(BF16) for i in n]
    nbk = [jnp.concatenate([jnp.where(mk[i]["incl"], -lmn[i][W:, :W], 0.0),
                            jnp.where(mk[i]["incl"], lmn[i][W:, W:], 0.0)], axis=1).astype(BF16) for i in n]
    rhs = [(_mm1(ak[i], Sb[i], "nt") + _mm1(M[i], vb[i])).astype(BF16) for i in n]
    X = [eye - jnp.where(mk[i][1], L[i], 0.0) for i in n]
    for s in (2, 4, 8, 16, 32):
        Xb = [x.astype(BF16) for x in X]
        t = [_mm1(jnp.where(mk[i][s], Lb[i], zero), Xb[i]).astype(BF16) for i in n]
        X = [X[i] - _mm1(Xb[i], t[i]) for i in n]
    uv = [jnp.concatenate([_mm1(X[i].astype(BF16), rhs[i]).astype(BF16), vb[i]], axis=0) for i in n]
    for i, (slot, sl, tk, _, _) in enumerate(chains):
        ends = jnp.concatenate([-expand(tk["b_end"], sl), expand(tk["kd_end"], sl)], axis=0)
        st_ref[slot] = S[i] * tk["e_tot"][:, sl] + _mm1(uv[i], ends, "tn")
    for i, (_, sl, _, _, y_ref) in enumerate(chains):
        y = _mm1(ar[i], Sb[i], "nt") + _mm1(nbk[i], uv[i])
        y_ref[:, sl] = y[:C] + y[C:]

    @pl.when(c == nc - 1)
    def _():
        fold = (lax.broadcasted_iota(jnp.int32, (W, N), 0) % N == lax.broadcasted_iota(jnp.int32, (W, N), 1))
        for d in range(2):
            for p in range(npair):
                so_ref[d, 2 * p:2 * p + 2] = _mmh(st_ref[d * npair + p], fold.astype(F32)).reshape(2, N, N)


def _rw_scan(r, kk, v, lw, b, kd, *, row0, B, T, s0=None, layer=0):
    C = r.shape[1]
    H = C // RW_HEAD
    CH = SCAN_CHUNK
    nc = T // CH
    blk0 = row0 // CH
    fwd = pl.BlockSpec((CH, C), lambda bi, c: (blk0 + bi * nc + c, 0))
    bwd = pl.BlockSpec((CH, C), lambda bi, c: (blk0 + bi * nc + nc - 1 - c, 0))
    fwd2 = pl.BlockSpec((None, CH, C), lambda bi, c: (0, blk0 + bi * nc + c, 0))
    bwd2 = pl.BlockSpec((None, CH, C), lambda bi, c: (1, blk0 + bi * nc + nc - 1 - c, 0))
    in_specs = [fwd, fwd, fwd, bwd, bwd, bwd, fwd2, fwd2, fwd2, bwd2, bwd2, bwd2]
    args = [r, kk, v, r, kk, v, lw, b, kd, lw, b, kd]
    if s0 is not None:
        in_specs = [pl.BlockSpec((None, None, 2, H, RW_HEAD, RW_HEAD),
                                 lambda bi, c: (bi, layer, 0, 0, 0, 0))] + in_specs
        args = [s0] + args
    y_shape = jax.ShapeDtypeStruct((B * T, C), F32)
    return pl.pallas_call(
        functools.partial(_rw_scan_kernel, has_s0=s0 is not None, nc=nc),
        grid=(B, nc),
        in_specs=in_specs,
        out_specs=[pl.BlockSpec((CH, C), lambda bi, c: (bi * nc + c, 0)),
                   pl.BlockSpec((CH, C), lambda bi, c: (bi * nc + nc - 1 - c, 0)),
                   pl.BlockSpec((None, 2, H, RW_HEAD, RW_HEAD), lambda bi, c: (bi, 0, 0, 0, 0))],
        out_shape=[y_shape, y_shape, jax.ShapeDtypeStruct((B, 2, H, RW_HEAD, RW_HEAD), F32)],
        scratch_shapes=[pltpu.VMEM((H, 2 * RW_HEAD, 2 * RW_HEAD), F32)],
        compiler_params=_params("parallel", "arbitrary"),
        name="rwkv_scan_ctx" if s0 is not None else "rwkv_scan",
    )(*args)


def _rw_post_kernel(yf_ref, yb_ref, bonus_ref, gate_ref, g_ref, b_ref, _mix_ref, o_ref):
    y = yf_ref[...] + yb_ref[...]
    inv_n = 1.0 / RW_HEAD
    mu = _head_sum(y) * inv_n
    yc = y - mu
    var = _head_sum(yc * yc) * inv_n
    yn = yc * lax.rsqrt(var + RW_GN_EPS) * g_ref[...] + b_ref[...]
    o_ref[...] = ((yn + bonus_ref[...]) * gate_ref[...]).astype(o_ref.dtype)


def _rw_post(y_f, y_b, bonus, gate, mix, gn_g, gn_b, *, row0, out_blk, tm=ROW_TM):
    n_rows, C = y_f.shape
    r0 = row0 // tm
    own = pl.BlockSpec((tm, C), lambda i: (i, 0))
    tok = pl.BlockSpec((tm, C), lambda i: (r0 + i, 0))
    vec = pl.BlockSpec((1, C), lambda i: (0, 0))
    return pl.pallas_call(
        _rw_post_kernel,
        grid=(n_rows // tm,),
        in_specs=[own, own, tok, tok, vec, vec, _any_spec()],
        out_specs=pl.BlockSpec((tm, C), lambda i: (r0 + i, out_blk)),
        out_shape=jax.ShapeDtypeStruct(mix.shape, mix.dtype),
        input_output_aliases={6: 0},
        compiler_params=_params("parallel"),
        name="rwkv_post",
    )(y_f, y_b, bonus, gate, gn_g.reshape(1, C), gn_b.reshape(1, C), mix)


def _pad_rows(w, rows, at):
    return jnp.zeros((rows, w.shape[1]), w.dtype).at[at:at + w.shape[0]].set(w)


def _layer_weights(l, D, w_in_b, rw_conv_w, rw_w2, rw_a2, rw_g2):
    C = D // 4
    o_z = 3 * (D // 2) + 3 * C
    n_z = 4 * RW_LORA_R + RW_GATE_R
    zpad = 4 * LANES - n_z
    cw = rw_conv_w[l]
    w = w_in_b[l]
    return dict(
        w_uvz=jnp.concatenate([w[:, o_z + n_z:], w[:, o_z:o_z + n_z], jnp.zeros((D, zpad), BF16)], axis=1),
        cw_rkv=cw[:, :3 * C],
        cw_z=jnp.concatenate([cw[:, 3 * C:], jnp.zeros((CONV_W, zpad), F32)], axis=1),
        w2_pad=jnp.stack([_pad_rows(rw_w2[l, d], LANES, d * RW_LORA_R) for d in range(2)]),
        a2_pad=jnp.stack([_pad_rows(rw_a2[l, d], LANES, d * RW_LORA_R) for d in range(2)]),
        g2_pad=_pad_rows(rw_g2[l], 2 * LANES, 0),
    )


def _rope_tables(T):
    n = DA_D // 4
    inv = ROPE_THETA ** (-jnp.arange(n, dtype=F32) / n)
    rows = T // GRID_W
    row = jnp.repeat(jnp.arange(rows), GRID_W).astype(F32)
    col = jnp.tile(jnp.arange(GRID_W), rows).astype(F32)
    sign = jnp.concatenate([-jnp.ones((n,), F32), jnp.ones((n,), F32)])
    cs, sn = [], []
    for pos in (row, col):
        ang = pos[:, None] * inv[None, :]
        cs.append(jnp.concatenate([jnp.cos(ang), jnp.cos(ang)], axis=1))
        sn.append(jnp.concatenate([jnp.sin(ang), jnp.sin(ang)], axis=1) * sign[None, :])
    cos = jnp.concatenate(cs, axis=1)
    sin = jnp.concatenate(sn, axis=1)
    return jnp.tile(cos, (1, 2)), jnp.tile(sin, (1, 2))


def kernel(x_prompt, x_sample, cache_da_k, cache_da_v, state_rwkv, c, c_ctx, mod_w, mod_b, norm1_g, norm2_g, w_in, da_lambda, da_subln_g, rw_conv_w, rw_w0, rw_w2, rw_a0, rw_a2, rw_g2, rw_k_k, rw_k_a, rw_r_k, rw_gn_g, rw_gn_b, cm_norm_g, cm_ws, cm_bs, w_out, ffn_up, ffn_conv_w, ffn_conv_b, ffn_down, final_norm_g):
    Bp, Tp, D = x_prompt.shape
    Bs, Ts, _ = x_sample.shape
    L = mod_w.shape[0]
    past = cache_da_k.shape[2]
    DA = D // 2
    H = DA // (2 * DA_D)
    C = D // 4
    F = ffn_down.shape[1]
    p_tok, s_tok = Bp * Tp, Bs * Ts
    assert Bs + 1 <= SUBLANES and p_tok % Ts == 0

    xs = (x_prompt.reshape(p_tok, D), x_sample.reshape(s_tok, D))
    cond8 = jnp.concatenate([c_ctx[None, :], c, jnp.zeros((SUBLANES - 1 - Bs, D), F32)], axis=0)
    mod = _modulation(cond8, mod_w, mod_b)
    ck4 = cache_da_k.reshape(Bs, L, past, DA)
    cv4 = cache_da_v.reshape(Bs, L, past, DA)
    cos, sin = _rope_tables(Ts)
    ko = jnp.zeros((Bp, L, Tp, DA), F32)
    vo = jnp.zeros((Bp, L, Tp, DA), F32)
    w_in_b, ffn_down_b = w_in.astype(BF16), ffn_down.astype(BF16)
    n_qkv_rkv = 3 * DA + 3 * C

    new_s = []
    for l in range(L):
        lw_ = _layer_weights(l, D, w_in_b, rw_conv_w, rw_w2, rw_a2, rw_g2)
        lp = dict(lw_, rw_w0=rw_w0[l], rw_a0=rw_a0[l], rw_k_k=rw_k_k[l], rw_k_a=rw_k_a[l], rw_r_k=rw_r_k[l])
        lam_init = 0.8 - 0.6 * math.exp(-0.3 * l)

        h = _norm_mod(xs, norm1_g[l], mod, l, 0, 1, p_tok, Ts)
        proj = _matmul(h, w_in_b, layer=l, n_cols=n_qkv_rkv, name="proj_in")
        proj_uz = _matmul(h, lp["w_uvz"], name="proj_in_uvz")
        ko, vo = _store_kv(proj, ko, vo, l, Bp=Bp, Tp=Tp, DA=DA)

        mix = jnp.zeros((p_tok + s_tok, D), BF16)
        mix = _attention(proj, mix, da_lambda[l], da_subln_g[l], lam_init, row0=0, B=Bp, T=Tp, H=H)
        mix = _attention(proj, mix, da_lambda[l], da_subln_g[l], lam_init, row0=p_tok, B=Bs, T=Ts, H=H,
                         ctx=(ck4, cv4, l, cos, sin))
        r_, kk_, v_, lg_, b_, kd_, gate_, bonus_ = _rw_prep(proj, proj_uz, lp, r_blk=3 * DA // C,
                                                            z_blk=2 * C // (4 * LANES), p_tok=p_tok, tp=Tp, ts=Ts)
        yf_p, yb_p, s_p = _rw_scan(r_, kk_, v_, lg_, b_, kd_, row0=0, B=Bp, T=Tp)
        yf_s, yb_s, _ = _rw_scan(r_, kk_, v_, lg_, b_, kd_, row0=p_tok, B=Bs, T=Ts, s0=state_rwkv, layer=l)
        mix = _rw_post(yf_p, yb_p, bonus_, gate_, mix, rw_gn_g[l], rw_gn_b[l], row0=0, out_blk=DA // C)
        mix = _rw_post(yf_s, yb_s, bonus_, gate_, mix, rw_gn_g[l], rw_gn_b[l], row0=p_tok, out_blk=DA // C)
        mix = _chunk_mlp(proj_uz, mix, cm_norm_g[l], cm_ws[l], cm_bs[l], u_blk=0, out_blk=DA // C + 1)

        x = _matmul(mix, w_out, layer=l, resid=(xs, mod, l, 2 * D // MM_TN, p_tok, Ts), name="proj_out")
        xs = (x,)

        h = _norm_mod(xs, norm2_g[l], mod, l, 3, 4, p_tok, Ts)
        up = _matmul(h, ffn_up, layer=l, name="ffn_up")
        act = _ffn_act(up, ffn_conv_w[l], ffn_conv_b[l], p_tok=p_tok, tp=Tp, ts=Ts)
        x = _matmul(act, ffn_down_b, layer=l, tk=F // 2, resid=(xs, mod, l, 5 * D // MM_TN, p_tok, Ts),
                    name="ffn_down")
        xs = (x,)

        new_s.append(s_p)

    y_p = _final_norm(x, final_norm_g, row0=0, n_rows=p_tok)
    y_s = _final_norm(x, final_norm_g, row0=p_tok, n_rows=s_tok)
    return (y_p.reshape(Bp, Tp, D), y_s.reshape(Bs, Ts, D), ko.reshape(Bp, L, Tp, H, 2, DA_D),
            vo.reshape(Bp, L, Tp, H, 2 * DA_D), jnp.stack(new_s, axis=1))
```

```python
import functools
import math

import jax
import jax.numpy as jnp
from jax import lax
from jax.experimental import pallas as pl
from jax.experimental.pallas import tpu as pltpu

F32 = jnp.float32
BF16 = jnp.bfloat16
HIGHEST = lax.Precision.HIGHEST

LANES = 128
SUBLANES = 8
VMEM_BYTES_V7X = 64 * 1024 * 1024
VMEM_BUDGET = VMEM_BYTES_V7X * 3 // 4

GRID_W = 64
DA_D = 128
RW_HEAD = 64
RW_LORA_R = 64
RW_GATE_R = 160
RW_GN_EPS = 64e-5
CM_GROUPS = 4
CM_CHUNK = 128
CONV_W = 3
ROPE_THETA = 10000.0
NORM_EPS = 1e-6
SCAN_CHUNK = 64

MM_TM, MM_TN = 1024, 512
ROW_TM = 256
ATT_TQ, ATT_KEYS = 512, 512


def _params(*sem):
    return pltpu.CompilerParams(dimension_semantics=sem, vmem_limit_bytes=VMEM_BUDGET)


def _any_spec():
    return pl.BlockSpec(memory_space=pl.ANY)


def _cond_row(tok0, p_tok, ts):
    return jnp.where(tok0 < p_tok, 0, 1 + jnp.maximum(tok0 - p_tok, 0) // ts)


def _mod_kernel(c_ref, w_ref, b_ref, o_ref):
    c = c_ref[...]
    s = (c * jax.nn.sigmoid(c)).astype(BF16)
    o_ref[...] = jnp.dot(s, w_ref[...].astype(BF16), preferred_element_type=F32) + b_ref[...]


def _modulation(cond8, mod_w, mod_b, tn=MM_TN):
    L, D, N = mod_w.shape
    return pl.pallas_call(
        _mod_kernel,
        grid=(L, N // tn),
        in_specs=[pl.BlockSpec((SUBLANES, D), lambda l, j: (0, 0)),
                  pl.BlockSpec((None, D, tn), lambda l, j: (l, 0, j)),
                  pl.BlockSpec((None, 1, tn), lambda l, j: (l, 0, j))],
        out_specs=pl.BlockSpec((None, SUBLANES, tn), lambda l, j: (l, 0, j)),
        out_shape=jax.ShapeDtypeStruct((L, SUBLANES, N), F32),
        compiler_params=_params("parallel", "parallel"),
        name="modulation",
    )(cond8, mod_w, mod_b.reshape(L, 1, N))


def _rows(refs, first_tiles):
    if len(refs) == 1:
        return refs[0][...]
    return jnp.where(pl.program_id(0) < first_tiles, refs[0][...], refs[1][...])


def _split_specs(xs, tm, block, idx):
    if len(xs) == 1:
        return [pl.BlockSpec(block, idx(lambda i: i))], 0
    first = xs[0].shape[0] // tm
    last = xs[1].shape[0] // tm - 1
    return [pl.BlockSpec(block, idx(lambda i: jnp.minimum(i, first - 1))),
            pl.BlockSpec(block, idx(lambda i: jnp.clip(i - first, 0, last)))], first


def _norm_mod_kernel(*refs, n_x, x_split, tm, p_tok, ts):
    x_refs, (g_ref, sh_ref, sc_ref, o_ref) = refs[:n_x], refs[n_x:]
    row = _cond_row(pl.program_id(0) * tm, p_tok, ts)
    x = _rows(x_refs, x_split)
    y = x * lax.rsqrt(jnp.mean(x * x, axis=-1, keepdims=True) + NORM_EPS)
    sc = sc_ref[pl.ds(row, 1), :]
    sh = sh_ref[pl.ds(row, 1), :]
    o_ref[...] = ((y * g_ref[...]) * (1.0 + sc) + sh).astype(o_ref.dtype)


def _norm_mod(xs, g, mod, layer, k_sh, k_sc, p_tok, ts, tm=ROW_TM):
    n = sum(x.shape[0] for x in xs)
    D = xs[0].shape[1]
    assert p_tok % tm == 0 and ts % tm == 0, "a row tile must not mix modulation groups"
    x_specs, x_split = _split_specs(xs, tm, (tm, D), lambda r: (lambda i: (r(i), 0)))
    return pl.pallas_call(
        functools.partial(_norm_mod_kernel, n_x=len(xs), x_split=x_split, tm=tm, p_tok=p_tok, ts=ts),
        grid=(n // tm,),
        in_specs=x_specs + [pl.BlockSpec((1, D), lambda i: (0, 0)),
                            pl.BlockSpec((None, SUBLANES, D), lambda i: (layer, 0, k_sh)),
                            pl.BlockSpec((None, SUBLANES, D), lambda i: (layer, 0, k_sc))],
        out_specs=pl.BlockSpec((tm, D), lambda i: (i, 0)),
        out_shape=jax.ShapeDtypeStruct((n, D), BF16),
        compiler_params=_params("parallel"),
        name="norm_mod",
    )(*xs, g.reshape(1, D), mod, mod)


def _final_norm_kernel(x_ref, g_ref, o_ref):
    x = x_ref[...]
    o_ref[...] = x * lax.rsqrt(jnp.mean(x * x, axis=-1, keepdims=True) + NORM_EPS) * g_ref[...]


def _final_norm(x, g, *, row0, n_rows, tm=ROW_TM):
    D = x.shape[1]
    r0 = row0 // tm
    return pl.pallas_call(
        _final_norm_kernel,
        grid=(n_rows // tm,),
        in_specs=[pl.BlockSpec((tm, D), lambda i: (r0 + i, 0)), pl.BlockSpec((1, D), lambda i: (0, 0))],
        out_specs=pl.BlockSpec((tm, D), lambda i: (i, 0)),
        out_shape=jax.ShapeDtypeStruct((n_rows, D), F32),
        compiler_params=_params("parallel"),
        name="final_norm",
    )(x, g.reshape(1, D))


def _mm_kernel(*refs, nk, n_x, x_split, tm, p_tok, ts):
    a_ref, b_ref = refs[:2]
    x_refs, g_ref = refs[2:2 + n_x], (refs[2 + n_x] if n_x else None)
    o_ref = refs[3 + n_x] if n_x else refs[2]
    acc_ref = refs[-1] if nk > 1 else None

    def finish(acc):
        if n_x:
            row = _cond_row(pl.program_id(0) * tm, p_tok, ts)
            o_ref[...] = _rows(x_refs, x_split) + g_ref[pl.ds(row, 1), :] * acc
        else:
            o_ref[...] = acc

    part = jnp.dot(a_ref[...], b_ref[...].astype(BF16), preferred_element_type=F32)
    if nk == 1:
        finish(part)
    else:
        k = pl.program_id(2)

        @pl.when(k == 0)
        def _():
            acc_ref[...] = part

        @pl.when((k > 0) & (k < nk - 1))
        def _():
            acc_ref[...] += part

        @pl.when(k == nk - 1)
        def _():
            finish(acc_ref[...] + part)


def _matmul(a, b, *, tm=MM_TM, tn=MM_TN, tk=None, layer=None, n_cols=None, resid=None, name):
    M, K = a.shape
    N = b.shape[-1] if n_cols is None else n_cols
    tk = K if tk is None else tk
    nk = K // tk
    if layer is None:
        b_spec = pl.BlockSpec((tk, tn), lambda i, j, k: (k, j))
    else:
        b_spec = pl.BlockSpec((None, tk, tn), lambda i, j, k: (layer, k, j))
    in_specs = [pl.BlockSpec((tm, tk), lambda i, j, k: (i, k)), b_spec]
    args = [a, b]
    p_tok = ts = x_split = n_x = 0
    if resid is not None:
        xs, mod, mod_layer, gate_blk, p_tok, ts = resid
        assert p_tok % tm == 0 and ts % tm == 0, "a row tile must not mix modulation groups"
        x_specs, x_split = _split_specs(xs, tm, (tm, tn), lambda r: (lambda i, j, k: (r(i), j)))
        n_x = len(xs)
        in_specs += x_specs + [pl.BlockSpec((None, SUBLANES, tn), lambda i, j, k: (mod_layer, 0, gate_blk + j))]
        args += list(xs) + [mod]
    return pl.pallas_call(
        functools.partial(_mm_kernel, nk=nk, n_x=n_x, x_split=x_split, tm=tm, p_tok=p_tok, ts=ts),
        grid=(M // tm, N // tn, nk),
        in_specs=in_specs,
        out_specs=pl.BlockSpec((tm, tn), lambda i, j, k: (i, j)),
        out_shape=jax.ShapeDtypeStruct((M, N), F32),
        scratch_shapes=[pltpu.VMEM((tm, tn), F32)] if nk > 1 else [],
        compiler_params=_params("parallel", "parallel", "arbitrary"),
        name=name,
    )(*args)


def _store_kv_kernel(k_ref, v_ref, _ko_in, _vo_in, ko_ref, vo_ref):
    ko_ref[...] = k_ref[...]
    vo_ref[...] = v_ref[...]


def _store_kv(proj, ko, vo, layer, *, Bp, Tp, DA):
    out = pl.BlockSpec((None, None, Tp, DA), lambda b: (b, layer, 0, 0))
    return pl.pallas_call(
        _store_kv_kernel,
        grid=(Bp,),
        in_specs=[pl.BlockSpec((Tp, DA), lambda b: (b, 1)), pl.BlockSpec((Tp, DA), lambda b: (b, 2)),
                  _any_spec(), _any_spec()],
        out_specs=[out, out],
        out_shape=[jax.ShapeDtypeStruct(ko.shape, ko.dtype), jax.ShapeDtypeStruct(vo.shape, vo.dtype)],
        input_output_aliases={2: 0, 3: 1},
        compiler_params=_params("parallel"),
        name="store_kv",
    )(proj, proj, ko, vo)


def _rope(x, cos, sin_signed):
    lane = lax.broadcasted_iota(jnp.int32, x.shape, 1)
    width = x.shape[1]
    rot = jnp.where((lane % 64) < 32, pltpu.roll(x, width - 32, 1), pltpu.roll(x, 32, 1))
    return x * cos + rot * sin_signed


def _attn_kernel(*refs, rope, lam_init, key_block):
    if rope:
        (lam_ref, q_ref, k_ref, v_ref, kc_ref, vc_ref, cq_ref, sq_ref, ck_ref, sk_ref, g_ref, _mix_ref,
         o_ref, kr_ref, vr_ref) = refs
    else:
        lam_ref, q_ref, k_ref, v_ref, g_ref, _mix_ref, o_ref = refs
    lm = lam_ref[...]
    s1 = jnp.sum(lm[0:1] * lm[1:2], axis=-1, keepdims=True)
    s2 = jnp.sum(lm[2:3] * lm[3:4], axis=-1, keepdims=True)
    lam = jnp.exp(s1) - jnp.exp(s2) + lam_init
    qscale = DA_D ** -0.5 * math.log2(math.e)

    if rope:
        @pl.when(pl.program_id(2) == 0)
        def _():
            kr_ref[...] = _rope(k_ref[...], ck_ref[...], sk_ref[...]).astype(BF16)
            vr_ref[...] = v_ref[...].astype(BF16)

        q = (_rope(q_ref[...], cq_ref[...], sq_ref[...]) * qscale).astype(BF16)
        T = kr_ref.shape[0]
        blocks = [(kr_ref[j * key_block:(j + 1) * key_block], vr_ref[j * key_block:(j + 1) * key_block])
                  for j in range(T // key_block)]
        blocks.append((kc_ref[...].astype(BF16), vc_ref[...].astype(BF16)))
    else:
        q = (q_ref[...] * qscale).astype(BF16)
        blocks = [(k_ref[...].astype(BF16), v_ref[...].astype(BF16))]

    cols = [slice(c * DA_D, (c + 1) * DA_D) for c in range(2)]
    ss = [[_mm1(q[:, cols[c]], kb[:, cols[c]], "nt") for c in range(2)] for kb, _ in blocks]
    m = [functools.reduce(jnp.maximum, [jnp.max(s[c], axis=-1, keepdims=True) for s in ss]) for c in range(2)]
    o = None
    d = [None, None]
    tq = q.shape[0]
    for s, (_, vb) in zip(ss, blocks):
        e = [jnp.exp2(s[c] - m[c]) for c in range(2)]
        for c in range(2):
            t = jnp.sum(e[c], axis=-1, keepdims=True)
            d[c] = t if d[c] is None else d[c] + t
        u = _mm1(jnp.concatenate([e[0].astype(BF16), e[1].astype(BF16)], axis=0), vb)
        o = u if o is None else o + u
    o = [o[:tq], o[tq:]]
    maps = list(zip(o, d))
    o = maps[0][0] * (1.0 / maps[0][1]) - maps[1][0] * (lam / maps[1][1])
    y = o * lax.rsqrt(jnp.mean(o * o, axis=-1, keepdims=True) + NORM_EPS)
    o_ref[...] = (y * g_ref[...] * (1.0 - lam_init)).astype(o_ref.dtype)


def _attention(proj, mix, lam_p, g, lam_init, *, row0, B, T, H, ctx=None, tq=ATT_TQ, key_block=ATT_KEYS):
    W = 2 * DA_D
    tq = min(tq, T)
    nq = T // tq
    qb0 = row0 // tq
    kb0 = row0 // T
    rope = ctx is not None
    in_specs = [pl.BlockSpec((4, DA_D), lambda b, h, i: (0, 0)),
                pl.BlockSpec((tq, W), lambda b, h, i: (qb0 + b * nq + i, h)),
                pl.BlockSpec((T, W), lambda b, h, i: (kb0 + b, H + h)),
                pl.BlockSpec((T, W), lambda b, h, i: (kb0 + b, 2 * H + h))]
    args = [lam_p, proj, proj, proj]
    scratch = []
    if rope:
        ck, cv, layer, cos, sin = ctx
        past = ck.shape[2]
        in_specs += [pl.BlockSpec((None, None, past, W), lambda b, h, i: (b, layer, 0, h)),
                     pl.BlockSpec((None, None, past, W), lambda b, h, i: (b, layer, 0, h)),
                     pl.BlockSpec((tq, W), lambda b, h, i: (i, 0)),
                     pl.BlockSpec((tq, W), lambda b, h, i: (i, 0)),
                     pl.BlockSpec((T, W), lambda b, h, i: (0, 0)),
                     pl.BlockSpec((T, W), lambda b, h, i: (0, 0))]
        args += [ck, cv, cos, sin, cos, sin]
        scratch = [pltpu.VMEM((T, W), BF16), pltpu.VMEM((T, W), BF16)]
    in_specs += [pl.BlockSpec((1, W), lambda b, h, i: (0, 0)), _any_spec()]
    args += [g.reshape(1, W), mix]
    return pl.pallas_call(
        functools.partial(_attn_kernel, rope=rope, lam_init=lam_init, key_block=key_block),
        grid=(B, H, nq),
        in_specs=in_specs,
        out_specs=pl.BlockSpec((tq, W), lambda b, h, i: (qb0 + b * nq + i, h)),
        out_shape=jax.ShapeDtypeStruct(mix.shape, mix.dtype),
        input_output_aliases={len(args) - 1: 0},
        scratch_shapes=scratch,
        compiler_params=_params("parallel", "parallel", "arbitrary"),
        name="diff_attn_ctx" if rope else "diff_attn",
    )(*args)


def _tile_edges(tok0, tm, p_tok, tp, ts):
    in_p = tok0 < p_tok
    pos = jnp.where(in_p, tok0 % tp, jnp.maximum(tok0 - p_tok, 0) % ts)
    length = jnp.where(in_p, tp, ts)
    return (pos != 0).astype(F32), (pos + tm != length).astype(F32)


def _conv3(x_ref, p_ref, n_ref, s_ref, w, keep_prev, keep_next, cols=slice(None)):
    tm = x_ref.shape[0]
    s_ref[SUBLANES:SUBLANES + tm, :] = x_ref[:, cols]
    s_ref[SUBLANES - 1:SUBLANES, :] = p_ref[SUBLANES - 1:SUBLANES, cols] * keep_prev
    s_ref[SUBLANES + tm:SUBLANES + tm + 1, :] = n_ref[0:1, cols] * keep_next
    return (s_ref[SUBLANES - 1:SUBLANES - 1 + tm, :] * w[0:1] + x_ref[:, cols] * w[1:2]
            + s_ref[SUBLANES + 1:SUBLANES + 1 + tm, :] * w[2:3])


def _halo_specs(tm, width, col_blk, n_tok, nidx=1):
    r = tm // SUBLANES
    last = n_tok // SUBLANES - 1
    if nidx == 1:
        return [pl.BlockSpec((tm, width), lambda i: (i, col_blk)),
                pl.BlockSpec((SUBLANES, width), lambda i: (jnp.maximum(i * r - 1, 0), col_blk)),
                pl.BlockSpec((SUBLANES, width), lambda i: (jnp.minimum((i + 1) * r, last), col_blk))]
    return [pl.BlockSpec((tm, width), lambda i, j: (i, col_blk + j)),
            pl.BlockSpec((SUBLANES, width), lambda i, j: (jnp.maximum(i * r - 1, 0), col_blk + j)),
            pl.BlockSpec((SUBLANES, width), lambda i, j: (jnp.minimum((i + 1) * r, last), col_blk + j))]


def _cmlp_kernel(u_ref, v_ref, gain_ref, ws_ref, bs_ref, _mix_ref, o_ref, *, tm):
    v = v_ref[...]
    z = (v * lax.rsqrt(jnp.mean(v * v, axis=-1, keepdims=True) + NORM_EPS) * gain_ref[...]).astype(BF16)
    gw = z.shape[1] // CM_GROUPS
    for n in range(tm // CM_CHUNK):
        rows = slice(n * CM_CHUNK, (n + 1) * CM_CHUNK)
        for g in range(CM_GROUPS):
            cols = slice(g * gw, (g + 1) * gw)
            t = jnp.dot(ws_ref[g].astype(BF16), z[rows, cols], preferred_element_type=F32) + bs_ref[g]
            o_ref[rows, cols] = (u_ref[rows, cols] * t).astype(o_ref.dtype)


def _chunk_mlp(proj, mix, gain, ws, bs, *, u_blk, out_blk, tm=ROW_TM):
    n_tok = proj.shape[0]
    W = gain.shape[0]
    gw = W // CM_GROUPS
    bs_b = jnp.broadcast_to(bs[:, :, None], (CM_GROUPS, CM_CHUNK, gw))
    return pl.pallas_call(
        functools.partial(_cmlp_kernel, tm=tm),
        grid=(n_tok // tm,),
        in_specs=[pl.BlockSpec((tm, W), lambda i: (i, u_blk)),
                  pl.BlockSpec((tm, W), lambda i: (i, u_blk + 1)),
                  pl.BlockSpec((1, W), lambda i: (0, 0)),
                  pl.BlockSpec((CM_GROUPS, CM_CHUNK, CM_CHUNK), lambda i: (0, 0, 0)),
                  pl.BlockSpec((CM_GROUPS, CM_CHUNK, gw), lambda i: (0, 0, 0)),
                  _any_spec()],
        out_specs=pl.BlockSpec((tm, W), lambda i: (i, out_blk)),
        out_shape=jax.ShapeDtypeStruct(mix.shape, mix.dtype),
        input_output_aliases={5: 0},
        compiler_params=_params("parallel"),
        name="chunk_mlp",
    )(proj, proj, gain.reshape(1, W), ws, bs_b, mix)


def _ffn_act_kernel(a_ref, ap_ref, an_ref, b_ref, bp_ref, bn_ref, wa_ref, wb_ref, ba_ref, bb_ref, o_ref,
                    sa_ref, sb_ref, *, tm, tc, unroll, p_tok, tp, ts):
    keep_prev, keep_next = _tile_edges(pl.program_id(0) * tm, tm, p_tok, tp, ts)

    def chunk(j, slab):
        cols = pl.ds(pl.multiple_of(j * LANES, LANES), LANES)
        ga = _conv3(a_ref, ap_ref, an_ref, sa_ref.at[slab], wa_ref[:, cols], keep_prev, keep_next, cols)
        gb = _conv3(b_ref, bp_ref, bn_ref, sb_ref.at[slab], wb_ref[:, cols], keep_prev, keep_next, cols)
        ga = ga + ba_ref[:, cols]
        o_ref[:, cols] = (ga * jax.nn.sigmoid(ga) * (gb + bb_ref[:, cols])).astype(o_ref.dtype)

    n_chunks = tc // LANES

    def body(g, carry):
        for u in range(unroll):
            chunk(g * unroll + u, u)
        return carry

    lax.fori_loop(0, n_chunks // unroll, body, 0)
    for u in range(n_chunks % unroll):
        chunk(n_chunks - n_chunks % unroll + u, u)


def _ffn_act(up, conv_w, conv_b, *, p_tok, tp, ts, tm=ROW_TM, col_blocks=2):
    n_tok, two_f = up.shape
    F = two_f // 2
    tc = F // col_blocks
    assert tp % tm == 0 and ts % tm == 0 and tc % LANES == 0
    cb = conv_b.reshape(1, two_f)
    wspec = lambda off: pl.BlockSpec((CONV_W, tc), lambda i, j: (0, off + j))
    bspec = lambda off: pl.BlockSpec((1, tc), lambda i, j: (0, off + j))
    unroll = 4
    halo = pltpu.VMEM((unroll, tm + 2 * SUBLANES, LANES), F32)
    return pl.pallas_call(
        functools.partial(_ffn_act_kernel, tm=tm, tc=tc, unroll=unroll, p_tok=p_tok, tp=tp, ts=ts),
        grid=(n_tok // tm, col_blocks),
        in_specs=(_halo_specs(tm, tc, 0, n_tok, 2) + _halo_specs(tm, tc, col_blocks, n_tok, 2)
                  + [wspec(0), wspec(col_blocks), bspec(0), bspec(col_blocks)]),
        out_specs=pl.BlockSpec((tm, tc), lambda i, j: (i, j)),
        out_shape=jax.ShapeDtypeStruct((n_tok, F), BF16),
        scratch_shapes=[halo, halo],
        compiler_params=_params("parallel", "parallel"),
        name="ffn_act",
    )(up, up, up, up, up, up, conv_w, conv_w, cb, cb)


_DIMS = {"nn": ((1,), (0,)), "nt": ((1,), (1,)), "tn": ((0,), (0,))}


def _mm1(a, b, kind="nn"):
    return lax.dot_general(a, b, (_DIMS[kind], ((), ())), preferred_element_type=F32)


def _mmh(a, b, dims=(((1,), (0,)), ((), ()))):
    return lax.dot_general(a, b, dims, precision=HIGHEST, preferred_element_type=F32)


def _split3(x):
    hi = x.astype(BF16)
    r1 = x - hi.astype(F32)
    mid = r1.astype(BF16)
    return hi, mid, (r1 - mid.astype(F32)).astype(BF16)


def _head_sum(x):
    r = lax.broadcasted_iota(jnp.int32, (3 * LANES, LANES), 0) % LANES // RW_HEAD
    c = lax.broadcasted_iota(jnp.int32, (3 * LANES, LANES), 1) // RW_HEAD
    e3 = jnp.where(r == c, 1.0, 0.0).astype(BF16)
    parts = _split3(x)
    cols = [_mm1(jnp.concatenate([p[:, j * LANES:(j + 1) * LANES] for p in parts], axis=1), e3)
            for j in range(x.shape[1] // LANES)]
    return jnp.concatenate(cols, axis=1)


def _cumsum_rows(tri, x):
    t = tri.astype(BF16)
    return _mm1(jnp.concatenate([t, t, t], axis=1), jnp.concatenate(_split3(x), axis=0))


def _rw_prep_kernel(r_ref, rp_ref, rn_ref, k_ref, kp_ref, kn_ref, v_ref, vp_ref, vn_ref,
                    z_ref, zp_ref, zn_ref, cw_ref, cz_ref, w0_ref, w2_ref, a0_ref, a2_ref, g2_ref,
                    kk_ref, ka_ref, rk_ref,
                    ro_ref, kko_ref, vo_ref, lw_ref, b_ref, kd_ref, gate_ref, bonus_ref,
                    sr_ref, sk_ref, sv_ref, sz_ref, *, tm, p_tok, tp, ts, C):
    edges = _tile_edges(pl.program_id(0) * tm, tm, p_tok, tp, ts)

    def conv(x_ref, p_ref, n_ref, s_ref, w):
        chunks = []
        for j in range(x_ref.shape[1] // LANES):
            cols = slice(j * LANES, (j + 1) * LANES)
            chunks.append(_conv3(x_ref, p_ref, n_ref, s_ref.at[j], w[:, cols], *edges, cols))
        return jnp.concatenate(chunks, axis=1)

    cw = cw_ref[...]
    r = conv(r_ref, rp_ref, rn_ref, sr_ref, cw[:, 0:C])
    k = conv(k_ref, kp_ref, kn_ref, sk_ref, cw[:, C:2 * C])
    v = conv(v_ref, vp_ref, vn_ref, sv_ref, cw[:, 2 * C:3 * C])
    z = conv(z_ref, zp_ref, zn_ref, sz_ref, cz_ref[...])
    dec = jnp.tanh(z[:, 0:LANES]).astype(BF16)
    aa = z[:, LANES:2 * LANES].astype(BF16)
    gl = jax.nn.sigmoid(z[:, 2 * LANES:4 * LANES]).astype(BF16)
    gate_ref[...] = jnp.dot(gl, g2_ref[...].astype(BF16), preferred_element_type=F32)
    kk = k * kk_ref[...]
    kk = kk * lax.rsqrt(_head_sum(kk * kk) + 1e-12)
    ro_ref[...] = r
    kko_ref[...] = kk
    vo_ref[...] = v
    kd_sum = None
    for d in range(2):
        wl = w0_ref[d:d + 1, :] + jnp.dot(dec, w2_ref[d].astype(BF16), preferred_element_type=F32)
        lw_ref[d] = -math.exp(-0.5) * jax.nn.sigmoid(wl)
        a = jax.nn.sigmoid(a0_ref[d:d + 1, :] + jnp.dot(aa, a2_ref[d].astype(BF16), preferred_element_type=F32))
        b_ref[d] = kk * a
        kd = k * (1.0 + (a - 1.0) * ka_ref[...])
        kd_ref[d] = kd
        kd_sum = kd if kd_sum is None else kd_sum + kd
    bonus_ref[...] = _head_sum(r * kd_sum * rk_ref[...]) * v


def _rw_prep(proj, proj_z, lp, *, r_blk, z_blk, p_tok, tp, ts, tm=ROW_TM):
    assert tp % tm == 0 and ts % tm == 0
    n_tok = proj.shape[0]
    C = lp["rw_k_k"].shape[0]
    ZW = 4 * LANES
    full = lambda shape: pl.BlockSpec(shape, lambda i: (0,) * len(shape))
    tok = pl.BlockSpec((tm, C), lambda i: (i, 0))
    tok2 = pl.BlockSpec((2, tm, C), lambda i: (0, i, 0))
    one = jax.ShapeDtypeStruct((n_tok, C), F32)
    two = jax.ShapeDtypeStruct((2, n_tok, C), F32)
    return pl.pallas_call(
        functools.partial(_rw_prep_kernel, tm=tm, p_tok=p_tok, tp=tp, ts=ts, C=C),
        grid=(n_tok // tm,),
        in_specs=(_halo_specs(tm, C, r_blk, n_tok) + _halo_specs(tm, C, r_blk + 1, n_tok)
                  + _halo_specs(tm, C, r_blk + 2, n_tok) + _halo_specs(tm, ZW, z_blk, n_tok)
                  + [full((CONV_W, 3 * C)), full((CONV_W, ZW)), full((2, C)), full((2, LANES, C)),
                     full((2, C)), full((2, LANES, C)), full((2 * LANES, C)),
                     full((1, C)), full((1, C)), full((1, C))]),
        out_specs=[tok, tok, tok, tok2, tok2, tok2, tok, tok],
        out_shape=[one, one, one, two, two, two, one, one],
        scratch_shapes=[pltpu.VMEM((w // LANES, tm + 2 * SUBLANES, LANES), F32) for w in (C, C, C, ZW)],
        compiler_params=_params("parallel"),
        name="rwkv_prep",
    )(*([proj] * 9 + [proj_z] * 3), lp["cw_rkv"], lp["cw_z"], lp["rw_w0"], lp["w2_pad"], lp["rw_a0"], lp["a2_pad"],
      lp["g2_pad"], lp["rw_k_k"].reshape(1, C), lp["rw_k_a"].reshape(1, C), lp["rw_r_k"].reshape(1, C))


def _rw_scan_kernel(*refs, has_s0, nc):
    if has_s0:
        s0_ref, refs = refs[0], refs[1:]
    (rf_ref, kkf_ref, vf_ref, rb_ref, kkb_ref, vb_ref, lwf_ref, bf_ref, kdf_ref, lwb_ref, bb_ref, kdb_ref,
     yf_ref, yb_ref, so_ref, st_ref) = refs
    C, N = SCAN_CHUNK, RW_HEAD
    W = 2 * N
    npair = rf_ref.shape[1] // W
    c = pl.program_id(1)
    zero = jnp.zeros((), BF16)

    row = lax.broadcasted_iota(jnp.int32, (W, W), 0)
    col = lax.broadcasted_iota(jnp.int32, (W, W), 1)
    same_head = (row // N) == (col // N)
    eye = (row == col).astype(F32)

    @pl.when(c == 0)
    def _():
        if has_s0:
            sel = (lax.broadcasted_iota(jnp.int32, (N, W), 0) == lax.broadcasted_iota(jnp.int32, (N, W), 1) % N)
            for d in range(2):
                for p in range(npair):
                    tiled = _mmh(s0_ref[d, 2 * p:2 * p + 2].reshape(W, N), sel.astype(F32))
                    st_ref[d * npair + p] = jnp.where(same_head, tiled, 0.0)
        else:
            st_ref[...] = jnp.zeros(st_ref.shape, F32)

    def order(shape, dim, bwd):
        t = lax.broadcasted_iota(jnp.int32, shape, dim) % C
        return C - 1 - t if bwd else t

    chains = []
    for d, (r_ref, kk_ref, v_ref, lw_ref, b_ref, kd_ref, y_ref) in enumerate(
            [(rf_ref, kkf_ref, vf_ref, lwf_ref, bf_ref, kdf_ref, yf_ref),
             (rb_ref, kkb_ref, vb_ref, lwb_ref, bb_ref, kdb_ref, yb_ref)]):
        rt, ct = order((W, W), 0, d == 1), order((W, W), 1, d == 1)
        masks = dict(strict=rt > ct, incl=rt >= ct)
        masks.update({s: ((rt // (2 * s)) == (ct // (2 * s))) & ((rt // s) % 2 == 1) & ((ct // s) % 2 == 0)
                      for s in (1, 2, 4, 8, 16, 32)})
        lw = lw_ref[...]
        g_in = _cumsum_rows(order((C, C), 0, d == 1) >= order((C, C), 1, d == 1), lw)
        g_tot = jnp.sum(lw, axis=0, keepdims=True)
        e_neg = jnp.exp(-g_in)
        e_rem = jnp.exp(g_tot - g_in)
        tok = dict(
            kk=(kk_ref[...] * jnp.exp(g_in - lw)).astype(BF16), r=(r_ref[...] * jnp.exp(g_in)).astype(BF16),
            b=(b_ref[...] * e_neg).astype(BF16), kd=(kd_ref[...] * e_neg).astype(BF16),
            b_end=(b_ref[...] * e_rem).astype(BF16), kd_end=(kd_ref[...] * e_rem).astype(BF16),
            v=v_ref[...].astype(BF16), e_tot=jnp.exp(g_tot))
        for p in range(npair):
            chains.append((d * npair + p, slice(p * W, (p + 1) * W), tok, masks, y_ref))

    def expand(x, sl):
        return jnp.where(same_head, jnp.concatenate([x[:, sl]] * 2, axis=0), zero)

    ak = [expand(t["kk"], sl) for _, sl, t, _, _ in chains]
    bk = [jnp.concatenate([expand(t["b"], sl), expand(t["kd"], sl)], axis=0) for _, sl, t, _, _ in chains]
    vb = [expand(t["v"], sl) for _, sl, t, _, _ in chains]
    S = [st_ref[i] for i, *_ in chains]
    Sb = [x.astype(BF16) for x in S]
    n = range(len(chains))
    mk = [ch[3] for ch in chains]
    ar = [expand(t["r"], sl) for _, sl, t, _, _ in chains]
    lmn = [_mm1(jnp.concatenate([ak[i], ar[i]], axis=0), bk[i], "nt") for i in n]
    L = [jnp.where(mk[i]["strict"], lmn[i][:W, :W], 0.0) for i in n]
    Lb = [x.astype(BF16) for x in L]
    M = [jnp.where(mk[i]["strict"], lmn[i][:W, W:], 0.0).astype(BF16) for i in n]
    nbk = [jnp.concatenate([jnp.where(mk[i]["incl"], -lmn[i][W:, :W], 0.0),
                            jnp.where(mk[i]["incl"], lmn[i][W:, W:], 0.0)], axis=1).astype(BF16) for i in n]
    rhs = [(_mm1(ak[i], Sb[i], "nt") + _mm1(M[i], vb[i])).astype(BF16) for i in n]
    X = [eye - jnp.where(mk[i][1], L[i], 0.0) for i in n]
    for s in (2, 4, 8, 16, 32):
        Xb = [x.astype(BF16) for x in X]
        t = [_mm1(jnp.where(mk[i][s], Lb[i], zero), Xb[i]).astype(BF16) for i in n]
        X = [X[i] - _mm1(Xb[i], t[i]) for i in n]
    uv = [jnp.concatenate([_mm1(X[i].astype(BF16), rhs[i]).astype(BF16), vb[i]], axis=0) for i in n]
    for i, (slot, sl, tk, _, _) in enumerate(chains):
        ends = jnp.concatenate([-expand(tk["b_end"], sl), expand(tk["kd_end"], sl)], axis=0)
        st_ref[slot] = S[i] * tk["e_tot"][:, sl] + _mm1(uv[i], ends, "tn")
    for i, (_, sl, _, _, y_ref) in enumerate(chains):
        y = _mm1(ar[i], Sb[i], "nt") + _mm1(nbk[i], uv[i])
        y_ref[:, sl] = y[:C] + y[C:]

    @pl.when(c == nc - 1)
    def _():
        fold = (lax.broadcasted_iota(jnp.int32, (W, N), 0) % N == lax.broadcasted_iota(jnp.int32, (W, N), 1))
        for d in range(2):
            for p in range(npair):
                so_ref[d, 2 * p:2 * p + 2] = _mmh(st_ref[d * npair + p], fold.astype(F32)).reshape(2, N, N)


def _rw_scan(r, kk, v, lw, b, kd, *, row0, B, T, s0=None, layer=0):
    C = r.shape[1]
    H = C // RW_HEAD
    CH = SCAN_CHUNK
    nc = T // CH
    blk0 = row0 // CH
    fwd = pl.BlockSpec((CH, C), lambda bi, c: (blk0 + bi * nc + c, 0))
    bwd = pl.BlockSpec((CH, C), lambda bi, c: (blk0 + bi * nc + nc - 1 - c, 0))
    fwd2 = pl.BlockSpec((None, CH, C), lambda bi, c: (0, blk0 + bi * nc + c, 0))
    bwd2 = pl.BlockSpec((None, CH, C), lambda bi, c: (1, blk0 + bi * nc + nc - 1 - c, 0))
    in_specs = [fwd, fwd, fwd, bwd, bwd, bwd, fwd2, fwd2, fwd2, bwd2, bwd2, bwd2]
    args = [r, kk, v, r, kk, v, lw, b, kd, lw, b, kd]
    if s0 is not None:
        in_specs = [pl.BlockSpec((None, None, 2, H, RW_HEAD, RW_HEAD),
                                 lambda bi, c: (bi, layer, 0, 0, 0, 0))] + in_specs
        args = [s0] + args
    y_shape = jax.ShapeDtypeStruct((B * T, C), F32)
    return pl.pallas_call(
        functools.partial(_rw_scan_kernel, has_s0=s0 is not None, nc=nc),
        grid=(B, nc),
        in_specs=in_specs,
        out_specs=[pl.BlockSpec((CH, C), lambda bi, c: (bi * nc + c, 0)),
                   pl.BlockSpec((CH, C), lambda bi, c: (bi * nc + nc - 1 - c, 0)),
                   pl.BlockSpec((None, 2, H, RW_HEAD, RW_HEAD), lambda bi, c: (bi, 0, 0, 0, 0))],
        out_shape=[y_shape, y_shape, jax.ShapeDtypeStruct((B, 2, H, RW_HEAD, RW_HEAD), F32)],
        scratch_shapes=[pltpu.VMEM((H, 2 * RW_HEAD, 2 * RW_HEAD), F32)],
        compiler_params=_params("parallel", "arbitrary"),
        name="rwkv_scan_ctx" if s0 is not None else "rwkv_scan",
    )(*args)


def _rw_post_kernel(yf_ref, yb_ref, bonus_ref, gate_ref, g_ref, b_ref, _mix_ref, o_ref):
    y = yf_ref[...] + yb_ref[...]
    inv_n = 1.0 / RW_HEAD
    mu = _head_sum(y) * inv_n
    yc = y - mu
    var = _head_sum(yc * yc) * inv_n
    yn = yc * lax.rsqrt(var + RW_GN_EPS) * g_ref[...] + b_ref[...]
    o_ref[...] = ((yn + bonus_ref[...]) * gate_ref[...]).astype(o_ref.dtype)


def _rw_post(y_f, y_b, bonus, gate, mix, gn_g, gn_b, *, row0, out_blk, tm=ROW_TM):
    n_rows, C = y_f.shape
    r0 = row0 // tm
    own = pl.BlockSpec((tm, C), lambda i: (i, 0))
    tok = pl.BlockSpec((tm, C), lambda i: (r0 + i, 0))
    vec = pl.BlockSpec((1, C), lambda i: (0, 0))
    return pl.pallas_call(
        _rw_post_kernel,
        grid=(n_rows // tm,),
        in_specs=[own, own, tok, tok, vec, vec, _any_spec()],
        out_specs=pl.BlockSpec((tm, C), lambda i: (r0 + i, out_blk)),
        out_shape=jax.ShapeDtypeStruct(mix.shape, mix.dtype),
        input_output_aliases={6: 0},
        compiler_params=_params("parallel"),
        name="rwkv_post",
    )(y_f, y_b, bonus, gate, gn_g.reshape(1, C), gn_b.reshape(1, C), mix)


def _pad_rows(w, rows, at):
    return jnp.zeros((rows, w.shape[1]), w.dtype).at[at:at + w.shape[0]].set(w)


def _uvz_weights(w_in_b):
    L, D, _ = w_in_b.shape
    o_z = 3 * (D // 2) + 3 * (D // 4)
    n_z = 4 * RW_LORA_R + RW_GATE_R
    pad = jnp.zeros((L, D, 4 * LANES - n_z), BF16)
    return jnp.concatenate([w_in_b[:, :, o_z + n_z:], w_in_b[:, :, o_z:o_z + n_z], pad], axis=2)


def _layer_weights(l, D, rw_conv_w, rw_w2, rw_a2, rw_g2):
    C = D // 4
    zpad = 4 * LANES - (4 * RW_LORA_R + RW_GATE_R)
    cw = rw_conv_w[l]
    return dict(
        cw_rkv=cw[:, :3 * C],
        cw_z=jnp.concatenate([cw[:, 3 * C:], jnp.zeros((CONV_W, zpad), F32)], axis=1),
        w2_pad=jnp.stack([_pad_rows(rw_w2[l, d], LANES, d * RW_LORA_R) for d in range(2)]),
        a2_pad=jnp.stack([_pad_rows(rw_a2[l, d], LANES, d * RW_LORA_R) for d in range(2)]),
        g2_pad=_pad_rows(rw_g2[l], 2 * LANES, 0),
    )


def _rope_tables(T):
    n = DA_D // 4
    inv = ROPE_THETA ** (-jnp.arange(n, dtype=F32) / n)
    rows = T // GRID_W
    row = jnp.repeat(jnp.arange(rows), GRID_W).astype(F32)
    col = jnp.tile(jnp.arange(GRID_W), rows).astype(F32)
    sign = jnp.concatenate([-jnp.ones((n,), F32), jnp.ones((n,), F32)])
    cs, sn = [], []
    for pos in (row, col):
        ang = pos[:, None] * inv[None, :]
        cs.append(jnp.concatenate([jnp.cos(ang), jnp.cos(ang)], axis=1))
        sn.append(jnp.concatenate([jnp.sin(ang), jnp.sin(ang)], axis=1) * sign[None, :])
    cos = jnp.concatenate(cs, axis=1)
    sin = jnp.concatenate(sn, axis=1)
    return jnp.tile(cos, (1, 2)), jnp.tile(sin, (1, 2))


def kernel(x_prompt, x_sample, cache_da_k, cache_da_v, state_rwkv, c, c_ctx, mod_w, mod_b, norm1_g, norm2_g, w_in, da_lambda, da_subln_g, rw_conv_w, rw_w0, rw_w2, rw_a0, rw_a2, rw_g2, rw_k_k, rw_k_a, rw_r_k, rw_gn_g, rw_gn_b, cm_norm_g, cm_ws, cm_bs, w_out, ffn_up, ffn_conv_w, ffn_conv_b, ffn_down, final_norm_g):
    Bp, Tp, D = x_prompt.shape
    Bs, Ts, _ = x_sample.shape
    L = mod_w.shape[0]
    past = cache_da_k.shape[2]
    DA = D // 2
    H = DA // (2 * DA_D)
    C = D // 4
    F = ffn_down.shape[1]
    p_tok, s_tok = Bp * Tp, Bs * Ts
    assert Bs + 1 <= SUBLANES and p_tok % Ts == 0

    xs = (x_prompt.reshape(p_tok, D), x_sample.reshape(s_tok, D))
    cond8 = jnp.concatenate([c_ctx[None, :], c, jnp.zeros((SUBLANES - 1 - Bs, D), F32)], axis=0)
    mod = _modulation(cond8, mod_w, mod_b)
    ck4 = cache_da_k.reshape(Bs, L, past, DA)
    cv4 = cache_da_v.reshape(Bs, L, past, DA)
    cos, sin = _rope_tables(Ts)
    ko = jnp.zeros((Bp, L, Tp, DA), F32)
    vo = jnp.zeros((Bp, L, Tp, DA), F32)
    w_in_b, ffn_down_b = w_in.astype(BF16), ffn_down.astype(BF16)
    n_qkv_rkv = 3 * DA + 3 * C
    w_uvz = _uvz_weights(w_in_b)

    new_s = []
    for l in range(L):
        lw_ = _layer_weights(l, D, rw_conv_w, rw_w2, rw_a2, rw_g2)
        lp = dict(lw_, rw_w0=rw_w0[l], rw_a0=rw_a0[l], rw_k_k=rw_k_k[l], rw_k_a=rw_k_a[l], rw_r_k=rw_r_k[l])
        lam_init = 0.8 - 0.6 * math.exp(-0.3 * l)

        h = _norm_mod(xs, norm1_g[l], mod, l, 0, 1, p_tok, Ts)
        proj = _matmul(h, w_in_b, layer=l, n_cols=n_qkv_rkv, name="proj_in")
        proj_uz = _matmul(h, w_uvz, layer=l, name="proj_in_uvz")
        ko, vo = _store_kv(proj, ko, vo, l, Bp=Bp, Tp=Tp, DA=DA)

        mix = jnp.zeros((p_tok + s_tok, D), BF16)
        mix = _attention(proj, mix, da_lambda[l], da_subln_g[l], lam_init, row0=0, B=Bp, T=Tp, H=H)
        mix = _attention(proj, mix, da_lambda[l], da_subln_g[l], lam_init, row0=p_tok, B=Bs, T=Ts, H=H,
                         ctx=(ck4, cv4, l, cos, sin))
        r_, kk_, v_, lg_, b_, kd_, gate_, bonus_ = _rw_prep(proj, proj_uz, lp, r_blk=3 * DA // C,
                                                            z_blk=2 * C // (4 * LANES), p_tok=p_tok, tp=Tp, ts=Ts)
        yf_p, yb_p, s_p = _rw_scan(r_, kk_, v_, lg_, b_, kd_, row0=0, B=Bp, T=Tp)
        yf_s, yb_s, _ = _rw_scan(r_, kk_, v_, lg_, b_, kd_, row0=p_tok, B=Bs, T=Ts, s0=state_rwkv, layer=l)
        mix = _rw_post(yf_p, yb_p, bonus_, gate_, mix, rw_gn_g[l], rw_gn_b[l], row0=0, out_blk=DA // C)
        mix = _rw_post(yf_s, yb_s, bonus_, gate_, mix, rw_gn_g[l], rw_gn_b[l], row0=p_tok, out_blk=DA // C)
        mix = _chunk_mlp(proj_uz, mix, cm_norm_g[l], cm_ws[l], cm_bs[l], u_blk=0, out_blk=DA // C + 1)

        x = _matmul(mix, w_out, layer=l, resid=(xs, mod, l, 2 * D // MM_TN, p_tok, Ts), name="proj_out")
        xs = (x,)

        h = _norm_mod(xs, norm2_g[l], mod, l, 3, 4, p_tok, Ts)
        up = _matmul(h, ffn_up, layer=l, name="ffn_up")
        act = _ffn_act(up, ffn_conv_w[l], ffn_conv_b[l], p_tok=p_tok, tp=Tp, ts=Ts)
        x = _matmul(act, ffn_down_b, layer=l, tk=F // 2, resid=(xs, mod, l, 5 * D // MM_TN, p_tok, Ts),
                    name="ffn_down")
        xs = (x,)

        new_s.append(s_p)

    y_p = _final_norm(x, final_norm_g, row0=0, n_rows=p_tok)
    y_s = _final_norm(x, final_norm_g, row0=p_tok, n_rows=s_tok)
    return (y_p.reshape(Bp, Tp, D), y_s.reshape(Bs, Ts, D), ko.reshape(Bp, L, Tp, H, 2, DA_D),
            vo.reshape(Bp, L, Tp, H, 2 * DA_D), jnp.stack(new_s, axis=1))
```

```python
import functools
import math

import jax
import jax.numpy as jnp
from jax import lax
from jax.experimental import pallas as pl
from jax.experimental.pallas import tpu as pltpu

F32 = jnp.float32
BF16 = jnp.bfloat16
HIGHEST = lax.Precision.HIGHEST

LANES = 128
SUBLANES = 8
VMEM_BYTES_V7X = 64 * 1024 * 1024
VMEM_BUDGET = VMEM_BYTES_V7X * 3 // 4

GRID_W = 64
DA_D = 128
RW_HEAD = 64
RW_LORA_R = 64
RW_GATE_R = 160
RW_GN_EPS = 64e-5
CM_GROUPS = 4
CM_CHUNK = 128
CONV_W = 3
ROPE_THETA = 10000.0
NORM_EPS = 1e-6
SCAN_CHUNK = 64

MM_TM, MM_TN = 1024, 512
ROW_TM = 256
ATT_TQ, ATT_KEYS = 512, 512


def _params(*sem):
    return pltpu.CompilerParams(dimension_semantics=sem, vmem_limit_bytes=VMEM_BUDGET)


def _any_spec():
    return pl.BlockSpec(memory_space=pl.ANY)


def _cond_row(tok0, p_tok, ts):
    return jnp.where(tok0 < p_tok, 0, 1 + jnp.maximum(tok0 - p_tok, 0) // ts)


def _mod_kernel(c_ref, w_ref, b_ref, o_ref):
    c = c_ref[...]
    s = (c * jax.nn.sigmoid(c)).astype(BF16)
    o_ref[...] = jnp.dot(s, w_ref[...].astype(BF16), preferred_element_type=F32) + b_ref[...]


def _modulation(cond8, mod_w, mod_b, tn=MM_TN):
    L, D, N = mod_w.shape
    return pl.pallas_call(
        _mod_kernel,
        grid=(L, N // tn),
        in_specs=[pl.BlockSpec((SUBLANES, D), lambda l, j: (0, 0)),
                  pl.BlockSpec((None, D, tn), lambda l, j: (l, 0, j)),
                  pl.BlockSpec((None, 1, tn), lambda l, j: (l, 0, j))],
        out_specs=pl.BlockSpec((None, SUBLANES, tn), lambda l, j: (l, 0, j)),
        out_shape=jax.ShapeDtypeStruct((L, SUBLANES, N), F32),
        compiler_params=_params("parallel", "parallel"),
        name="modulation",
    )(cond8, mod_w, mod_b.reshape(L, 1, N))


def _rows(refs, first_tiles):
    if len(refs) == 1:
        return refs[0][...]
    return jnp.where(pl.program_id(0) < first_tiles, refs[0][...], refs[1][...])


def _split_specs(xs, tm, block, idx):
    if len(xs) == 1:
        return [pl.BlockSpec(block, idx(lambda i: i))], 0
    first = xs[0].shape[0] // tm
    last = xs[1].shape[0] // tm - 1
    return [pl.BlockSpec(block, idx(lambda i: jnp.minimum(i, first - 1))),
            pl.BlockSpec(block, idx(lambda i: jnp.clip(i - first, 0, last)))], first


def _norm_mod_kernel(*refs, n_x, x_split, tm, p_tok, ts):
    x_refs, (g_ref, sh_ref, sc_ref, o_ref) = refs[:n_x], refs[n_x:]
    row = _cond_row(pl.program_id(0) * tm, p_tok, ts)
    x = _rows(x_refs, x_split)
    y = x * lax.rsqrt(jnp.mean(x * x, axis=-1, keepdims=True) + NORM_EPS)
    sc = sc_ref[pl.ds(row, 1), :]
    sh = sh_ref[pl.ds(row, 1), :]
    o_ref[...] = ((y * g_ref[...]) * (1.0 + sc) + sh).astype(o_ref.dtype)


def _norm_mod(xs, g, mod, layer, k_sh, k_sc, p_tok, ts, tm=ROW_TM):
    n = sum(x.shape[0] for x in xs)
    D = xs[0].shape[1]
    assert p_tok % tm == 0 and ts % tm == 0, "a row tile must not mix modulation groups"
    x_specs, x_split = _split_specs(xs, tm, (tm, D), lambda r: (lambda i: (r(i), 0)))
    return pl.pallas_call(
        functools.partial(_norm_mod_kernel, n_x=len(xs), x_split=x_split, tm=tm, p_tok=p_tok, ts=ts),
        grid=(n // tm,),
        in_specs=x_specs + [pl.BlockSpec((1, D), lambda i: (0, 0)),
                            pl.BlockSpec((None, SUBLANES, D), lambda i: (layer, 0, k_sh)),
                            pl.BlockSpec((None, SUBLANES, D), lambda i: (layer, 0, k_sc))],
        out_specs=pl.BlockSpec((tm, D), lambda i: (i, 0)),
        out_shape=jax.ShapeDtypeStruct((n, D), BF16),
        compiler_params=_params("parallel"),
        name="norm_mod",
    )(*xs, g.reshape(1, D), mod, mod)


def _final_norm_kernel(x_ref, g_ref, o_ref):
    x = x_ref[...]
    o_ref[...] = x * lax.rsqrt(jnp.mean(x * x, axis=-1, keepdims=True) + NORM_EPS) * g_ref[...]


def _final_norm(x, g, *, row0, n_rows, tm=ROW_TM):
    D = x.shape[1]
    r0 = row0 // tm
    return pl.pallas_call(
        _final_norm_kernel,
        grid=(n_rows // tm,),
        in_specs=[pl.BlockSpec((tm, D), lambda i: (r0 + i, 0)), pl.BlockSpec((1, D), lambda i: (0, 0))],
        out_specs=pl.BlockSpec((tm, D), lambda i: (i, 0)),
        out_shape=jax.ShapeDtypeStruct((n_rows, D), F32),
        compiler_params=_params("parallel"),
        name="final_norm",
    )(x, g.reshape(1, D))


def _mm_kernel(*refs, nk, n_x, x_split, tm, p_tok, ts):
    a_ref, b_ref = refs[:2]
    x_refs, g_ref = refs[2:2 + n_x], (refs[2 + n_x] if n_x else None)
    o_ref = refs[3 + n_x] if n_x else refs[2]
    acc_ref = refs[-1] if nk > 1 else None

    def finish(acc):
        if n_x:
            row = _cond_row(pl.program_id(0) * tm, p_tok, ts)
            o_ref[...] = _rows(x_refs, x_split) + g_ref[pl.ds(row, 1), :] * acc
        else:
            o_ref[...] = acc

    part = jnp.dot(a_ref[...], b_ref[...].astype(BF16), preferred_element_type=F32)
    if nk == 1:
        finish(part)
    else:
        k = pl.program_id(2)

        @pl.when(k == 0)
        def _():
            acc_ref[...] = part

        @pl.when((k > 0) & (k < nk - 1))
        def _():
            acc_ref[...] += part

        @pl.when(k == nk - 1)
        def _():
            finish(acc_ref[...] + part)


def _matmul(a, b, *, tm=MM_TM, tn=MM_TN, tk=None, layer=None, n_cols=None, resid=None, name):
    M, K = a.shape
    N = b.shape[-1] if n_cols is None else n_cols
    tk = K if tk is None else tk
    nk = K // tk
    if layer is None:
        b_spec = pl.BlockSpec((tk, tn), lambda i, j, k: (k, j))
    else:
        b_spec = pl.BlockSpec((None, tk, tn), lambda i, j, k: (layer, k, j))
    in_specs = [pl.BlockSpec((tm, tk), lambda i, j, k: (i, k)), b_spec]
    args = [a, b]
    p_tok = ts = x_split = n_x = 0
    if resid is not None:
        xs, mod, mod_layer, gate_blk, p_tok, ts = resid
        assert p_tok % tm == 0 and ts % tm == 0, "a row tile must not mix modulation groups"
        x_specs, x_split = _split_specs(xs, tm, (tm, tn), lambda r: (lambda i, j, k: (r(i), j)))
        n_x = len(xs)
        in_specs += x_specs + [pl.BlockSpec((None, SUBLANES, tn), lambda i, j, k: (mod_layer, 0, gate_blk + j))]
        args += list(xs) + [mod]
    return pl.pallas_call(
        functools.partial(_mm_kernel, nk=nk, n_x=n_x, x_split=x_split, tm=tm, p_tok=p_tok, ts=ts),
        grid=(M // tm, N // tn, nk),
        in_specs=in_specs,
        out_specs=pl.BlockSpec((tm, tn), lambda i, j, k: (i, j)),
        out_shape=jax.ShapeDtypeStruct((M, N), F32),
        scratch_shapes=[pltpu.VMEM((tm, tn), F32)] if nk > 1 else [],
        compiler_params=_params("parallel", "parallel", "arbitrary"),
        name=name,
    )(*args)


def _store_kv_kernel(k_ref, v_ref, _ko_in, _vo_in, ko_ref, vo_ref):
    ko_ref[...] = k_ref[...]
    vo_ref[...] = v_ref[...]


def _store_kv(proj, ko, vo, layer, *, Bp, Tp, DA):
    out = pl.BlockSpec((None, None, Tp, DA), lambda b: (b, layer, 0, 0))
    return pl.pallas_call(
        _store_kv_kernel,
        grid=(Bp,),
        in_specs=[pl.BlockSpec((Tp, DA), lambda b: (b, 1)), pl.BlockSpec((Tp, DA), lambda b: (b, 2)),
                  _any_spec(), _any_spec()],
        out_specs=[out, out],
        out_shape=[jax.ShapeDtypeStruct(ko.shape, ko.dtype), jax.ShapeDtypeStruct(vo.shape, vo.dtype)],
        input_output_aliases={2: 0, 3: 1},
        compiler_params=_params("parallel"),
        name="store_kv",
    )(proj, proj, ko, vo)


def _rope(x, cos, sin_signed):
    lane = lax.broadcasted_iota(jnp.int32, x.shape, 1)
    width = x.shape[1]
    rot = jnp.where((lane % 64) < 32, pltpu.roll(x, width - 32, 1), pltpu.roll(x, 32, 1))
    return x * cos + rot * sin_signed


def _attn_kernel(*refs, rope, lam_init, key_block):
    if rope:
        (lam_ref, q_ref, k_ref, v_ref, kc_ref, vc_ref, cq_ref, sq_ref, ck_ref, sk_ref, g_ref, _mix_ref,
         o_ref, kr_ref, vr_ref) = refs
    else:
        lam_ref, q_ref, k_ref, v_ref, g_ref, _mix_ref, o_ref = refs
    lm = lam_ref[...]
    s1 = jnp.sum(lm[0:1] * lm[1:2], axis=-1, keepdims=True)
    s2 = jnp.sum(lm[2:3] * lm[3:4], axis=-1, keepdims=True)
    lam = jnp.exp(s1) - jnp.exp(s2) + lam_init
    qscale = DA_D ** -0.5 * math.log2(math.e)

    if rope:
        @pl.when(pl.program_id(2) == 0)
        def _():
            kr_ref[...] = _rope(k_ref[...], ck_ref[...], sk_ref[...]).astype(BF16)
            vr_ref[...] = v_ref[...].astype(BF16)

        q = (_rope(q_ref[...], cq_ref[...], sq_ref[...]) * qscale).astype(BF16)
        T = kr_ref.shape[0]
        blocks = [(kr_ref[j * key_block:(j + 1) * key_block], vr_ref[j * key_block:(j + 1) * key_block])
                  for j in range(T // key_block)]
        blocks.append((kc_ref[...].astype(BF16), vc_ref[...].astype(BF16)))
    else:
        q = (q_ref[...] * qscale).astype(BF16)
        blocks = [(k_ref[...].astype(BF16), v_ref[...].astype(BF16))]

    cols = [slice(c * DA_D, (c + 1) * DA_D) for c in range(2)]
    ss = [[_mm1(q[:, cols[c]], kb[:, cols[c]], "nt") for c in range(2)] for kb, _ in blocks]
    m = [functools.reduce(jnp.maximum, [jnp.max(s[c], axis=-1, keepdims=True) for s in ss]) for c in range(2)]
    o = None
    d = [None, None]
    tq = q.shape[0]
    for s, (_, vb) in zip(ss, blocks):
        e = [jnp.exp2(s[c] - m[c]) for c in range(2)]
        for c in range(2):
            t = jnp.sum(e[c], axis=-1, keepdims=True)
            d[c] = t if d[c] is None else d[c] + t
        u = _mm1(jnp.concatenate([e[0].astype(BF16), e[1].astype(BF16)], axis=0), vb)
        o = u if o is None else o + u
    o = [o[:tq], o[tq:]]
    maps = list(zip(o, d))
    o = maps[0][0] * (1.0 / maps[0][1]) - maps[1][0] * (lam / maps[1][1])
    y = o * lax.rsqrt(jnp.mean(o * o, axis=-1, keepdims=True) + NORM_EPS)
    o_ref[...] = (y * g_ref[...] * (1.0 - lam_init)).astype(o_ref.dtype)


def _attention(proj, mix, lam_p, g, lam_init, *, row0, B, T, H, ctx=None, tq=ATT_TQ, key_block=ATT_KEYS):
    W = 2 * DA_D
    tq = min(tq, T)
    nq = T // tq
    qb0 = row0 // tq
    kb0 = row0 // T
    rope = ctx is not None
    in_specs = [pl.BlockSpec((4, DA_D), lambda b, h, i: (0, 0)),
                pl.BlockSpec((tq, W), lambda b, h, i: (qb0 + b * nq + i, h)),
                pl.BlockSpec((T, W), lambda b, h, i: (kb0 + b, H + h)),
                pl.BlockSpec((T, W), lambda b, h, i: (kb0 + b, 2 * H + h))]
    args = [lam_p, proj, proj, proj]
    scratch = []
    if rope:
        ck, cv, layer, cos, sin = ctx
        past = ck.shape[2]
        in_specs += [pl.BlockSpec((None, None, past, W), lambda b, h, i: (b, layer, 0, h)),
                     pl.BlockSpec((None, None, past, W), lambda b, h, i: (b, layer, 0, h)),
                     pl.BlockSpec((tq, W), lambda b, h, i: (i, 0)),
                     pl.BlockSpec((tq, W), lambda b, h, i: (i, 0)),
                     pl.BlockSpec((T, W), lambda b, h, i: (0, 0)),
                     pl.BlockSpec((T, W), lambda b, h, i: (0, 0))]
        args += [ck, cv, cos, sin, cos, sin]
        scratch = [pltpu.VMEM((T, W), BF16), pltpu.VMEM((T, W), BF16)]
    in_specs += [pl.BlockSpec((1, W), lambda b, h, i: (0, 0)), _any_spec()]
    args += [g.reshape(1, W), mix]
    return pl.pallas_call(
        functools.partial(_attn_kernel, rope=rope, lam_init=lam_init, key_block=key_block),
        grid=(B, H, nq),
        in_specs=in_specs,
        out_specs=pl.BlockSpec((tq, W), lambda b, h, i: (qb0 + b * nq + i, h)),
        out_shape=jax.ShapeDtypeStruct(mix.shape, mix.dtype),
        input_output_aliases={len(args) - 1: 0},
        scratch_shapes=scratch,
        compiler_params=_params("parallel", "parallel", "arbitrary"),
        name="diff_attn_ctx" if rope else "diff_attn",
    )(*args)


def _tile_edges(tok0, tm, p_tok, tp, ts):
    in_p = tok0 < p_tok
    pos = jnp.where(in_p, tok0 % tp, jnp.maximum(tok0 - p_tok, 0) % ts)
    length = jnp.where(in_p, tp, ts)
    return (pos != 0).astype(F32), (pos + tm != length).astype(F32)


def _conv3(x_ref, p_ref, n_ref, s_ref, w, keep_prev, keep_next, cols=slice(None)):
    tm = x_ref.shape[0]
    s_ref[SUBLANES:SUBLANES + tm, :] = x_ref[:, cols]
    s_ref[SUBLANES - 1:SUBLANES, :] = p_ref[SUBLANES - 1:SUBLANES, cols] * keep_prev
    s_ref[SUBLANES + tm:SUBLANES + tm + 1, :] = n_ref[0:1, cols] * keep_next
    return (s_ref[SUBLANES - 1:SUBLANES - 1 + tm, :] * w[0:1] + x_ref[:, cols] * w[1:2]
            + s_ref[SUBLANES + 1:SUBLANES + 1 + tm, :] * w[2:3])


def _halo_specs(tm, width, col_blk, n_tok, nidx=1):
    r = tm // SUBLANES
    last = n_tok // SUBLANES - 1
    if nidx == 1:
        return [pl.BlockSpec((tm, width), lambda i: (i, col_blk)),
                pl.BlockSpec((SUBLANES, width), lambda i: (jnp.maximum(i * r - 1, 0), col_blk)),
                pl.BlockSpec((SUBLANES, width), lambda i: (jnp.minimum((i + 1) * r, last), col_blk))]
    return [pl.BlockSpec((tm, width), lambda i, j: (i, col_blk + j)),
            pl.BlockSpec((SUBLANES, width), lambda i, j: (jnp.maximum(i * r - 1, 0), col_blk + j)),
            pl.BlockSpec((SUBLANES, width), lambda i, j: (jnp.minimum((i + 1) * r, last), col_blk + j))]


def _cmlp_kernel(u_ref, v_ref, gain_ref, ws_ref, bs_ref, _mix_ref, o_ref, *, tm):
    v = v_ref[...]
    z = (v * lax.rsqrt(jnp.mean(v * v, axis=-1, keepdims=True) + NORM_EPS) * gain_ref[...]).astype(BF16)
    gw = z.shape[1] // CM_GROUPS
    for n in range(tm // CM_CHUNK):
        rows = slice(n * CM_CHUNK, (n + 1) * CM_CHUNK)
        for g in range(CM_GROUPS):
            cols = slice(g * gw, (g + 1) * gw)
            t = jnp.dot(ws_ref[g].astype(BF16), z[rows, cols], preferred_element_type=F32) + bs_ref[g]
            o_ref[rows, cols] = (u_ref[rows, cols] * t).astype(o_ref.dtype)


def _chunk_mlp(proj, mix, gain, ws, bs, *, u_blk, out_blk, tm=ROW_TM):
    n_tok = proj.shape[0]
    W = gain.shape[0]
    gw = W // CM_GROUPS
    bs_b = jnp.broadcast_to(bs[:, :, None], (CM_GROUPS, CM_CHUNK, gw))
    return pl.pallas_call(
        functools.partial(_cmlp_kernel, tm=tm),
        grid=(n_tok // tm,),
        in_specs=[pl.BlockSpec((tm, W), lambda i: (i, u_blk)),
                  pl.BlockSpec((tm, W), lambda i: (i, u_blk + 1)),
                  pl.BlockSpec((1, W), lambda i: (0, 0)),
                  pl.BlockSpec((CM_GROUPS, CM_CHUNK, CM_CHUNK), lambda i: (0, 0, 0)),
                  pl.BlockSpec((CM_GROUPS, CM_CHUNK, gw), lambda i: (0, 0, 0)),
                  _any_spec()],
        out_specs=pl.BlockSpec((tm, W), lambda i: (i, out_blk)),
        out_shape=jax.ShapeDtypeStruct(mix.shape, mix.dtype),
        input_output_aliases={5: 0},
        compiler_params=_params("parallel"),
        name="chunk_mlp",
    )(proj, proj, gain.reshape(1, W), ws, bs_b, mix)


def _ffn_act_kernel(a_ref, ap_ref, an_ref, b_ref, bp_ref, bn_ref, wa_ref, wb_ref, ba_ref, bb_ref, o_ref,
                    sa_ref, sb_ref, *, tm, tc, unroll, p_tok, tp, ts):
    keep_prev, keep_next = _tile_edges(pl.program_id(0) * tm, tm, p_tok, tp, ts)

    def chunk(j, slab):
        cols = pl.ds(pl.multiple_of(j * LANES, LANES), LANES)
        ga = _conv3(a_ref, ap_ref, an_ref, sa_ref.at[slab], wa_ref[:, cols], keep_prev, keep_next, cols)
        gb = _conv3(b_ref, bp_ref, bn_ref, sb_ref.at[slab], wb_ref[:, cols], keep_prev, keep_next, cols)
        ga = ga + ba_ref[:, cols]
        o_ref[:, cols] = (ga * jax.nn.sigmoid(ga) * (gb + bb_ref[:, cols])).astype(o_ref.dtype)

    n_chunks = tc // LANES

    def body(g, carry):
        for u in range(unroll):
            chunk(g * unroll + u, u)
        return carry

    lax.fori_loop(0, n_chunks // unroll, body, 0)
    for u in range(n_chunks % unroll):
        chunk(n_chunks - n_chunks % unroll + u, u)


def _ffn_act(up, conv_w, conv_b, *, p_tok, tp, ts, tm=ROW_TM, col_blocks=2):
    n_tok, two_f = up.shape
    F = two_f // 2
    tc = F // col_blocks
    assert tp % tm == 0 and ts % tm == 0 and tc % LANES == 0
    cb = conv_b.reshape(1, two_f)
    wspec = lambda off: pl.BlockSpec((CONV_W, tc), lambda i, j: (0, off + j))
    bspec = lambda off: pl.BlockSpec((1, tc), lambda i, j: (0, off + j))
    unroll = 4
    halo = pltpu.VMEM((unroll, tm + 2 * SUBLANES, LANES), F32)
    return pl.pallas_call(
        functools.partial(_ffn_act_kernel, tm=tm, tc=tc, unroll=unroll, p_tok=p_tok, tp=tp, ts=ts),
        grid=(n_tok // tm, col_blocks),
        in_specs=(_halo_specs(tm, tc, 0, n_tok, 2) + _halo_specs(tm, tc, col_blocks, n_tok, 2)
                  + [wspec(0), wspec(col_blocks), bspec(0), bspec(col_blocks)]),
        out_specs=pl.BlockSpec((tm, tc), lambda i, j: (i, j)),
        out_shape=jax.ShapeDtypeStruct((n_tok, F), BF16),
        scratch_shapes=[halo, halo],
        compiler_params=_params("parallel", "parallel"),
        name="ffn_act",
    )(up, up, up, up, up, up, conv_w, conv_w, cb, cb)


_DIMS = {"nn": ((1,), (0,)), "nt": ((1,), (1,)), "tn": ((0,), (0,))}


def _mm1(a, b, kind="nn"):
    return lax.dot_general(a, b, (_DIMS[kind], ((), ())), preferred_element_type=F32)


def _mmh(a, b, dims=(((1,), (0,)), ((), ()))):
    return lax.dot_general(a, b, dims, precision=HIGHEST, preferred_element_type=F32)


def _split3(x):
    hi = x.astype(BF16)
    r1 = x - hi.astype(F32)
    mid = r1.astype(BF16)
    return hi, mid, (r1 - mid.astype(F32)).astype(BF16)


def _head_sum(x):
    r = lax.broadcasted_iota(jnp.int32, (3 * LANES, LANES), 0) % LANES // RW_HEAD
    c = lax.broadcasted_iota(jnp.int32, (3 * LANES, LANES), 1) // RW_HEAD
    e3 = jnp.where(r == c, 1.0, 0.0).astype(BF16)
    parts = _split3(x)
    cols = [_mm1(jnp.concatenate([p[:, j * LANES:(j + 1) * LANES] for p in parts], axis=1), e3)
            for j in range(x.shape[1] // LANES)]
    return jnp.concatenate(cols, axis=1)


def _cumsum_rows(tri, x):
    t = tri.astype(BF16)
    return _mm1(jnp.concatenate([t, t, t], axis=1), jnp.concatenate(_split3(x), axis=0))


def _rw_prep_kernel(r_ref, rp_ref, rn_ref, k_ref, kp_ref, kn_ref, v_ref, vp_ref, vn_ref,
                    z_ref, zp_ref, zn_ref, cw_ref, cz_ref, w0_ref, w2_ref, a0_ref, a2_ref, g2_ref,
                    kk_ref, ka_ref, rk_ref,
                    ro_ref, kko_ref, vo_ref, lw_ref, b_ref, kd_ref, gate_ref, bonus_ref,
                    sr_ref, sk_ref, sv_ref, sz_ref, *, tm, p_tok, tp, ts, C):
    edges = _tile_edges(pl.program_id(0) * tm, tm, p_tok, tp, ts)

    def conv(x_ref, p_ref, n_ref, s_ref, w):
        chunks = []
        for j in range(x_ref.shape[1] // LANES):
            cols = slice(j * LANES, (j + 1) * LANES)
            chunks.append(_conv3(x_ref, p_ref, n_ref, s_ref.at[j], w[:, cols], *edges, cols))
        return jnp.concatenate(chunks, axis=1)

    cw = cw_ref[...]
    r = conv(r_ref, rp_ref, rn_ref, sr_ref, cw[:, 0:C])
    k = conv(k_ref, kp_ref, kn_ref, sk_ref, cw[:, C:2 * C])
    v = conv(v_ref, vp_ref, vn_ref, sv_ref, cw[:, 2 * C:3 * C])
    z = conv(z_ref, zp_ref, zn_ref, sz_ref, cz_ref[...])
    dec = jnp.tanh(z[:, 0:LANES]).astype(BF16)
    aa = z[:, LANES:2 * LANES].astype(BF16)
    gl = jax.nn.sigmoid(z[:, 2 * LANES:4 * LANES]).astype(BF16)
    gate_ref[...] = jnp.dot(gl, g2_ref[...].astype(BF16), preferred_element_type=F32)
    kk = k * kk_ref[...]
    kk = kk * lax.rsqrt(_head_sum(kk * kk) + 1e-12)
    ro_ref[...] = r
    kko_ref[...] = kk
    vo_ref[...] = v
    kd_sum = None
    for d in range(2):
        wl = w0_ref[d:d + 1, :] + jnp.dot(dec, w2_ref[d].astype(BF16), preferred_element_type=F32)
        lw_ref[d] = -math.exp(-0.5) * jax.nn.sigmoid(wl)
        a = jax.nn.sigmoid(a0_ref[d:d + 1, :] + jnp.dot(aa, a2_ref[d].astype(BF16), preferred_element_type=F32))
        b_ref[d] = kk * a
        kd = k * (1.0 + (a - 1.0) * ka_ref[...])
        kd_ref[d] = kd
        kd_sum = kd if kd_sum is None else kd_sum + kd
    bonus_ref[...] = _head_sum(r * kd_sum * rk_ref[...]) * v


def _rw_prep(proj, proj_z, lp, *, r_blk, z_blk, p_tok, tp, ts, tm=ROW_TM):
    assert tp % tm == 0 and ts % tm == 0
    n_tok = proj.shape[0]
    C = lp["rw_k_k"].shape[0]
    ZW = 4 * LANES
    full = lambda shape: pl.BlockSpec(shape, lambda i: (0,) * len(shape))
    tok = pl.BlockSpec((tm, C), lambda i: (i, 0))
    tok2 = pl.BlockSpec((2, tm, C), lambda i: (0, i, 0))
    one = jax.ShapeDtypeStruct((n_tok, C), F32)
    two = jax.ShapeDtypeStruct((2, n_tok, C), F32)
    return pl.pallas_call(
        functools.partial(_rw_prep_kernel, tm=tm, p_tok=p_tok, tp=tp, ts=ts, C=C),
        grid=(n_tok // tm,),
        in_specs=(_halo_specs(tm, C, r_blk, n_tok) + _halo_specs(tm, C, r_blk + 1, n_tok)
                  + _halo_specs(tm, C, r_blk + 2, n_tok) + _halo_specs(tm, ZW, z_blk, n_tok)
                  + [full((CONV_W, 3 * C)), full((CONV_W, ZW)), full((2, C)), full((2, LANES, C)),
                     full((2, C)), full((2, LANES, C)), full((2 * LANES, C)),
                     full((1, C)), full((1, C)), full((1, C))]),
        out_specs=[tok, tok, tok, tok2, tok2, tok2, tok, tok],
        out_shape=[one, one, one, two, two, two, one, one],
        scratch_shapes=[pltpu.VMEM((w // LANES, tm + 2 * SUBLANES, LANES), F32) for w in (C, C, C, ZW)],
        compiler_params=_params("parallel"),
        name="rwkv_prep",
    )(*([proj] * 9 + [proj_z] * 3), lp["cw_rkv"], lp["cw_z"], lp["rw_w0"], lp["w2_pad"], lp["rw_a0"], lp["a2_pad"],
      lp["g2_pad"], lp["rw_k_k"].reshape(1, C), lp["rw_k_a"].reshape(1, C), lp["rw_r_k"].reshape(1, C))


def _rw_scan_kernel(*refs, has_s0, nc):
    if has_s0:
        s0_ref, refs = refs[0], refs[1:]
    (rf_ref, kkf_ref, vf_ref, rb_ref, kkb_ref, vb_ref, lwf_ref, bf_ref, kdf_ref, lwb_ref, bb_ref, kdb_ref,
     yf_ref, yb_ref, so_ref, st_ref) = refs
    C, N = SCAN_CHUNK, RW_HEAD
    W = 2 * N
    npair = rf_ref.shape[1] // W
    c = pl.program_id(1)
    zero = jnp.zeros((), BF16)

    row = lax.broadcasted_iota(jnp.int32, (W, W), 0)
    col = lax.broadcasted_iota(jnp.int32, (W, W), 1)
    same_head = (row // N) == (col // N)
    eye = (row == col).astype(F32)

    @pl.when(c == 0)
    def _():
        if has_s0:
            sel = (lax.broadcasted_iota(jnp.int32, (N, W), 0) == lax.broadcasted_iota(jnp.int32, (N, W), 1) % N)
            for d in range(2):
                for p in range(npair):
                    tiled = _mmh(s0_ref[d, 2 * p:2 * p + 2].reshape(W, N), sel.astype(F32))
                    st_ref[d * npair + p] = jnp.where(same_head, tiled, 0.0)
        else:
            st_ref[...] = jnp.zeros(st_ref.shape, F32)

    def order(shape, dim, bwd):
        t = lax.broadcasted_iota(jnp.int32, shape, dim) % C
        return C - 1 - t if bwd else t

    chains = []
    for d, (r_ref, kk_ref, v_ref, lw_ref, b_ref, kd_ref, y_ref) in enumerate(
            [(rf_ref, kkf_ref, vf_ref, lwf_ref, bf_ref, kdf_ref, yf_ref),
             (rb_ref, kkb_ref, vb_ref, lwb_ref, bb_ref, kdb_ref, yb_ref)]):
        rt, ct = order((W, W), 0, d == 1), order((W, W), 1, d == 1)
        masks = dict(strict=rt > ct, incl=rt >= ct)
        masks.update({s: ((rt // (2 * s)) == (ct // (2 * s))) & ((rt // s) % 2 == 1) & ((ct // s) % 2 == 0)
                      for s in (1, 2, 4, 8, 16, 32)})
        lw = lw_ref[...]
        g_in = _cumsum_rows(order((C, C), 0, d == 1) >= order((C, C), 1, d == 1), lw)
        g_tot = jnp.sum(lw, axis=0, keepdims=True)
        e_neg = jnp.exp(-g_in)
        e_rem = jnp.exp(g_tot - g_in)
        tok = dict(
            kk=(kk_ref[...] * jnp.exp(g_in - lw)).astype(BF16), r=(r_ref[...] * jnp.exp(g_in)).astype(BF16),
            b=(b_ref[...] * e_neg).astype(BF16), kd=(kd_ref[...] * e_neg).astype(BF16),
            b_end=(b_ref[...] * e_rem).astype(BF16), kd_end=(kd_ref[...] * e_rem).astype(BF16),
            v=v_ref[...].astype(BF16), e_tot=jnp.exp(g_tot))
        for p in range(npair):
            chains.append((d * npair + p, slice(p * W, (p + 1) * W), tok, masks, y_ref))

    def expand(x, sl):
        return jnp.where(same_head, jnp.concatenate([x[:, sl]] * 2, axis=0), zero)

    ak = [expand(t["kk"], sl) for _, sl, t, _, _ in chains]
    bk = [jnp.concatenate([expand(t["b"], sl), expand(t["kd"], sl)], axis=0) for _, sl, t, _, _ in chains]
    vb = [expand(t["v"], sl) for _, sl, t, _, _ in chains]
    S = [st_ref[i] for i, *_ in chains]
    Sb = [x.astype(BF16) for x in S]
    n = range(len(chains))
    mk = [ch[3] for ch in chains]
    ar = [expand(t["r"], sl) for _, sl, t, _, _ in chains]
    lmn = [_mm1(jnp.concatenate([ak[i], ar[i]], axis=0), bk[i], "nt") for i in n]
    L = [jnp.where(mk[i]["strict"], lmn[i][:W, :W], 0.0) for i in n]
    Lb = [x.astype(BF16) for x in L]
    M = [jnp.where(mk[i]["strict"], lmn[i][:W, W:], 0.0).astype(BF16) for i in n]
    nbk = [jnp.concatenate([jnp.where(mk[i]["incl"], -lmn[i][W:, :W], 0.0),
                            jnp.where(mk[i]["incl"], lmn[i][W:, W:], 0.0)], axis=1).astype(BF16) for i in n]
    rhs = [(_mm1(ak[i], Sb[i], "nt") + _mm1(M[i], vb[i])).astype(BF16) for i in n]
    X = [eye - jnp.where(mk[i][1], L[i], 0.0) for i in n]
    for s in (2, 4, 8, 16, 32):
        Xb = [x.astype(BF16) for x in X]
        t = [_mm1(jnp.where(mk[i][s], Lb[i], zero), Xb[i]).astype(BF16) for i in n]
        X = [X[i] - _mm1(Xb[i], t[i]) for i in n]
    uv = [jnp.concatenate([_mm1(X[i].astype(BF16), rhs[i]).astype(BF16), vb[i]], axis=0) for i in n]
    for i, (slot, sl, tk, _, _) in enumerate(chains):
        ends = jnp.concatenate([-expand(tk["b_end"], sl), expand(tk["kd_end"], sl)], axis=0)
        st_ref[slot] = S[i] * tk["e_tot"][:, sl] + _mm1(uv[i], ends, "tn")
    for i, (_, sl, _, _, y_ref) in enumerate(chains):
        y = _mm1(ar[i], Sb[i], "nt") + _mm1(nbk[i], uv[i])
        y_ref[:, sl] = y[:C] + y[C:]

    @pl.when(c == nc - 1)
    def _():
        fold = (lax.broadcasted_iota(jnp.int32, (W, N), 0) % N == lax.broadcasted_iota(jnp.int32, (W, N), 1))
        for d in range(2):
            for p in range(npair):
                so_ref[d, 2 * p:2 * p + 2] = _mmh(st_ref[d * npair + p], fold.astype(F32)).reshape(2, N, N)


def _rw_scan(r, kk, v, lw, b, kd, *, row0, B, T, s0=None, layer=0):
    C = r.shape[1]
    H = C // RW_HEAD
    CH = SCAN_CHUNK
    nc = T // CH
    blk0 = row0 // CH
    fwd = pl.BlockSpec((CH, C), lambda bi, c: (blk0 + bi * nc + c, 0))
    bwd = pl.BlockSpec((CH, C), lambda bi, c: (blk0 + bi * nc + nc - 1 - c, 0))
    fwd2 = pl.BlockSpec((None, CH, C), lambda bi, c: (0, blk0 + bi * nc + c, 0))
    bwd2 = pl.BlockSpec((None, CH, C), lambda bi, c: (1, blk0 + bi * nc + nc - 1 - c, 0))
    in_specs = [fwd, fwd, fwd, bwd, bwd, bwd, fwd2, fwd2, fwd2, bwd2, bwd2, bwd2]
    args = [r, kk, v, r, kk, v, lw, b, kd, lw, b, kd]
    if s0 is not None:
        in_specs = [pl.BlockSpec((None, None, 2, H, RW_HEAD, RW_HEAD),
                                 lambda bi, c: (bi, layer, 0, 0, 0, 0))] + in_specs
        args = [s0] + args
    y_shape = jax.ShapeDtypeStruct((B * T, C), F32)
    return pl.pallas_call(
        functools.partial(_rw_scan_kernel, has_s0=s0 is not None, nc=nc),
        grid=(B, nc),
        in_specs=in_specs,
        out_specs=[pl.BlockSpec((CH, C), lambda bi, c: (bi * nc + c, 0)),
                   pl.BlockSpec((CH, C), lambda bi, c: (bi * nc + nc - 1 - c, 0)),
                   pl.BlockSpec((None, 2, H, RW_HEAD, RW_HEAD), lambda bi, c: (bi, 0, 0, 0, 0))],
        out_shape=[y_shape, y_shape, jax.ShapeDtypeStruct((B, 2, H, RW_HEAD, RW_HEAD), F32)],
        scratch_shapes=[pltpu.VMEM((H, 2 * RW_HEAD, 2 * RW_HEAD), F32)],
        compiler_params=_params("parallel", "arbitrary"),
        name="rwkv_scan_ctx" if s0 is not None else "rwkv_scan",
    )(*args)


def _rw_post_kernel(yf_ref, yb_ref, bonus_ref, gate_ref, g_ref, b_ref, _mix_ref, o_ref):
    y = yf_ref[...] + yb_ref[...]
    inv_n = 1.0 / RW_HEAD
    mu = _head_sum(y) * inv_n
    yc = y - mu
    var = _head_sum(yc * yc) * inv_n
    yn = yc * lax.rsqrt(var + RW_GN_EPS) * g_ref[...] + b_ref[...]
    o_ref[...] = ((yn + bonus_ref[...]) * gate_ref[...]).astype(o_ref.dtype)


def _rw_post(y_f, y_b, bonus, gate, mix, gn_g, gn_b, *, row0, out_blk, tm=ROW_TM):
    n_rows, C = y_f.shape
    r0 = row0 // tm
    own = pl.BlockSpec((tm, C), lambda i: (i, 0))
    tok = pl.BlockSpec((tm, C), lambda i: (r0 + i, 0))
    vec = pl.BlockSpec((1, C), lambda i: (0, 0))
    return pl.pallas_call(
        _rw_post_kernel,
        grid=(n_rows // tm,),
        in_specs=[own, own, tok, tok, vec, vec, _any_spec()],
        out_specs=pl.BlockSpec((tm, C), lambda i: (r0 + i, out_blk)),
        out_shape=jax.ShapeDtypeStruct(mix.shape, mix.dtype),
        input_output_aliases={6: 0},
        compiler_params=_params("parallel"),
        name="rwkv_post",
    )(y_f, y_b, bonus, gate, gn_g.reshape(1, C), gn_b.reshape(1, C), mix)


def _pad_rows(w, rows, at):
    return jnp.zeros((rows, w.shape[1]), w.dtype).at[at:at + w.shape[0]].set(w)


def _uvz_weights(w_in_b):
    L, D, _ = w_in_b.shape
    o_z = 3 * (D // 2) + 3 * (D // 4)
    n_z = 4 * RW_LORA_R + RW_GATE_R
    pad = jnp.zeros((L, D, 4 * LANES - n_z), BF16)
    return jnp.concatenate([w_in_b[:, :, o_z + n_z:], w_in_b[:, :, o_z:o_z + n_z], pad], axis=2)


def _layer_weights(l, D, rw_conv_w, rw_w2, rw_a2, rw_g2):
    C = D // 4
    zpad = 4 * LANES - (4 * RW_LORA_R + RW_GATE_R)
    cw = rw_conv_w[l]
    return dict(
        cw_rkv=cw[:, :3 * C],
        cw_z=jnp.concatenate([cw[:, 3 * C:], jnp.zeros((CONV_W, zpad), F32)], axis=1),
        w2_pad=jnp.stack([_pad_rows(rw_w2[l, d], LANES, d * RW_LORA_R) for d in range(2)]),
        a2_pad=jnp.stack([_pad_rows(rw_a2[l, d], LANES, d * RW_LORA_R) for d in range(2)]),
        g2_pad=_pad_rows(rw_g2[l], 2 * LANES, 0),
    )


def _rope_tables(T):
    n = DA_D // 4
    inv = ROPE_THETA ** (-jnp.arange(n, dtype=F32) / n)
    rows = T // GRID_W
    row = jnp.repeat(jnp.arange(rows), GRID_W).astype(F32)
    col = jnp.tile(jnp.arange(GRID_W), rows).astype(F32)
    sign = jnp.concatenate([-jnp.ones((n,), F32), jnp.ones((n,), F32)])
    cs, sn = [], []
    for pos in (row, col):
        ang = pos[:, None] * inv[None, :]
        cs.append(jnp.concatenate([jnp.cos(ang), jnp.cos(ang)], axis=1))
        sn.append(jnp.concatenate([jnp.sin(ang), jnp.sin(ang)], axis=1) * sign[None, :])
    cos = jnp.concatenate(cs, axis=1)
    sin = jnp.concatenate(sn, axis=1)
    return jnp.tile(cos, (1, 2)), jnp.tile(sin, (1, 2))


def kernel(x_prompt, x_sample, cache_da_k, cache_da_v, state_rwkv, c, c_ctx, mod_w, mod_b, norm1_g, norm2_g, w_in, da_lambda, da_subln_g, rw_conv_w, rw_w0, rw_w2, rw_a0, rw_a2, rw_g2, rw_k_k, rw_k_a, rw_r_k, rw_gn_g, rw_gn_b, cm_norm_g, cm_ws, cm_bs, w_out, ffn_up, ffn_conv_w, ffn_conv_b, ffn_down, final_norm_g):
    Bp, Tp, D = x_prompt.shape
    Bs, Ts, _ = x_sample.shape
    L = mod_w.shape[0]
    past = cache_da_k.shape[2]
    DA = D // 2
    H = DA // (2 * DA_D)
    C = D // 4
    F = ffn_down.shape[1]
    p_tok, s_tok = Bp * Tp, Bs * Ts
    assert Bs + 1 <= SUBLANES and p_tok % Ts == 0

    xs = (x_prompt.reshape(p_tok, D), x_sample.reshape(s_tok, D))
    cond8 = jnp.concatenate([c_ctx[None, :], c, jnp.zeros((SUBLANES - 1 - Bs, D), F32)], axis=0)
    mod = _modulation(cond8, mod_w, mod_b)
    ck4 = cache_da_k.reshape(Bs, L, past, DA)
    cv4 = cache_da_v.reshape(Bs, L, past, DA)
    cos, sin = _rope_tables(Ts)
    ko = jnp.zeros((Bp, L, Tp, DA), F32)
    vo = jnp.zeros((Bp, L, Tp, DA), F32)
    w_in_b, w_out_b, ffn_down_b = w_in.astype(BF16), w_out.astype(BF16), ffn_down.astype(BF16)
    n_qkv_rkv = 3 * DA + 3 * C
    w_uvz = _uvz_weights(w_in_b)

    new_s = []
    for l in range(L):
        lw_ = _layer_weights(l, D, rw_conv_w, rw_w2, rw_a2, rw_g2)
        lp = dict(lw_, rw_w0=rw_w0[l], rw_a0=rw_a0[l], rw_k_k=rw_k_k[l], rw_k_a=rw_k_a[l], rw_r_k=rw_r_k[l])
        lam_init = 0.8 - 0.6 * math.exp(-0.3 * l)

        h = _norm_mod(xs, norm1_g[l], mod, l, 0, 1, p_tok, Ts)
        proj = _matmul(h, w_in_b, layer=l, n_cols=n_qkv_rkv, name="proj_in")
        proj_uz = _matmul(h, w_uvz, layer=l, name="proj_in_uvz")
        ko, vo = _store_kv(proj, ko, vo, l, Bp=Bp, Tp=Tp, DA=DA)

        mix = jnp.zeros((p_tok + s_tok, D), BF16)
        mix = _attention(proj, mix, da_lambda[l], da_subln_g[l], lam_init, row0=0, B=Bp, T=Tp, H=H)
        mix = _attention(proj, mix, da_lambda[l], da_subln_g[l], lam_init, row0=p_tok, B=Bs, T=Ts, H=H,
                         ctx=(ck4, cv4, l, cos, sin))
        r_, kk_, v_, lg_, b_, kd_, gate_, bonus_ = _rw_prep(proj, proj_uz, lp, r_blk=3 * DA // C,
                                                            z_blk=2 * C // (4 * LANES), p_tok=p_tok, tp=Tp, ts=Ts)
        yf_p, yb_p, s_p = _rw_scan(r_, kk_, v_, lg_, b_, kd_, row0=0, B=Bp, T=Tp)
        yf_s, yb_s, _ = _rw_scan(r_, kk_, v_, lg_, b_, kd_, row0=p_tok, B=Bs, T=Ts, s0=state_rwkv, layer=l)
        mix = _rw_post(yf_p, yb_p, bonus_, gate_, mix, rw_gn_g[l], rw_gn_b[l], row0=0, out_blk=DA // C)
        mix = _rw_post(yf_s, yb_s, bonus_, gate_, mix, rw_gn_g[l], rw_gn_b[l], row0=p_tok, out_blk=DA // C)
        mix = _chunk_mlp(proj_uz, mix, cm_norm_g[l], cm_ws[l], cm_bs[l], u_blk=0, out_blk=DA // C + 1)

        x = _matmul(mix, w_out_b, layer=l, resid=(xs, mod, l, 2 * D // MM_TN, p_tok, Ts), name="proj_out")
        xs = (x,)

        h = _norm_mod(xs, norm2_g[l], mod, l, 3, 4, p_tok, Ts)
        up = _matmul(h, ffn_up, layer=l, name="ffn_up")
        act = _ffn_act(up, ffn_conv_w[l], ffn_conv_b[l], p_tok=p_tok, tp=Tp, ts=Ts)
        x = _matmul(act, ffn_down_b, layer=l, tk=F // 2, resid=(xs, mod, l, 5 * D // MM_TN, p_tok, Ts),
                    name="ffn_down")
        xs = (x,)

        new_s.append(s_p)

    y_p = _final_norm(x, final_norm_g, row0=0, n_rows=p_tok)
    y_s = _final_norm(x, final_norm_g, row0=p_tok, n_rows=s_tok)
    return (y_p.reshape(Bp, Tp, D), y_s.reshape(Bs, Ts, D), ko.reshape(Bp, L, Tp, H, 2, DA_D),
            vo.reshape(Bp, L, Tp, H, 2 * DA_D), jnp.stack(new_s, axis=1))
```
